```python
import jax, jax.numpy as jnp
from jax import lax
import numpy as np

D_MODEL = 1024
BATCH = 32
SEQ = 2048
DEPTH = 2
DEC_BATCH = 8
DEC_SEQ = 32
PAST_LEN = 1024

CHUNK = 64
HEAD_DIM = 64
N_HEADS = D_MODEL // HEAD_DIM
HA = N_HEADS // 2
HB = N_HEADS - HA
HC = N_HEADS
A_PAST_CHUNKS = 8
A_PAST = A_PAST_CHUNKS * CHUNK
A_BAND = A_PAST + CHUNK
REL_CLIP = 128
N_REL = 2 * REL_CLIP + 1
Q_BLOCK = 128
N_MEM = 256
XA_HEADS = 4
XA_HEAD_DIM = D_MODEL // XA_HEADS
D_FF = -(-8 * D_MODEL // (3 * 256)) * 256
N_EVEN = (DEPTH + 1) // 2
N_ODD = DEPTH // 2
FORGET_BIAS = 2.0
RMS_EPS = 1e-6
NEG_INF = -1e30

kernel_name = 'hybrid_stream_encoder_step'


def rmsnorm(x, g):
    xf = x.astype(jnp.float32)
    y = xf * lax.rsqrt(jnp.mean(xf * xf, axis=-1, keepdims=True) + RMS_EPS)
    return (y * g.astype(jnp.float32)).astype(x.dtype)


def to_blocks(x, size):
    b, s = x.shape[:2]
    return jnp.moveaxis(x.reshape((b, s // size, size) + x.shape[2:]), 1, 0)


def from_blocks(x):
    n, b, size = x.shape[:3]
    return jnp.moveaxis(x, 0, 1).reshape((b, n * size) + x.shape[3:])


def merge_heads(*outs):
    o = jnp.concatenate(outs, axis=2)
    return o.reshape(o.shape[0], o.shape[1], -1)


def rel_bias_table(rel_bias, q_loc, k_loc):
    rel = jnp.clip(q_loc[:, None] - k_loc[None, :], -REL_CLIP, REL_CLIP) + REL_CLIP
    return jnp.moveaxis(rel_bias[rel].astype(jnp.float32), -1, 0)


def band_attend(q, k, v, bias, valid=None):
    s = jnp.einsum('bqhd,bkhd->bhqk', q, k).astype(jnp.float32) * (HEAD_DIM ** -0.5) + bias
    if valid is not None:
        s = jnp.where(valid, s, NEG_INF)
    p = jax.nn.softmax(s, axis=-1)
    return jnp.einsum('bhqk,bkhd->bqhd', p.astype(v.dtype), v)


def chunk_band_attn_prompt(q, k, v, rel_bias):
    s_len = q.shape[1]
    pad = ((0, 0), (A_PAST, 0), (0, 0), (0, 0))
    kp, vp = jnp.pad(k, pad), jnp.pad(v, pad)
    k_loc = jnp.arange(A_BAND)
    bias = rel_bias_table(rel_bias, A_PAST + jnp.arange(CHUNK), k_loc)

    def one_chunk(args):
        c, qc = args
        start = c * CHUNK
        kb = lax.dynamic_slice_in_dim(kp, start, A_BAND, axis=1)
        vb = lax.dynamic_slice_in_dim(vp, start, A_BAND, axis=1)
        valid = (start - A_PAST + k_loc) >= 0
        return band_attend(qc, kb, vb, bias, valid)

    out = lax.map(one_chunk, (jnp.arange(s_len // CHUNK), to_blocks(q, CHUNK)))
    return from_blocks(out)


def chunk_band_attn_sample(q, k_new, v_new, k_cache, v_cache, rel_bias):
    n_past, t = k_cache.shape[1], q.shape[1]
    k = jnp.concatenate([k_cache, k_new], axis=1)
    v = jnp.concatenate([v_cache, v_new], axis=1)
    bias = rel_bias_table(rel_bias, n_past + jnp.arange(t), jnp.arange(n_past + t))
    return band_attend(q, k, v, bias)


def stick_break_attend(q, k, v, q_pos, k_pos):
    z = jnp.einsum('bqhd,bkhd->bhqk', q, k).astype(jnp.float32) * (HEAD_DIM ** -0.5)
    before = k_pos[None, :] < q_pos[:, None]
    log_keep = jnp.where(before, jax.nn.log_sigmoid(-z), 0.0)
    log_later = lax.cumsum(log_keep, axis=3, reverse=True) - log_keep
    w = jnp.where(before, jnp.exp(jax.nn.log_sigmoid(z) + log_later), 0.0)
    return jnp.einsum('bhqk,bkhd->bqhd', w.astype(v.dtype), v)


def stick_break_prompt(q, k, v):
    s_len = q.shape[1]
    k_pos = jnp.arange(s_len)

    def one_block(args):
        i, qb = args
        return stick_break_attend(qb, k, v, i * Q_BLOCK + jnp.arange(Q_BLOCK), k_pos)

    return from_blocks(lax.map(one_block, (jnp.arange(s_len // Q_BLOCK), to_blocks(q, Q_BLOCK))))


def stick_break_sample(q, k_new, v_new, k_cache, v_cache):
    n_past, t = k_cache.shape[1], q.shape[1]
    k = jnp.concatenate([k_cache, k_new], axis=1)
    v = jnp.concatenate([v_cache, v_new], axis=1)
    return stick_break_attend(q, k, v, n_past + jnp.arange(t), jnp.arange(n_past + t))


def forget_attend(q, k, v, fq, fk, q_pos, k_pos):
    decay = jnp.swapaxes(fq, 1, 2)[..., :, None] - jnp.swapaxes(fk, 1, 2)[..., None, :]
    s = jnp.einsum('bqhd,bkhd->bhqk', q, k).astype(jnp.float32) * (HEAD_DIM ** -0.5) + decay
    s = jnp.where(k_pos[None, :] <= q_pos[:, None], s, NEG_INF)
    p = jax.nn.softmax(s, axis=-1)
    return jnp.einsum('bhqk,bkhd->bqhd', p.astype(v.dtype), v)


def forget_attn_prompt(q, k, v, log_f):
    s_len = q.shape[1]
    cum = jnp.cumsum(log_f.astype(jnp.float32), axis=1)
    k_pos = jnp.arange(s_len)

    def one_block(args):
        i, qb, fb = args
        return forget_attend(qb, k, v, fb, cum, i * Q_BLOCK + jnp.arange(Q_BLOCK), k_pos)

    blocks = (jnp.arange(s_len // Q_BLOCK), to_blocks(q, Q_BLOCK), to_blocks(cum, Q_BLOCK))
    return from_blocks(lax.map(one_block, blocks))


def forget_attn_sample(q, k_new, v_new, lf_new, k_cache, v_cache, lf_cache):
    n_past, t = k_cache.shape[1], q.shape[1]
    k = jnp.concatenate([k_cache, k_new], axis=1)
    v = jnp.concatenate([v_cache, v_new], axis=1)
    lf = jnp.concatenate([lf_cache.astype(jnp.float32), lf_new.astype(jnp.float32)], axis=1)
    cum = jnp.cumsum(lf, axis=1)
    return forget_attend(q, k, v, cum[:, n_past:], cum, n_past + jnp.arange(t), jnp.arange(n_past + t))


def proj_ab(h, w_in):
    b, t, _ = h.shape
    wa, wb = HA * HEAD_DIM, HB * HEAD_DIM
    cuts = [int(c) for c in np.cumsum([wa, wa, wa, wb, wb])]
    parts = jnp.split(h @ w_in, cuts, axis=-1)
    heads = [HA, HA, HA, HB, HB, HB]
    return [p.reshape(b, t, n, HEAD_DIM) for p, n in zip(parts, heads)]


def proj_c(h, w_in, b_f):
    b, t, _ = h.shape
    w = HC * HEAD_DIM
    z = h @ w_in
    q = z[..., :w].reshape(b, t, HC, HEAD_DIM)
    k = z[..., w:2 * w].reshape(b, t, HC, HEAD_DIM)
    v = z[..., 2 * w:3 * w].reshape(b, t, HC, HEAD_DIM)
    log_f = jax.nn.log_sigmoid(z[..., 3 * w:].astype(jnp.float32) + b_f.astype(jnp.float32))
    return q, k, v, log_f


def memory_kv(mem, g, w_k, w_v):
    b = mem.shape[0]
    m = rmsnorm(mem, g)
    mk = (m @ w_k).reshape(b, N_MEM, XA_HEADS, XA_HEAD_DIM)
    mv = (m @ w_v).reshape(b, N_MEM, XA_HEADS, XA_HEAD_DIM)
    return mk, mv


def memory_attn(h, mk, mv, w_q, w_o):
    b, t, _ = h.shape
    q = (h @ w_q).reshape(b, t, XA_HEADS, XA_HEAD_DIM)
    s = jnp.einsum('bqhd,bkhd->bhqk', q, mk).astype(jnp.float32) * (XA_HEAD_DIM ** -0.5)
    p = jax.nn.softmax(s, axis=-1)
    o = jnp.einsum('bhqk,bkhd->bqhd', p.astype(mv.dtype), mv)
    return o.reshape(b, t, D_MODEL) @ w_o


def swiglu(h, w_gate, w_up, w_down):
    return (jax.nn.silu(h @ w_gate) * (h @ w_up)) @ w_down


def setup_inputs(seed: int = 0) -> dict:
    key = jax.random.key(seed)
    ks = iter(jax.random.split(key, 32))

    def nrm(shape, scale=1.0):
        return scale * jax.random.normal(next(ks), shape, jnp.float32)

    a_rows = min(A_PAST, PAST_LEN)
    d_in = D_MODEL ** -0.5
    inp = {}
    inp['x_prompt'] = nrm((BATCH, SEQ, D_MODEL))
    inp['x_sample'] = nrm((DEC_BATCH, DEC_SEQ, D_MODEL))
    inp['cache_a_k'] = nrm((N_EVEN, DEC_BATCH, a_rows, HA, HEAD_DIM))
    inp['cache_a_v'] = nrm((N_EVEN, DEC_BATCH, a_rows, HA, HEAD_DIM))
    inp['cache_b_k'] = nrm((N_EVEN, DEC_BATCH, PAST_LEN, HB, HEAD_DIM))
    inp['cache_b_v'] = nrm((N_EVEN, DEC_BATCH, PAST_LEN, HB, HEAD_DIM))
    inp['cache_c_k'] = nrm((N_ODD, DEC_BATCH, PAST_LEN, HC, HEAD_DIM))
    inp['cache_c_v'] = nrm((N_ODD, DEC_BATCH, PAST_LEN, HC, HEAD_DIM))
    inp['cache_c_logf'] = jax.nn.log_sigmoid(FORGET_BIAS + nrm((N_ODD, DEC_BATCH, PAST_LEN, HC)))
    inp['cache_mem_k'] = nrm((DEPTH, DEC_BATCH, N_MEM, XA_HEADS, XA_HEAD_DIM))
    inp['cache_mem_v'] = nrm((DEPTH, DEC_BATCH, N_MEM, XA_HEADS, XA_HEAD_DIM))
    inp['mem_prompt'] = nrm((BATCH, N_MEM, D_MODEL))
    inp['w_in_ab'] = nrm((N_EVEN, D_MODEL, 3 * (HA + HB) * HEAD_DIM), d_in)
    inp['w_out_ab'] = nrm((N_EVEN, (HA + HB) * HEAD_DIM, D_MODEL), ((HA + HB) * HEAD_DIM) ** -0.5)
    inp['rel_bias_a'] = nrm((N_EVEN, N_REL, HA), 0.1)
    inp['w_in_c'] = nrm((N_ODD, D_MODEL, 3 * HC * HEAD_DIM + HC), d_in)
    inp['b_f_c'] = FORGET_BIAS + nrm((N_ODD, HC), 0.1)
    inp['w_out_c'] = nrm((N_ODD, HC * HEAD_DIM, D_MODEL), (HC * HEAD_DIM) ** -0.5)
    inp['g_mix'] = 1.0 + nrm((DEPTH, D_MODEL), 0.05)
    inp['g_xattn'] = 1.0 + nrm((DEPTH, D_MODEL), 0.05)
    inp['g_mem'] = 1.0 + nrm((DEPTH, D_MODEL), 0.05)
    inp['w_xq'] = nrm((DEPTH, D_MODEL, D_MODEL), d_in)
    inp['w_xk'] = nrm((DEPTH, D_MODEL, D_MODEL), d_in)
    inp['w_xv'] = nrm((DEPTH, D_MODEL, D_MODEL), d_in)
    inp['w_xo'] = nrm((DEPTH, D_MODEL, D_MODEL), d_in)
    inp['g_ffn'] = 1.0 + nrm((DEPTH, D_MODEL), 0.05)
    inp['w_gate'] = nrm((DEPTH, D_MODEL, D_FF), d_in)
    inp['w_up'] = nrm((DEPTH, D_MODEL, D_FF), d_in)
    inp['w_down'] = nrm((DEPTH, D_FF, D_MODEL), D_FF ** -0.5)
    inp['g_final'] = 1.0 + nrm((D_MODEL,), 0.05)
    return inp


def reference(x_prompt, x_sample, cache_a_k, cache_a_v, cache_b_k, cache_b_v, cache_c_k, cache_c_v,
              cache_c_logf, cache_mem_k, cache_mem_v, mem_prompt, w_in_ab, w_out_ab, rel_bias_a,
              w_in_c, b_f_c, w_out_c, g_mix, g_xattn, g_mem, w_xq, w_xk, w_xv, w_xo, g_ffn,
              w_gate, w_up, w_down, g_final):
    xp, xs = x_prompt, x_sample
    a_kp, a_vp, b_kp, b_vp, a_ks, a_vs, b_ks, b_vs = [], [], [], [], [], [], [], []
    c_kp, c_vp, c_lfp, c_ks, c_vs, c_lfs = [], [], [], [], [], []
    mem_kp, mem_vp = [], []
    for layer in range(DEPTH):
        if layer % 2 == 0:
            e = layer // 2
            qa, ka, va, qb, kb, vb = proj_ab(rmsnorm(xp, g_mix[layer]), w_in_ab[e])
            o = merge_heads(chunk_band_attn_prompt(qa, ka, va, rel_bias_a[e]),
                            stick_break_prompt(qb, kb, vb))
            xp = xp + o @ w_out_ab[e]
            keep = min(A_PAST, xp.shape[1])
            a_kp.append(ka[:, -keep:]); a_vp.append(va[:, -keep:])
            b_kp.append(kb); b_vp.append(vb)
            qa, ka, va, qb, kb, vb = proj_ab(rmsnorm(xs, g_mix[layer]), w_in_ab[e])
            o = merge_heads(chunk_band_attn_sample(qa, ka, va, cache_a_k[e], cache_a_v[e], rel_bias_a[e]),
                            stick_break_sample(qb, kb, vb, cache_b_k[e], cache_b_v[e]))
            xs = xs + o @ w_out_ab[e]
            a_ks.append(ka); a_vs.append(va); b_ks.append(kb); b_vs.append(vb)
        else:
            c = layer // 2
            q, k, v, lf = proj_c(rmsnorm(xp, g_mix[layer]), w_in_c[c], b_f_c[c])
            xp = xp + merge_heads(forget_attn_prompt(q, k, v, lf)) @ w_out_c[c]
            c_kp.append(k); c_vp.append(v); c_lfp.append(lf)
            q, k, v, lf = proj_c(rmsnorm(xs, g_mix[layer]), w_in_c[c], b_f_c[c])
            o = forget_attn_sample(q, k, v, lf, cache_c_k[c], cache_c_v[c], cache_c_logf[c])
            xs = xs + merge_heads(o) @ w_out_c[c]
            c_ks.append(k); c_vs.append(v); c_lfs.append(lf)
        mk, mv = memory_kv(mem_prompt, g_mem[layer], w_xk[layer], w_xv[layer])
        mem_kp.append(mk); mem_vp.append(mv)
        xp = xp + memory_attn(rmsnorm(xp, g_xattn[layer]), mk, mv, w_xq[layer], w_xo[layer])
        xs = xs + memory_attn(rmsnorm(xs, g_xattn[layer]), cache_mem_k[layer], cache_mem_v[layer],
                              w_xq[layer], w_xo[layer])
        xp = xp + swiglu(rmsnorm(xp, g_ffn[layer]), w_gate[layer], w_up[layer], w_down[layer])
        xs = xs + swiglu(rmsnorm(xs, g_ffn[layer]), w_gate[layer], w_up[layer], w_down[layer])
    y_prompt = rmsnorm(xp, g_final)
    y_sample = rmsnorm(xs, g_final)
    return (y_prompt, y_sample,
            jnp.stack(a_kp), jnp.stack(a_vp), jnp.stack(b_kp), jnp.stack(b_vp),
            jnp.stack(c_kp), jnp.stack(c_vp), jnp.stack(c_lfp),
            jnp.stack(mem_kp), jnp.stack(mem_vp),
            jnp.stack(a_ks), jnp.stack(a_vs), jnp.stack(b_ks), jnp.stack(b_vs),
            jnp.stack(c_ks), jnp.stack(c_vs), jnp.stack(c_lfs))
```

```python
import functools

import jax
import jax.numpy as jnp
from jax import lax
from jax.experimental import pallas as pl
from jax.experimental.pallas import tpu as pltpu

F32 = jnp.float32
BF16 = jnp.bfloat16

RMS_EPS = 1e-6
NEG_INF = -1e30
FORGET_BIAS = 2.0
HEAD_DIM = 64
CHUNK = 64
A_PAST_CHUNKS = 8
A_PAST = A_PAST_CHUNKS * CHUNK
REL_CLIP = 128
XA_HEADS = 4

LANES = 128
HEADS_PER_VREG = LANES // HEAD_DIM
VMEM_LIMIT = 56 * 1024 * 1024


def _cparams(*sem):
    return pltpu.CompilerParams(dimension_semantics=sem, vmem_limit_bytes=VMEM_LIMIT)


def _rms_bf16(x, g):
    y = x * lax.rsqrt(jnp.mean(x * x, axis=-1, keepdims=True) + RMS_EPS)
    return (y * g).astype(BF16)


def _log_sigmoid(z):
    return jnp.minimum(z, 0.0) - jnp.log1p(jnp.exp(-jnp.abs(z)))


def _split3(x):
    hi = x.astype(BF16)
    r = x - hi.astype(F32)
    mid = r.astype(BF16)
    lo = (r - mid.astype(F32)).astype(BF16)
    return hi, mid, lo


def _split2(x):
    hi = x.astype(BF16)
    lo = (x - hi.astype(F32)).astype(BF16)
    return hi, lo


def _dot(a, b):
    return jnp.dot(a, b, preferred_element_type=F32)


def _dot_nt(a, b):
    return lax.dot_general(a, b, (((1,), (1,)), ((), ())), preferred_element_type=F32)


def _norm_proj_kernel(x_ref, g_ref, b_ref, *refs, n_w, outs, tiles_per_seq):
    w_refs, o_refs = refs[:n_w], refs[n_w:]
    h = _rms_bf16(x_ref[...], g_ref[...])
    ys = {}
    for o_ref, (grp, kind, arg) in zip(o_refs, outs):
        if grp not in ys:
            ys[grp] = _dot(h, w_refs[grp][...])
        y = ys[grp]
        if kind == 'f32':
            o_ref[...] = y
        elif kind == 'bf16':
            o_ref[...] = (y * arg).astype(BF16)
        elif kind == 'logf':
            o_ref[...] = _log_sigmoid(y + b_ref[...])
        else:
            @pl.when(pl.program_id(0) % tiles_per_seq == tiles_per_seq - 1)
            def _(o_ref=o_ref, y=y):
                o_ref[...] = y


def norm_proj(x, g, ws, outs, *, seq, tm, bias=None):
    m, d = x.shape
    tiles_per_seq = max(seq // tm, 1)
    if bias is None:
        bias = jnp.zeros((1, 16), F32)
    in_specs = [pl.BlockSpec((tm, d), lambda i: (i, 0)),
                pl.BlockSpec((1, d), lambda i: (0, 0)),
                pl.BlockSpec(bias.shape, lambda i: (0, 0))]
    in_specs += [pl.BlockSpec(w.shape, lambda i: (0, 0)) for w in ws]
    out_shape, out_specs = [], []
    for grp, kind, arg in outs:
        n = ws[grp].shape[1]
        if kind == 'tail':
            assert arg == tm
            out_shape.append(jax.ShapeDtypeStruct((m // tiles_per_seq, n), F32))
            out_specs.append(pl.BlockSpec((tm, n), lambda i: (i // tiles_per_seq, 0)))
        else:
            out_shape.append(jax.ShapeDtypeStruct((m, n), BF16 if kind == 'bf16' else F32))
            out_specs.append(pl.BlockSpec((tm, n), lambda i: (i, 0)))
    kern = functools.partial(_norm_proj_kernel, n_w=len(ws), outs=tuple(outs),
                             tiles_per_seq=tiles_per_seq)
    return pl.pallas_call(
        kern, grid=(m // tm,), in_specs=in_specs, out_specs=out_specs, out_shape=out_shape,
        compiler_params=_cparams("arbitrary"), name="norm_proj")(x, g.reshape(1, d), bias, *ws)


def _proj_res_kernel(a_ref, w_ref, r_ref, o_ref):
    o_ref[...] = r_ref[...] + _dot(a_ref[...], w_ref[...])


def proj_res(a, w, res, *, tm):
    m, k = a.shape
    n = w.shape[1]
    return pl.pallas_call(
        _proj_res_kernel, grid=(m // tm,),
        in_specs=[pl.BlockSpec((tm, k), lambda i: (i, 0)),
                  pl.BlockSpec((k, n), lambda i: (0, 0)),
                  pl.BlockSpec((tm, n), lambda i: (i, 0))],
        out_specs=pl.BlockSpec((tm, n), lambda i: (i, 0)),
        out_shape=jax.ShapeDtypeStruct((m, n), F32),
        compiler_params=_cparams("parallel"), name="proj_res")(a, w, res)


def _bias_band_kernel(rb_ref, o_ref, *, n_rel):
    h = pl.program_id(0)
    rows, cols = o_ref.shape[1], o_ref.shape[2]
    i = lax.broadcasted_iota(jnp.int32, (rows, cols), 0)
    r = lax.broadcasted_iota(jnp.int32, (rows, cols), 1) + pl.program_id(1) * cols
    d = jnp.clip(A_PAST + i - r, -REL_CLIP, REL_CLIP) + REL_CLIP

    def body(u, tbl):
        return jnp.where(d == u, rb_ref[h, u], tbl)

    o_ref[0] = lax.fori_loop(0, n_rel, body, jnp.zeros((rows, cols), F32))


def bias_band(rel_bias, rows, cols):
    n_rel, n_heads = rel_bias.shape
    return pl.pallas_call(
        functools.partial(_bias_band_kernel, n_rel=n_rel),
        grid=(n_heads, cols // LANES),
        in_specs=[pl.BlockSpec(memory_space=pltpu.SMEM)],
        out_specs=pl.BlockSpec((1, rows, LANES), lambda h, c: (h, 0, c)),
        out_shape=jax.ShapeDtypeStruct((n_heads, rows, cols), F32),
        compiler_params=_cparams("arbitrary", "arbitrary"), name="bias_band")(rel_bias.T)


def _head_masks(shape):
    lane = lax.broadcasted_iota(jnp.int32, shape, len(shape) - 1)
    return [(lane // HEAD_DIM) == j for j in range(HEADS_PER_VREG)]


def _band_attn_kernel(q_ref, k_ref, v_ref, bias_ref, o_ref, *, tq, band, n_valid_fn):
    c = pl.program_id(1)
    start = pl.multiple_of(jnp.maximum(c - A_PAST_CHUNKS, 0) * tq, tq)
    n_valid = n_valid_fn(c)
    width = q_ref.shape[2]
    col = lax.broadcasted_iota(jnp.int32, (tq, band), 1)
    valid = col < n_valid
    qmask = _head_masks((tq, LANES))
    vmask = _head_masks((band, LANES))
    zero = jnp.zeros((), BF16)
    for hp in range(width // LANES):
        lanes = slice(hp * LANES, (hp + 1) * LANES)
        q = q_ref[0, :, lanes]
        k = k_ref[0, pl.ds(start, band), lanes]
        v = v_ref[0, pl.ds(start, band), lanes]
        acc = jnp.zeros((tq, LANES), F32)
        for j in range(HEADS_PER_VREG):
            s = _dot_nt(jnp.where(qmask[j], q, zero), k) + bias_ref[0, hp * HEADS_PER_VREG + j]
            s = jnp.where(valid, s, NEG_INF)
            p = jnp.exp(s - jnp.max(s, axis=-1, keepdims=True))
            p = p / jnp.sum(p, axis=-1, keepdims=True)
            acc = acc + _dot(p.astype(BF16), jnp.where(vmask[j], v, zero))
        o_ref[0, :, lanes] = acc.astype(BF16)


def band_attn(q, k, v, bias, *, tq, band, n_valid_fn):
    b, sq, w = q.shape
    sk = k.shape[1]
    n_shift, n_heads = bias.shape[:2]
    kern = functools.partial(_band_attn_kernel, tq=tq, band=band, n_valid_fn=n_valid_fn)
    return pl.pallas_call(
        kern, grid=(b, sq // tq),
        in_specs=[pl.BlockSpec((1, tq, w), lambda i, c: (i, c, 0)),
                  pl.BlockSpec((1, sk, w), lambda i, c: (i, 0, 0)),
                  pl.BlockSpec((1, sk, w), lambda i, c: (i, 0, 0)),
                  pl.BlockSpec((1, n_heads, tq, band),
                               lambda i, c: (jnp.minimum(c, n_shift - 1), 0, 0, 0))],
        out_specs=pl.BlockSpec((1, tq, w), lambda i, c: (i, c, 0)),
        out_shape=jax.ShapeDtypeStruct((b, sq, w), BF16),
        compiler_params=_cparams("parallel", "arbitrary"), name="band_attn")(q, k, v, bias)


def _stick_kernel(q_ref, k_ref, v_ref, o_ref, *, tq, tk, q_offset):
    qi = pl.program_id(2)
    q = q_ref[0]
    q_pos = q_offset + qi * tq + lax.broadcasted_iota(jnp.int32, (tq, tk), 0)
    k_iota = lax.broadcasted_iota(jnp.int32, (tq, tk), 1)
    qmask = _head_masks((tq, LANES))
    vmask = _head_masks((tk, LANES))
    zero = jnp.zeros((), BF16)
    rr = lax.broadcasted_iota(jnp.int32, (tk, 2 * tk), 0)
    cc = lax.broadcasted_iota(jnp.int32, (tk, 2 * tk), 1)
    tri = jnp.where((rr > cc) | (cc >= tk), 1.0, 0.0).astype(BF16)
    qs = [jnp.where(qmask[j], q, zero) for j in range(HEADS_PER_VREG)]
    n_blocks = (q_offset + (qi + 1) * tq + tk - 1) // tk

    def body(it, carry):
        acc, later = carry
        kb = n_blocks - 1 - it
        k0 = pl.multiple_of(kb * tk, tk)
        k = k_ref[0, pl.ds(k0, tk), :].astype(BF16)
        v = v_ref[0, pl.ds(k0, tk), :].astype(BF16)
        before = (k0 + k_iota) < q_pos
        new_later = []
        for j in range(HEADS_PER_VREG):
            z = _dot_nt(qs[j], k)
            lk = jnp.where(before, _log_sigmoid(-z), 0.0)
            hi, lo = _split2(lk)
            sums = _dot(hi, tri) + _dot(lo, tri)
            log_w = z + lk + sums[:, :tk] + later[j]
            w = jnp.where(before, jnp.exp(log_w), 0.0)
            acc = acc + _dot(w.astype(BF16), jnp.where(vmask[j], v, zero))
            new_later.append(later[j] + sums[:, tk:])
        return acc, tuple(new_later)

    init = (jnp.zeros((tq, LANES), F32),
            tuple(jnp.zeros((tq, tk), F32) for _ in range(HEADS_PER_VREG)))
    acc, _ = lax.fori_loop(0, n_blocks, body, init)
    o_ref[0] = acc.astype(BF16)


def stick_attn(q, k, v, *, tq, tk, q_offset):
    b, sq, w = q.shape
    sk = k.shape[1]
    kern = functools.partial(_stick_kernel, tq=tq, tk=tk, q_offset=q_offset)
    return pl.pallas_call(
        kern, grid=(b, w // LANES, sq // tq),
        in_specs=[pl.BlockSpec((1, tq, LANES), lambda i, h, t: (i, t, h)),
                  pl.BlockSpec((1, sk, LANES), lambda i, h, t: (i, 0, h)),
                  pl.BlockSpec((1, sk, LANES), lambda i, h, t: (i, 0, h))],
        out_specs=pl.BlockSpec((1, tq, LANES), lambda i, h, t: (i, t, h)),
        out_shape=jax.ShapeDtypeStruct((b, sq, w), BF16),
        compiler_params=_cparams("parallel", "parallel", "arbitrary"), name="stick_attn")(q, k, v)


def _cumsum_kernel(lf_ref, lft_ref, col_ref, row_ref, *, blk):
    s_len, n_heads = lf_ref.shape[1], lf_ref.shape[2]
    rr = lax.broadcasted_iota(jnp.int32, (blk, blk), 0)
    cc = lax.broadcasted_iota(jnp.int32, (blk, blk), 1)
    lower = jnp.where(rr >= cc, 1.0, 0.0).astype(BF16)
    upper = jnp.where(rr <= cc, 1.0, 0.0).astype(BF16)
    carry_col = jnp.zeros((1, n_heads), F32)
    carry_row = jnp.zeros((n_heads, 1), F32)
    for n in range(s_len // blk):
        rows = slice(n * blk, (n + 1) * blk)
        c = sum(_dot(lower, p) for p in _split3(lf_ref[0, rows, :])) + carry_col
        col_ref[0, rows, :] = c
        carry_col = c[blk - 1:blk, :]
        ct = sum(_dot(p, upper) for p in _split3(lft_ref[0, :, rows])) + carry_row
        row_ref[0, :, rows] = ct
        carry_row = ct[:, blk - 1:blk]


def cumsum_logf(lf):
    b, s_len, n_heads = lf.shape
    kern = functools.partial(_cumsum_kernel, blk=LANES)
    return pl.pallas_call(
        kern, grid=(b,),
        in_specs=[pl.BlockSpec((1, s_len, n_heads), lambda i: (i, 0, 0)),
                  pl.BlockSpec((1, n_heads, s_len), lambda i: (i, 0, 0))],
        out_specs=[pl.BlockSpec((1, s_len, n_heads), lambda i: (i, 0, 0)),
                   pl.BlockSpec((1, n_heads, s_len), lambda i: (i, 0, 0))],
        out_shape=[jax.ShapeDtypeStruct((b, s_len, n_heads), F32),
                   jax.ShapeDtypeStruct((b, n_heads, s_len), F32)],
        compiler_params=_cparams("parallel"), name="cumsum_logf")(lf, jnp.swapaxes(lf, 1, 2))


def _forget_kernel(q_ref, k_ref, v_ref, cq_ref, ck_ref, o_ref, *, tq, tk, q_offset):
    hp = pl.program_id(1)
    qi = pl.program_id(2)
    q = q_ref[0]
    q_pos = q_offset + qi * tq + lax.broadcasted_iota(jnp.int32, (tq, tk), 0)
    k_iota = lax.broadcasted_iota(jnp.int32, (tq, tk), 1)
    qmask = _head_masks((tq, LANES))
    vmask = _head_masks((tk, LANES))
    zero = jnp.zeros((), BF16)
    qs = [jnp.where(qmask[j], q, zero) for j in range(HEADS_PER_VREG)]
    cq_all = cq_ref[0]
    head_lane = lax.broadcasted_iota(jnp.int32, cq_all.shape, 1)
    cqs = [jnp.sum(jnp.where(head_lane == hp * HEADS_PER_VREG + j, cq_all, 0.0),
                   axis=-1, keepdims=True) for j in range(HEADS_PER_VREG)]
    n_blocks = (q_offset + (qi + 1) * tq + tk - 1) // tk

    def body(kb, carry):
        accs, ms, ls = carry
        k0 = pl.multiple_of(kb * tk, tk)
        k = k_ref[0, pl.ds(k0, tk), :].astype(BF16)
        v = v_ref[0, pl.ds(k0, tk), :].astype(BF16)
        allowed = (k0 + k_iota) <= q_pos
        new = ([], [], [])
        for j in range(HEADS_PER_VREG):
            ck = ck_ref[0, 0, j:j + 1, pl.ds(k0, tk)]
            s = _dot_nt(qs[j], k) + (cqs[j] - ck)
            s = jnp.where(allowed, s, NEG_INF)
            m_new = jnp.maximum(ms[j], jnp.max(s, axis=-1, keepdims=True))
            alpha = jnp.exp(ms[j] - m_new)
            p = jnp.exp(s - m_new)
            new[0].append(alpha * accs[j] + _dot(p.astype(BF16), jnp.where(vmask[j], v, zero)))
            new[1].append(m_new)
            new[2].append(alpha * ls[j] + jnp.sum(p, axis=-1, keepdims=True))
        return tuple(tuple(x) for x in new)

    init = (tuple(jnp.zeros((tq, LANES), F32) for _ in range(HEADS_PER_VREG)),
            tuple(jnp.full((tq, 1), NEG_INF, F32) for _ in range(HEADS_PER_VREG)),
            tuple(jnp.zeros((tq, 1), F32) for _ in range(HEADS_PER_VREG)))
    accs, _, ls = lax.fori_loop(0, n_blocks, body, init)
    o_ref[0] = sum(accs[j] / ls[j] for j in range(HEADS_PER_VREG)).astype(BF16)


def forget_attn(q, k, v, cum_q, cum_k_rows, *, tq, tk, q_offset):
    b, sq, w = q.shape
    sk = k.shape[1]
    n_heads = cum_q.shape[2]
    ck = cum_k_rows.reshape(b, n_heads // HEADS_PER_VREG, HEADS_PER_VREG, sk)
    kern = functools.partial(_forget_kernel, tq=tq, tk=tk, q_offset=q_offset)
    return pl.pallas_call(
        kern, grid=(b, w // LANES, sq // tq),
        in_specs=[pl.BlockSpec((1, tq, LANES), lambda i, h, t: (i, t, h)),
                  pl.BlockSpec((1, sk, LANES), lambda i, h, t: (i, 0, h)),
                  pl.BlockSpec((1, sk, LANES), lambda i, h, t: (i, 0, h)),
                  pl.BlockSpec((1, tq, n_heads), lambda i, h, t: (i, t, 0)),
                  pl.BlockSpec((1, 1, HEADS_PER_VREG, sk), lambda i, h, t: (i, h, 0, 0))],
        out_specs=pl.BlockSpec((1, tq, LANES), lambda i, h, t: (i, t, h)),
        out_shape=jax.ShapeDtypeStruct((b, sq, w), BF16),
        compiler_params=_cparams("parallel", "parallel", "arbitrary"),
        name="forget_attn")(q, k, v, cum_q, ck)


def _xattn_kernel(x_ref, g_ref, wq_ref, wo_ref, mk_ref, mv_ref, o_ref, *, n_heads):
    x = x_ref[...]
    d_model = x.shape[1]
    hd = d_model // n_heads
    h = _rms_bf16(x, g_ref[...])
    q = (_dot(h, wq_ref[...]) * (hd ** -0.5)).astype(BF16)
    outs = []
    for j in range(n_heads):
        cols = slice(j * hd, (j + 1) * hd)
        s = _dot_nt(q[:, cols], mk_ref[0, :, cols])
        p = jnp.exp(s - jnp.max(s, axis=-1, keepdims=True))
        p = p / jnp.sum(p, axis=-1, keepdims=True)
        outs.append(_dot(p.astype(BF16), mv_ref[0, :, cols]).astype(BF16))
    o = jnp.concatenate(outs, axis=-1)
    o_ref[...] = x + _dot(o, wo_ref[...])


def xattn(x, g, wq, wo, mk, mv, *, seq, tm):
    m, d = x.shape
    n_mem = mk.shape[1]
    tiles_per_seq = seq // tm
    kern = functools.partial(_xattn_kernel, n_heads=XA_HEADS)
    return pl.pallas_call(
        kern, grid=(m // tm,),
        in_specs=[pl.BlockSpec((tm, d), lambda i: (i, 0)),
                  pl.BlockSpec((1, d), lambda i: (0, 0)),
                  pl.BlockSpec((d, d), lambda i: (0, 0)),
                  pl.BlockSpec((d, d), lambda i: (0, 0)),
                  pl.BlockSpec((1, n_mem, d), lambda i: (i // tiles_per_seq, 0, 0)),
                  pl.BlockSpec((1, n_mem, d), lambda i: (i // tiles_per_seq, 0, 0))],
        out_specs=pl.BlockSpec((tm, d), lambda i: (i, 0)),
        out_shape=jax.ShapeDtypeStruct((m, d), F32),
        compiler_params=_cparams("parallel"), name="xattn")(x, g.reshape(1, d), wq, wo, mk, mv)


def _ffn_kernel(x_ref, g_ref, wg_ref, wu_ref, wd_ref, gf_ref, o_ref, *, tf, final_norm):
    x = x_ref[...]
    h = _rms_bf16(x, g_ref[...])
    acc = x
    for c in range(wg_ref.shape[1] // tf):
        cols = slice(c * tf, (c + 1) * tf)
        gate = _dot(h, wg_ref[:, cols])
        up = _dot(h, wu_ref[:, cols])
        a = (gate * jax.nn.sigmoid(gate) * up).astype(BF16)
        acc = acc + _dot(a, wd_ref[cols, :])
    if final_norm:
        acc = acc * lax.rsqrt(jnp.mean(acc * acc, axis=-1, keepdims=True) + RMS_EPS) * gf_ref[...]
    o_ref[...] = acc


def ffn(x, g, wg, wu, wd, g_final, *, tm, tf, final_norm):
    m, d = x.shape
    dff = wg.shape[1]
    kern = functools.partial(_ffn_kernel, tf=tf, final_norm=final_norm)
    return pl.pallas_call(
        kern, grid=(m // tm,),
        in_specs=[pl.BlockSpec((tm, d), lambda i: (i, 0)),
                  pl.BlockSpec((1, d), lambda i: (0, 0)),
                  pl.BlockSpec((d, dff), lambda i: (0, 0)),
                  pl.BlockSpec((d, dff), lambda i: (0, 0)),
                  pl.BlockSpec((dff, d), lambda i: (0, 0)),
                  pl.BlockSpec((1, d), lambda i: (0, 0))],
        out_specs=pl.BlockSpec((tm, d), lambda i: (i, 0)),
        out_shape=jax.ShapeDtypeStruct((m, d), F32),
        compiler_params=_cparams("parallel"),
        name="ffn")(x, g.reshape(1, d), wg, wu, wd, g_final.reshape(1, d))


def _pad_rows(a, rows):
    return jnp.pad(a, ((0, 0), (0, rows - a.shape[1]), (0, 0)))


def _row_tile(m, seq, cap):
    tm = min(cap, seq)
    assert seq % tm == 0 and m % tm == 0
    return tm


def _layer_ab(x, seq, g, w_in, w_out, band_tbl, cache, *, tm):
    m, d = x.shape
    b = m // seq
    wa = w_in.shape[1] // 6
    ws = [w_in[:, n * wa:(n + 1) * wa].astype(BF16) for n in range(6)]
    scale = HEAD_DIM ** -0.5
    keep = min(A_PAST, seq)
    outs = [(0, 'bf16', scale), (1, 'bf16', 1.0), (2, 'bf16', 1.0),
            (1, 'tail' if keep < seq else 'f32', keep), (2, 'tail' if keep < seq else 'f32', keep),
            (3, 'bf16', scale), (4, 'f32', None), (5, 'f32', None)]
    qa, ka16, va16, ka_keep, va_keep, qb, kb, vb = norm_proj(x, g, ws, outs, seq=seq, tm=tm)
    shp = lambda a: a.reshape(b, -1, a.shape[-1])
    if cache is None:
        n_shift = A_PAST_CHUNKS + 1
        band = A_PAST + CHUNK
        bias = jnp.stack([band_tbl[:, :CHUNK, (A_PAST_CHUNKS - s) * CHUNK:][:, :, :band]
                          for s in range(n_shift)])
        oa = band_attn(shp(qa), shp(ka16), shp(va16), bias, tq=CHUNK, band=band,
                       n_valid_fn=lambda c: (jnp.minimum(c, A_PAST_CHUNKS) + 1) * CHUNK)
        ob = stick_attn(shp(qb), shp(kb), shp(vb), tq=128, tk=LANES, q_offset=0)
    else:
        ca_k, ca_v, cb_k, cb_v = cache
        n_past = ca_k.shape[1]
        n_keys = n_past + seq
        band = -(-n_keys // LANES) * LANES
        flat = lambda a: a.reshape(a.shape[0], a.shape[1], -1)
        k_all = _pad_rows(jnp.concatenate([flat(ca_k).astype(BF16), shp(ka16)], axis=1), band)
        v_all = _pad_rows(jnp.concatenate([flat(ca_v).astype(BF16), shp(va16)], axis=1), band)
        bias = band_tbl[None, :, :seq, A_PAST - n_past:][:, :, :, :band]
        oa = band_attn(shp(qa), k_all, v_all, bias, tq=seq, band=band,
                       n_valid_fn=lambda c: n_keys)
        n_pastb = cb_k.shape[1]
        sk = -(-(n_pastb + seq) // LANES) * LANES
        kb_all = _pad_rows(jnp.concatenate([flat(cb_k), shp(kb)], axis=1), sk)
        vb_all = _pad_rows(jnp.concatenate([flat(cb_v), shp(vb)], axis=1), sk)
        ob = stick_attn(shp(qb), kb_all, vb_all, tq=seq, tk=LANES, q_offset=n_pastb)
    o = jnp.concatenate([oa, ob], axis=-1).reshape(m, d)
    x = proj_res(o, w_out.astype(BF16), x, tm=tm)
    return x, (ka_keep, va_keep, kb, vb)


def _layer_c(x, seq, g, w_in, b_f, w_out, cache, *, tm):
    m, d = x.shape
    b = m // seq
    n_heads = b_f.shape[0]
    wq, wk, wv, wf = (w_in[:, :d], w_in[:, d:2 * d], w_in[:, 2 * d:3 * d], w_in[:, 3 * d:])
    ws = [w.astype(BF16) for w in (wq, wk, wv, wf)]
    outs = [(0, 'bf16', HEAD_DIM ** -0.5), (1, 'f32', None), (2, 'f32', None), (3, 'logf', None)]
    q, k, v, lf = norm_proj(x, g, ws, outs, seq=seq, tm=tm, bias=b_f.reshape(1, n_heads))
    shp = lambda a: a.reshape(b, -1, a.shape[-1])
    if cache is None:
        cum_col, cum_row = cumsum_logf(shp(lf))
        o = forget_attn(shp(q), shp(k), shp(v), cum_col, cum_row, tq=128, tk=LANES, q_offset=0)
    else:
        c_k, c_v, c_lf = cache
        n_past = c_k.shape[1]
        sk = -(-(n_past + seq) // LANES) * LANES
        flat = lambda a: a.reshape(a.shape[0], a.shape[1], -1)
        k_all = _pad_rows(jnp.concatenate([flat(c_k), shp(k)], axis=1), sk)
        v_all = _pad_rows(jnp.concatenate([flat(c_v), shp(v)], axis=1), sk)
        lf_all = _pad_rows(jnp.concatenate([c_lf, shp(lf)], axis=1), sk)
        cum_col, cum_row = cumsum_logf(lf_all)
        o = forget_attn(shp(q), k_all, v_all, cum_col[:, n_past:n_past + seq], cum_row,
                        tq=seq, tk=LANES, q_offset=n_past)
    x = proj_res(o.reshape(m, d), w_out.astype(BF16), x, tm=tm)
    return x, (k, v, lf)


def kernel(x_prompt, x_sample, cache_a_k, cache_a_v, cache_b_k, cache_b_v, cache_c_k, cache_c_v, cache_c_logf, cache_mem_k, cache_mem_v, mem_prompt, w_in_ab, w_out_ab, rel_bias_a, w_in_c, b_f_c, w_out_c, g_mix, g_xattn, g_mem, w_xq, w_xk, w_xv, w_xo, g_ffn, w_gate, w_up, w_down, g_final):
    bp, sp, d = x_prompt.shape
    bs, ss, _ = x_sample.shape
    depth = g_mix.shape[0]
    n_mem = mem_prompt.shape[1]
    xp = x_prompt.reshape(bp * sp, d)
    xs = x_sample.reshape(bs * ss, d)
    tmp = _row_tile(bp * sp, sp, 512)
    tms = bs * ss
    assert tms <= 512
    mem = mem_prompt.reshape(bp * n_mem, d)
    tmm = _row_tile(bp * n_mem, n_mem, 512)
    dff = w_gate.shape[2]
    tf = 256 if dff % 256 == 0 else dff

    a_kp, a_vp, b_kp, b_vp, a_ks, a_vs, b_ks, b_vs = [], [], [], [], [], [], [], []
    c_kp, c_vp, c_lfp, c_ks, c_vs, c_lfs = [], [], [], [], [], []
    mem_kp, mem_vp = [], []
    for layer in range(depth):
        if layer % 2 == 0:
            e = layer // 2
            band_cols = -(-(2 * A_PAST + CHUNK) // LANES) * LANES
            band_tbl = bias_band(rel_bias_a[e], CHUNK, band_cols)
            xp, (ka, va, kb, vb) = _layer_ab(xp, sp, g_mix[layer], w_in_ab[e], w_out_ab[e],
                                             band_tbl, None, tm=tmp)
            a_kp.append(ka); a_vp.append(va); b_kp.append(kb); b_vp.append(vb)
            xs, (ka, va, kb, vb) = _layer_ab(xs, ss, g_mix[layer], w_in_ab[e], w_out_ab[e], band_tbl,
                                             (cache_a_k[e], cache_a_v[e], cache_b_k[e], cache_b_v[e]),
                                             tm=tms)
            a_ks.append(ka); a_vs.append(va); b_ks.append(kb); b_vs.append(vb)
        else:
            c = layer // 2
            xp, (k, v, lf) = _layer_c(xp, sp, g_mix[layer], w_in_c[c], b_f_c[c], w_out_c[c], None,
                                      tm=tmp)
            c_kp.append(k); c_vp.append(v); c_lfp.append(lf)
            xs, (k, v, lf) = _layer_c(xs, ss, g_mix[layer], w_in_c[c], b_f_c[c], w_out_c[c],
                                      (cache_c_k[c], cache_c_v[c], cache_c_logf[c]), tm=tms)
            c_ks.append(k); c_vs.append(v); c_lfs.append(lf)
        mk, mv, mk16, mv16 = norm_proj(
            mem, g_mem[layer], [w_xk[layer].astype(BF16), w_xv[layer].astype(BF16)],
            [(0, 'f32', None), (1, 'f32', None), (0, 'bf16', 1.0), (1, 'bf16', 1.0)],
            seq=n_mem, tm=tmm)
        mem_kp.append(mk); mem_vp.append(mv)
        wq16, wo16 = w_xq[layer].astype(BF16), w_xo[layer].astype(BF16)
        xp = xattn(xp, g_xattn[layer], wq16, wo16, mk16.reshape(bp, n_mem, d),
                   mv16.reshape(bp, n_mem, d), seq=sp, tm=tmp)
        xs = xattn(xs, g_xattn[layer], wq16, wo16,
                   cache_mem_k[layer].reshape(bs, n_mem, d).astype(BF16),
                   cache_mem_v[layer].reshape(bs, n_mem, d).astype(BF16), seq=ss, tm=ss)
        last = layer == depth - 1
        wg16, wu16, wd16 = (w_gate[layer].astype(BF16), w_up[layer].astype(BF16),
                            w_down[layer].astype(BF16))
        xp = ffn(xp, g_ffn[layer], wg16, wu16, wd16, g_final, tm=tmp, tf=tf, final_norm=last)
        xs = ffn(xs, g_ffn[layer], wg16, wu16, wd16, g_final, tm=tms, tf=tf, final_norm=last)

    hd = HEAD_DIM
    xa_hd = d // XA_HEADS
    r5 = lambda lst, b, s, dd: jnp.stack([a.reshape(b, s, -1, dd) for a in lst])
    r4 = lambda lst, b, s: jnp.stack([a.reshape(b, s, -1) for a in lst])
    keep = min(A_PAST, sp)
    return (xp.reshape(bp, sp, d), xs.reshape(bs, ss, d),
            r5(a_kp, bp, keep, hd), r5(a_vp, bp, keep, hd), r5(b_kp, bp, sp, hd), r5(b_vp, bp, sp, hd),
            r5(c_kp, bp, sp, hd), r5(c_vp, bp, sp, hd), r4(c_lfp, bp, sp),
            r5(mem_kp, bp, n_mem, xa_hd), r5(mem_vp, bp, n_mem, xa_hd),
            r5(a_ks, bs, ss, hd), r5(a_vs, bs, ss, hd), r5(b_ks, bs, ss, hd), r5(b_vs, bs, ss, hd),
            r5(c_ks, bs, ss, hd), r5(c_vs, bs, ss, hd), r4(c_lfs, bs, ss))
```

```python
import functools

import jax
import jax.numpy as jnp
from jax import lax
from jax.experimental import pallas as pl
from jax.experimental.pallas import tpu as pltpu

F32 = jnp.float32
BF16 = jnp.bfloat16

RMS_EPS = 1e-6
NEG_INF = -1e30
FORGET_BIAS = 2.0
HEAD_DIM = 64
CHUNK = 64
A_PAST_CHUNKS = 8
A_PAST = A_PAST_CHUNKS * CHUNK
REL_CLIP = 128
XA_HEADS = 4

LANES = 128
HEADS_PER_VREG = LANES // HEAD_DIM
KEY_CHUNK = 256
VMEM_LIMIT = 56 * 1024 * 1024


def _cparams(*sem):
    return pltpu.CompilerParams(dimension_semantics=sem, vmem_limit_bytes=VMEM_LIMIT)


def _rms_bf16(x, g):
    y = x * lax.rsqrt(jnp.mean(x * x, axis=-1, keepdims=True) + RMS_EPS)
    return (y * g).astype(BF16)


def _log_sigmoid(z):
    return jnp.minimum(z, 0.0) - jnp.log1p(jnp.exp(-jnp.abs(z)))


def _split3(x):
    hi = x.astype(BF16)
    r = x - hi.astype(F32)
    mid = r.astype(BF16)
    lo = (r - mid.astype(F32)).astype(BF16)
    return hi, mid, lo


def _split2(x):
    hi = x.astype(BF16)
    lo = (x - hi.astype(F32)).astype(BF16)
    return hi, lo


def _dot(a, b):
    return jnp.dot(a, b, preferred_element_type=F32)


def _dot_nt(a, b):
    return lax.dot_general(a, b, (((1,), (1,)), ((), ())), preferred_element_type=F32)


def _norm_proj_kernel(x_ref, g_ref, b_ref, *refs, n_w, outs, tiles_per_seq):
    w_refs, o_refs = refs[:n_w], refs[n_w:]
    h = _rms_bf16(x_ref[...], g_ref[...])
    ys = {}
    for o_ref, (grp, kind, arg) in zip(o_refs, outs):
        if grp not in ys:
            ys[grp] = _dot(h, w_refs[grp][...])
        y = ys[grp]
        if kind == 'f32':
            o_ref[...] = y
        elif kind == 'bf16':
            o_ref[...] = (y * arg).astype(BF16)
        elif kind == 'logf':
            o_ref[...] = _log_sigmoid(y + b_ref[...])
        else:
            @pl.when(pl.program_id(0) % tiles_per_seq == tiles_per_seq - 1)
            def _(o_ref=o_ref, y=y):
                o_ref[...] = y


def norm_proj(x, g, ws, outs, *, seq, tm, bias=None):
    m, d = x.shape
    tiles_per_seq = max(seq // tm, 1)
    if bias is None:
        bias = jnp.zeros((1, 16), F32)
    in_specs = [pl.BlockSpec((tm, d), lambda i: (i, 0)),
                pl.BlockSpec((1, d), lambda i: (0, 0)),
                pl.BlockSpec(bias.shape, lambda i: (0, 0))]
    in_specs += [pl.BlockSpec(w.shape, lambda i: (0, 0)) for w in ws]
    out_shape, out_specs = [], []
    for grp, kind, arg in outs:
        n = ws[grp].shape[1]
        if kind == 'tail':
            assert arg == tm
            out_shape.append(jax.ShapeDtypeStruct((m // tiles_per_seq, n), F32))
            out_specs.append(pl.BlockSpec((tm, n), lambda i: (i // tiles_per_seq, 0)))
        else:
            out_shape.append(jax.ShapeDtypeStruct((m, n), BF16 if kind == 'bf16' else F32))
            out_specs.append(pl.BlockSpec((tm, n), lambda i: (i, 0)))
    kern = functools.partial(_norm_proj_kernel, n_w=len(ws), outs=tuple(outs),
                             tiles_per_seq=tiles_per_seq)
    return pl.pallas_call(
        kern, grid=(m // tm,), in_specs=in_specs, out_specs=out_specs, out_shape=out_shape,
        compiler_params=_cparams("arbitrary"), name="norm_proj")(x, g.reshape(1, d), bias, *ws)


def _proj_res_kernel(a_ref, w_ref, r_ref, o_ref):
    o_ref[...] = r_ref[...] + _dot(a_ref[...], w_ref[...])


def proj_res(a, w, res, *, tm):
    m, k = a.shape
    n = w.shape[1]
    return pl.pallas_call(
        _proj_res_kernel, grid=(m // tm,),
        in_specs=[pl.BlockSpec((tm, k), lambda i: (i, 0)),
                  pl.BlockSpec((k, n), lambda i: (0, 0)),
                  pl.BlockSpec((tm, n), lambda i: (i, 0))],
        out_specs=pl.BlockSpec((tm, n), lambda i: (i, 0)),
        out_shape=jax.ShapeDtypeStruct((m, n), F32),
        compiler_params=_cparams("parallel"), name="proj_res")(a, w, res)


def _bias_band_kernel(rb_ref, o_ref, *, n_rel):
    h = pl.program_id(0)
    rows, cols = o_ref.shape[1], o_ref.shape[2]
    i = lax.broadcasted_iota(jnp.int32, (rows, cols), 0)
    r = lax.broadcasted_iota(jnp.int32, (rows, cols), 1) + pl.program_id(1) * cols
    d = jnp.clip(A_PAST + i - r, -REL_CLIP, REL_CLIP) + REL_CLIP

    def body(u, tbl):
        return jnp.where(d == u, rb_ref[h, u], tbl)

    o_ref[0] = lax.fori_loop(0, n_rel, body, jnp.zeros((rows, cols), F32))


def bias_band(rel_bias, rows, cols):
    n_rel, n_heads = rel_bias.shape
    return pl.pallas_call(
        functools.partial(_bias_band_kernel, n_rel=n_rel),
        grid=(n_heads, cols // LANES),
        in_specs=[pl.BlockSpec(memory_space=pltpu.SMEM)],
        out_specs=pl.BlockSpec((1, rows, LANES), lambda h, c: (h, 0, c)),
        out_shape=jax.ShapeDtypeStruct((n_heads, rows, cols), F32),
        compiler_params=_cparams("arbitrary", "arbitrary"), name="bias_band")(rel_bias.T)


def _head_masks(shape):
    lane = lax.broadcasted_iota(jnp.int32, shape, len(shape) - 1)
    return [(lane // HEAD_DIM) == j for j in range(HEADS_PER_VREG)]


def _band_attn_kernel(q_ref, k_ref, v_ref, bias_ref, o_ref, *, tq, band, n_valid_fn):
    c = pl.program_id(1)
    start = pl.multiple_of(jnp.maximum(c - A_PAST_CHUNKS, 0) * tq, tq)
    n_valid = n_valid_fn(c)
    width = q_ref.shape[2]
    col = lax.broadcasted_iota(jnp.int32, (tq, band), 1)
    valid = col < n_valid
    qmask = _head_masks((tq, LANES))
    vmask = _head_masks((band, LANES))
    zero = jnp.zeros((), BF16)
    for hp in range(width // LANES):
        lanes = slice(hp * LANES, (hp + 1) * LANES)
        q = q_ref[0, :, lanes]
        k = k_ref[0, pl.ds(start, band), lanes]
        v = v_ref[0, pl.ds(start, band), lanes]
        acc = jnp.zeros((tq, LANES), F32)
        for j in range(HEADS_PER_VREG):
            s = _dot_nt(jnp.where(qmask[j], q, zero), k) + bias_ref[0, hp * HEADS_PER_VREG + j]
            s = jnp.where(valid, s, NEG_INF)
            p = jnp.exp(s - jnp.max(s, axis=-1, keepdims=True))
            p = p / jnp.sum(p, axis=-1, keepdims=True)
            acc = acc + _dot(p.astype(BF16), jnp.where(vmask[j], v, zero))
        o_ref[0, :, lanes] = acc.astype(BF16)


def band_attn(q, k, v, bias, *, tq, band, n_valid_fn):
    b, sq, w = q.shape
    sk = k.shape[1]
    n_shift, n_heads = bias.shape[:2]
    kern = functools.partial(_band_attn_kernel, tq=tq, band=band, n_valid_fn=n_valid_fn)
    return pl.pallas_call(
        kern, grid=(b, sq // tq),
        in_specs=[pl.BlockSpec((1, tq, w), lambda i, c: (i, c, 0)),
                  pl.BlockSpec((1, sk, w), lambda i, c: (i, 0, 0)),
                  pl.BlockSpec((1, sk, w), lambda i, c: (i, 0, 0)),
                  pl.BlockSpec((1, n_heads, tq, band),
                               lambda i, c: (jnp.minimum(c, n_shift - 1), 0, 0, 0))],
        out_specs=pl.BlockSpec((1, tq, w), lambda i, c: (i, c, 0)),
        out_shape=jax.ShapeDtypeStruct((b, sq, w), BF16),
        compiler_params=_cparams("parallel", "arbitrary"), name="band_attn")(q, k, v, bias)


STICK_UNDERFLOW = 110.0
STICK_NEAR_CHUNKS = 2


def _stick_kernel(q_ref, k_ref, v_ref, o_ref, k16, v16, zl_scr, hl_scr, w_scr, *,
                  tq, q_offset, variants):
    qi = pl.program_id(2)
    ck = KEY_CHUNK
    n_heads = HEADS_PER_VREG

    @pl.when(qi == 0)
    def _():
        k16[...] = k_ref[0].astype(BF16)
        v = v_ref[0].astype(BF16)
        vmask = _head_masks(v.shape)
        for j in range(n_heads):
            v16[j] = jnp.where(vmask[j], v, jnp.zeros((), BF16))

    q = q_ref[0]
    qmask = _head_masks((tq, LANES))
    qs = [jnp.where(qmask[j], q, jnp.zeros((), BF16)) for j in range(n_heads)]
    q_pos = q_offset + qi * tq + lax.broadcasted_iota(jnp.int32, (tq, ck), 0)
    k_iota = lax.broadcasted_iota(jnp.int32, (tq, ck), 1)
    rr = lax.broadcasted_iota(jnp.int32, (ck, ck), 0)
    cc = lax.broadcasted_iota(jnp.int32, (ck, ck), 1)
    tri = jnp.where(rr > cc, 1.0, 0.0).astype(BF16)
    n_chunks = (q_offset + (qi + 1) * tq + ck - 1) // ck

    def logits(j, slot, c, masked):
        k0 = pl.multiple_of(c * ck, ck)
        z = _dot_nt(qs[j], k16[pl.ds(k0, ck), :])
        sp = jnp.maximum(z, 0.0) + jnp.log(1.0 + jnp.exp(-jnp.abs(z)))
        if masked:
            sp = jnp.where(k0 + k_iota < q_pos, sp, 0.0)
        zl_scr[j, :, slot] = z - sp
        hi, lo = _split2(sp)
        hl_scr[0, j, :, slot] = hi
        hl_scr[1, j, :, slot] = lo
        return jnp.sum(sp, axis=-1, keepdims=True)

    def weights(j, slot, c, later, masked):
        sums = _dot(hl_scr[0, j, :, slot], tri) + _dot(hl_scr[1, j, :, slot], tri)
        w = jnp.exp(zl_scr[j, :, slot] - sums - later)
        if masked:
            w = jnp.where(c * ck + k_iota < q_pos, w, 0.0)
        w_scr[j, :, slot] = w.astype(BF16)

    def more(laters):
        return (jnp.min(functools.reduce(jnp.minimum, laters)) < STICK_UNDERFLOW).astype(jnp.int32)

    def attend(near, n_masked):
        slots = [slice((near - 1 - i) * ck, (near - i) * ck) for i in range(near)]
        row_sums = [[logits(j, slots[i], n_chunks - 1 - i, i < n_masked) for i in range(near)]
                    for j in range(n_heads)]
        laters = []
        for j in range(n_heads):
            later = jnp.zeros((tq, 1), F32)
            for i in range(near):
                weights(j, slots[i], n_chunks - 1 - i, later, i < n_masked)
                later = later + row_sums[j][i]
            laters.append(later)
        k0 = pl.multiple_of((n_chunks - near) * ck, ck)
        acc = sum(_dot(w_scr[j, :, :near * ck], v16[j, pl.ds(k0, near * ck), :])
                  for j in range(n_heads))

        def cond(carry):
            return (carry[0] >= 0) & (carry[1] > 0)

        def body(carry):
            c, _, acc, laters = carry
            k0 = pl.multiple_of(c * ck, ck)
            new = []
            for j in range(n_heads):
                rs = logits(j, slice(0, ck), c, False)
                weights(j, slice(0, ck), c, laters[j], False)
                acc = acc + _dot(w_scr[j, :, :ck], v16[j, pl.ds(k0, ck), :])
                new.append(laters[j] + rs)
            return c - 1, more(new), acc, tuple(new)

        carry = lax.while_loop(cond, body, (n_chunks - 1 - near, more(laters), acc, tuple(laters)))
        o_ref[0] = carry[2].astype(BF16)

    for n_total, near, n_masked in variants:
        if n_total is None:
            pl.when(n_chunks >= near)(functools.partial(attend, near, n_masked))
        else:
            pl.when(n_chunks == n_total)(functools.partial(attend, near, n_masked))


def stick_attn(q, k, v, *, tq, q_offset):
    b, sq, w = q.shape
    sk = k.shape[1]
    ck = KEY_CHUNK
    assert q_offset % ck == 0 and (tq % ck == 0 or sq == tq <= ck) and sk % ck == 0
    n_masked = -(-tq // ck)
    totals = sorted({-(-(q_offset + (t + 1) * tq) // ck) for t in range(sq // tq)})
    variants = [(n, n, min(n_masked, n)) for n in totals if n < STICK_NEAR_CHUNKS]
    if totals[-1] >= STICK_NEAR_CHUNKS:
        variants.append((None, STICK_NEAR_CHUNKS, n_masked))
    near_cols = STICK_NEAR_CHUNKS * ck
    kern = functools.partial(_stick_kernel, tq=tq, q_offset=q_offset, variants=tuple(variants))
    return pl.pallas_call(
        kern, grid=(b, w // LANES, sq // tq),
        in_specs=[pl.BlockSpec((1, tq, LANES), lambda i, h, t: (i, t, h)),
                  pl.BlockSpec((1, sk, LANES), lambda i, h, t: (i, 0, h)),
                  pl.BlockSpec((1, sk, LANES), lambda i, h, t: (i, 0, h))],
        out_specs=pl.BlockSpec((1, tq, LANES), lambda i, h, t: (i, t, h)),
        out_shape=jax.ShapeDtypeStruct((b, sq, w), BF16),
        scratch_shapes=[pltpu.VMEM((sk, LANES), BF16),
                        pltpu.VMEM((HEADS_PER_VREG, sk, LANES), BF16),
                        pltpu.VMEM((HEADS_PER_VREG, tq, near_cols), F32),
                        pltpu.VMEM((2, HEADS_PER_VREG, tq, near_cols), BF16),
                        pltpu.VMEM((HEADS_PER_VREG, tq, near_cols), BF16)],
        compiler_params=_cparams("parallel", "parallel", "arbitrary"), name="stick_attn")(q, k, v)


def _cumsum_kernel(lf_ref, lft_ref, col_ref, row_ref, *, blk):
    s_len, n_heads = lf_ref.shape[1], lf_ref.shape[2]
    rr = lax.broadcasted_iota(jnp.int32, (blk, blk), 0)
    cc = lax.broadcasted_iota(jnp.int32, (blk, blk), 1)
    lower = jnp.where(rr >= cc, 1.0, 0.0).astype(BF16)
    upper = jnp.where(rr <= cc, 1.0, 0.0).astype(BF16)
    carry_col = jnp.zeros((1, n_heads), F32)
    carry_row = jnp.zeros((n_heads, 1), F32)
    for n in range(s_len // blk):
        rows = slice(n * blk, (n + 1) * blk)
        c = sum(_dot(lower, p) for p in _split3(lf_ref[0, rows, :])) + carry_col
        col_ref[0, rows, :] = c
        carry_col = c[blk - 1:blk, :]
        ct = sum(_dot(p, upper) for p in _split3(lft_ref[0, :, rows])) + carry_row
        row_ref[0, :, rows] = ct
        carry_row = ct[:, blk - 1:blk]


def cumsum_logf(lf):
    b, s_len, n_heads = lf.shape
    kern = functools.partial(_cumsum_kernel, blk=LANES)
    return pl.pallas_call(
        kern, grid=(b,),
        in_specs=[pl.BlockSpec((1, s_len, n_heads), lambda i: (i, 0, 0)),
                  pl.BlockSpec((1, n_heads, s_len), lambda i: (i, 0, 0))],
        out_specs=[pl.BlockSpec((1, s_len, n_heads), lambda i: (i, 0, 0)),
                   pl.BlockSpec((1, n_heads, s_len), lambda i: (i, 0, 0))],
        out_shape=[jax.ShapeDtypeStruct((b, s_len, n_heads), F32),
                   jax.ShapeDtypeStruct((b, n_heads, s_len), F32)],
        compiler_params=_cparams("parallel"), name="cumsum_logf")(lf, jnp.swapaxes(lf, 1, 2))


def _key_extent(qi, *, tq, q_offset, bucket, sk):
    return jnp.minimum((q_offset + (qi + 1) * tq + bucket - 1) // bucket * bucket, sk)


def _forget_kernel(q_ref, k_ref, v_ref, cq_ref, ck_ref, o_ref, k16, v16, s_scr, p_scr, *,
                   tq, q_offset, bucket, variants):
    hp = pl.program_id(1)
    qi = pl.program_id(2)
    sk = k_ref.shape[1]
    ck_w = KEY_CHUNK

    @pl.when(qi == 0)
    def _():
        k16[...] = k_ref[0].astype(BF16)
        v = v_ref[0].astype(BF16)
        vmask = _head_masks(v.shape)
        for j in range(HEADS_PER_VREG):
            v16[j] = jnp.where(vmask[j], v, jnp.zeros((), BF16))

    q = q_ref[0]
    qmask = _head_masks((tq, LANES))
    q_pos = q_offset + qi * tq + lax.broadcasted_iota(jnp.int32, (tq, ck_w), 0)
    k_iota = lax.broadcasted_iota(jnp.int32, (tq, ck_w), 1)
    cq_all = cq_ref[0]
    head_lane = lax.broadcasted_iota(jnp.int32, cq_all.shape, 1)
    n_k_here = _key_extent(qi, tq=tq, q_offset=q_offset, bucket=bucket, sk=sk)

    def attend(n_k, mask_from):
        out = None
        for j in range(HEADS_PER_VREG):
            qj = jnp.where(qmask[j], q, jnp.zeros((), BF16))
            cq = jnp.sum(jnp.where(head_lane == hp * HEADS_PER_VREG + j, cq_all, 0.0),
                         axis=-1, keepdims=True)
            mrun = jnp.full((tq, LANES), NEG_INF, F32)
            for c in range(n_k // ck_w):
                cols = slice(c * ck_w, (c + 1) * ck_w)
                s = _dot_nt(qj, k16[cols, :]) - ck_ref[0, 0, j:j + 1, cols]
                if c >= mask_from:
                    s = jnp.where(c * ck_w + k_iota <= q_pos, s, NEG_INF)
                s_scr[j, :, cols] = s
                for part in range(ck_w // LANES):
                    mrun = jnp.maximum(mrun, s[:, part * LANES:(part + 1) * LANES])
            m = jnp.max(mrun, axis=-1, keepdims=True) + cq
            row = cq - m
            lrun = jnp.zeros((tq, LANES), F32)
            for c in range(n_k // ck_w):
                cols = slice(c * ck_w, (c + 1) * ck_w)
                p = jnp.exp(s_scr[j, :, cols] + row)
                for part in range(ck_w // LANES):
                    lrun = lrun + p[:, part * LANES:(part + 1) * LANES]
                p_scr[j, :, cols] = p.astype(BF16)
            l = jnp.sum(lrun, axis=-1, keepdims=True)
            o = _dot(p_scr[j, :, :n_k], v16[j, :n_k, :]) / l
            out = o if out is None else out + o
        o_ref[0] = out.astype(BF16)

    for n_k, mask_from in variants:
        pl.when(n_k_here == n_k)(functools.partial(attend, n_k, mask_from))


def forget_attn(q, k, v, cum_q, cum_k_rows, *, tq, bucket, q_offset):
    b, sq, w = q.shape
    sk = k.shape[1]
    n_heads = cum_q.shape[2]
    assert sk % KEY_CHUNK == 0 and bucket % KEY_CHUNK == 0
    extent = lambda t: min(-(-(q_offset + (t + 1) * tq) // bucket) * bucket, sk)
    variants = []
    for n_k in sorted({extent(t) for t in range(sq // tq)}):
        first_row = min(q_offset + t * tq for t in range(sq // tq) if extent(t) == n_k)
        variants.append((n_k, first_row // KEY_CHUNK))
    ck = cum_k_rows.reshape(b, n_heads // HEADS_PER_VREG, HEADS_PER_VREG, sk)
    kern = functools.partial(_forget_kernel, tq=tq, q_offset=q_offset, bucket=bucket,
                             variants=tuple(variants))
    n_k_max = variants[-1][0]
    return pl.pallas_call(
        kern, grid=(b, w // LANES, sq // tq),
        in_specs=[pl.BlockSpec((1, tq, LANES), lambda i, h, t: (i, t, h)),
                  pl.BlockSpec((1, sk, LANES), lambda i, h, t: (i, 0, h)),
                  pl.BlockSpec((1, sk, LANES), lambda i, h, t: (i, 0, h)),
                  pl.BlockSpec((1, tq, n_heads), lambda i, h, t: (i, t, 0)),
                  pl.BlockSpec((1, 1, HEADS_PER_VREG, sk), lambda i, h, t: (i, h, 0, 0))],
        out_specs=pl.BlockSpec((1, tq, LANES), lambda i, h, t: (i, t, h)),
        out_shape=jax.ShapeDtypeStruct((b, sq, w), BF16),
        scratch_shapes=[pltpu.VMEM((sk, LANES), BF16),
                        pltpu.VMEM((HEADS_PER_VREG, sk, LANES), BF16),
                        pltpu.VMEM((HEADS_PER_VREG, tq, n_k_max), F32),
                        pltpu.VMEM((HEADS_PER_VREG, tq, n_k_max), BF16)],
        compiler_params=_cparams("parallel", "parallel", "arbitrary"),
        name="forget_attn")(q, k, v, cum_q, ck)


def _xattn_kernel(x_ref, g_ref, wq_ref, wo_ref, mk_ref, mv_ref, o_ref, *, n_heads):
    x = x_ref[...]
    d_model = x.shape[1]
    hd = d_model // n_heads
    h = _rms_bf16(x, g_ref[...])
    q = (_dot(h, wq_ref[...]) * (hd ** -0.5)).astype(BF16)
    outs = []
    for j in range(n_heads):
        cols = slice(j * hd, (j + 1) * hd)
        s = _dot_nt(q[:, cols], mk_ref[0, :, cols])
        p = jnp.exp(s - jnp.max(s, axis=-1, keepdims=True))
        p = p / jnp.sum(p, axis=-1, keepdims=True)
        outs.append(_dot(p.astype(BF16), mv_ref[0, :, cols]).astype(BF16))
    o = jnp.concatenate(outs, axis=-1)
    o_ref[...] = x + _dot(o, wo_ref[...])


def xattn(x, g, wq, wo, mk, mv, *, seq, tm):
    m, d = x.shape
    n_mem = mk.shape[1]
    tiles_per_seq = seq // tm
    kern = functools.partial(_xattn_kernel, n_heads=XA_HEADS)
    return pl.pallas_call(
        kern, grid=(m // tm,),
        in_specs=[pl.BlockSpec((tm, d), lambda i: (i, 0)),
                  pl.BlockSpec((1, d), lambda i: (0, 0)),
                  pl.BlockSpec((d, d), lambda i: (0, 0)),
                  pl.BlockSpec((d, d), lambda i: (0, 0)),
                  pl.BlockSpec((1, n_mem, d), lambda i: (i // tiles_per_seq, 0, 0)),
                  pl.BlockSpec((1, n_mem, d), lambda i: (i // tiles_per_seq, 0, 0))],
        out_specs=pl.BlockSpec((tm, d), lambda i: (i, 0)),
        out_shape=jax.ShapeDtypeStruct((m, d), F32),
        compiler_params=_cparams("parallel"), name="xattn")(x, g.reshape(1, d), wq, wo, mk, mv)


def _ffn_kernel(x_ref, g_ref, wg_ref, wu_ref, wd_ref, gf_ref, o_ref, *, tf, final_norm):
    x = x_ref[...]
    h = _rms_bf16(x, g_ref[...])
    acc = x
    for c in range(wg_ref.shape[1] // tf):
        cols = slice(c * tf, (c + 1) * tf)
        gate = _dot(h, wg_ref[:, cols])
        up = _dot(h, wu_ref[:, cols])
        a = (gate * jax.nn.sigmoid(gate) * up).astype(BF16)
        acc = acc + _dot(a, wd_ref[cols, :])
    if final_norm:
        acc = acc * lax.rsqrt(jnp.mean(acc * acc, axis=-1, keepdims=True) + RMS_EPS) * gf_ref[...]
    o_ref[...] = acc


def ffn(x, g, wg, wu, wd, g_final, *, tm, tf, final_norm):
    m, d = x.shape
    dff = wg.shape[1]
    kern = functools.partial(_ffn_kernel, tf=tf, final_norm=final_norm)
    return pl.pallas_call(
        kern, grid=(m // tm,),
        in_specs=[pl.BlockSpec((tm, d), lambda i: (i, 0)),
                  pl.BlockSpec((1, d), lambda i: (0, 0)),
                  pl.BlockSpec((d, dff), lambda i: (0, 0)),
                  pl.BlockSpec((d, dff), lambda i: (0, 0)),
                  pl.BlockSpec((dff, d), lambda i: (0, 0)),
                  pl.BlockSpec((1, d), lambda i: (0, 0))],
        out_specs=pl.BlockSpec((tm, d), lambda i: (i, 0)),
        out_shape=jax.ShapeDtypeStruct((m, d), F32),
        compiler_params=_cparams("parallel"),
        name="ffn")(x, g.reshape(1, d), wg, wu, wd, g_final.reshape(1, d))


def _pad_rows(a, rows):
    return jnp.pad(a, ((0, 0), (0, rows - a.shape[1]), (0, 0)))


def _row_tile(m, seq, cap):
    tm = min(cap, seq)
    assert seq % tm == 0 and m % tm == 0
    return tm


def _layer_ab(x, seq, g, w_in, w_out, band_tbl, cache, *, tm):
    m, d = x.shape
    b = m // seq
    wa = w_in.shape[1] // 6
    ws = [w_in[:, n * wa:(n + 1) * wa].astype(BF16) for n in range(6)]
    scale = HEAD_DIM ** -0.5
    keep = min(A_PAST, seq)
    outs = [(0, 'bf16', scale), (1, 'bf16', 1.0), (2, 'bf16', 1.0),
            (1, 'tail' if keep < seq else 'f32', keep), (2, 'tail' if keep < seq else 'f32', keep),
            (3, 'bf16', scale), (4, 'f32', None), (5, 'f32', None)]
    qa, ka16, va16, ka_keep, va_keep, qb, kb, vb = norm_proj(x, g, ws, outs, seq=seq, tm=tm)
    shp = lambda a: a.reshape(b, -1, a.shape[-1])
    if cache is None:
        n_shift = A_PAST_CHUNKS + 1
        band = A_PAST + CHUNK
        bias = jnp.stack([band_tbl[:, :CHUNK, (A_PAST_CHUNKS - s) * CHUNK:][:, :, :band]
                          for s in range(n_shift)])
        oa = band_attn(shp(qa), shp(ka16), shp(va16), bias, tq=CHUNK, band=band,
                       n_valid_fn=lambda c: (jnp.minimum(c, A_PAST_CHUNKS) + 1) * CHUNK)
        ob = stick_attn(shp(qb), shp(kb), shp(vb), tq=256, q_offset=0)
    else:
        ca_k, ca_v, cb_k, cb_v = cache
        n_past = ca_k.shape[1]
        n_keys = n_past + seq
        band = -(-n_keys // LANES) * LANES
        flat = lambda a: a.reshape(a.shape[0], a.shape[1], -1)
        k_all = _pad_rows(jnp.concatenate([flat(ca_k).astype(BF16), shp(ka16)], axis=1), band)
        v_all = _pad_rows(jnp.concatenate([flat(ca_v).astype(BF16), shp(va16)], axis=1), band)
        bias = band_tbl[None, :, :seq, A_PAST - n_past:][:, :, :, :band]
        oa = band_attn(shp(qa), k_all, v_all, bias, tq=seq, band=band,
                       n_valid_fn=lambda c: n_keys)
        n_pastb = cb_k.shape[1]
        sk = -(-(n_pastb + seq) // KEY_CHUNK) * KEY_CHUNK
        kb_all = _pad_rows(jnp.concatenate([flat(cb_k), shp(kb)], axis=1), sk)
        vb_all = _pad_rows(jnp.concatenate([flat(cb_v), shp(vb)], axis=1), sk)
        ob = stick_attn(shp(qb), kb_all, vb_all, tq=seq, q_offset=n_pastb)
    o = jnp.concatenate([oa, ob], axis=-1).reshape(m, d)
    x = proj_res(o, w_out.astype(BF16), x, tm=tm)
    return x, (ka_keep, va_keep, kb, vb)


def _layer_c(x, seq, g, w_in, b_f, w_out, cache, *, tm):
    m, d = x.shape
    b = m // seq
    n_heads = b_f.shape[0]
    wq, wk, wv, wf = (w_in[:, :d], w_in[:, d:2 * d], w_in[:, 2 * d:3 * d], w_in[:, 3 * d:])
    ws = [w.astype(BF16) for w in (wq, wk, wv, wf)]
    outs = [(0, 'bf16', HEAD_DIM ** -0.5), (1, 'f32', None), (2, 'f32', None), (3, 'logf', None)]
    q, k, v, lf = norm_proj(x, g, ws, outs, seq=seq, tm=tm, bias=b_f.reshape(1, n_heads))
    shp = lambda a: a.reshape(b, -1, a.shape[-1])
    if cache is None:
        cum_col, cum_row = cumsum_logf(shp(lf))
        o = forget_attn(shp(q), shp(k), shp(v), cum_col, cum_row, tq=256, bucket=512, q_offset=0)
    else:
        c_k, c_v, c_lf = cache
        n_past = c_k.shape[1]
        sk = -(-(n_past + seq) // KEY_CHUNK) * KEY_CHUNK
        flat = lambda a: a.reshape(a.shape[0], a.shape[1], -1)
        k_all = _pad_rows(jnp.concatenate([flat(c_k), shp(k)], axis=1), sk)
        v_all = _pad_rows(jnp.concatenate([flat(c_v), shp(v)], axis=1), sk)
        lf_all = _pad_rows(jnp.concatenate([c_lf, shp(lf)], axis=1), sk)
        cum_col, cum_row = cumsum_logf(lf_all)
        o = forget_attn(shp(q), k_all, v_all, cum_col[:, n_past:n_past + seq], cum_row,
                        tq=seq, bucket=KEY_CHUNK, q_offset=n_past)
    x = proj_res(o.reshape(m, d), w_out.astype(BF16), x, tm=tm)
    return x, (k, v, lf)


def kernel(x_prompt, x_sample, cache_a_k, cache_a_v, cache_b_k, cache_b_v, cache_c_k, cache_c_v, cache_c_logf, cache_mem_k, cache_mem_v, mem_prompt, w_in_ab, w_out_ab, rel_bias_a, w_in_c, b_f_c, w_out_c, g_mix, g_xattn, g_mem, w_xq, w_xk, w_xv, w_xo, g_ffn, w_gate, w_up, w_down, g_final):
    bp, sp, d = x_prompt.shape
    bs, ss, _ = x_sample.shape
    depth = g_mix.shape[0]
    n_mem = mem_prompt.shape[1]
    xp = x_prompt.reshape(bp * sp, d)
    xs = x_sample.reshape(bs * ss, d)
    tmp = _row_tile(bp * sp, sp, 512)
    tms = bs * ss
    assert tms <= 512
    mem = mem_prompt.reshape(bp * n_mem, d)
    tmm = _row_tile(bp * n_mem, n_mem, 512)
    dff = w_gate.shape[2]
    tf = 256 if dff % 256 == 0 else dff

    a_kp, a_vp, b_kp, b_vp, a_ks, a_vs, b_ks, b_vs = [], [], [], [], [], [], [], []
    c_kp, c_vp, c_lfp, c_ks, c_vs, c_lfs = [], [], [], [], [], []
    mem_kp, mem_vp = [], []
    for layer in range(depth):
        if layer % 2 == 0:
            e = layer // 2
            band_cols = -(-(2 * A_PAST + CHUNK) // LANES) * LANES
            band_tbl = bias_band(rel_bias_a[e], CHUNK, band_cols)
            xp, (ka, va, kb, vb) = _layer_ab(xp, sp, g_mix[layer], w_in_ab[e], w_out_ab[e],
                                             band_tbl, None, tm=tmp)
            a_kp.append(ka); a_vp.append(va); b_kp.append(kb); b_vp.append(vb)
            xs, (ka, va, kb, vb) = _layer_ab(xs, ss, g_mix[layer], w_in_ab[e], w_out_ab[e], band_tbl,
                                             (cache_a_k[e], cache_a_v[e], cache_b_k[e], cache_b_v[e]),
                                             tm=tms)
            a_ks.append(ka); a_vs.append(va); b_ks.append(kb); b_vs.append(vb)
        else:
            c = layer // 2
            xp, (k, v, lf) = _layer_c(xp, sp, g_mix[layer], w_in_c[c], b_f_c[c], w_out_c[c], None,
                                      tm=tmp)
            c_kp.append(k); c_vp.append(v); c_lfp.append(lf)
            xs, (k, v, lf) = _layer_c(xs, ss, g_mix[layer], w_in_c[c], b_f_c[c], w_out_c[c],
                                      (cache_c_k[c], cache_c_v[c], cache_c_logf[c]), tm=tms)
            c_ks.append(k); c_vs.append(v); c_lfs.append(lf)
        mk, mv, mk16, mv16 = norm_proj(
            mem, g_mem[layer], [w_xk[layer].astype(BF16), w_xv[layer].astype(BF16)],
            [(0, 'f32', None), (1, 'f32', None), (0, 'bf16', 1.0), (1, 'bf16', 1.0)],
            seq=n_mem, tm=tmm)
        mem_kp.append(mk); mem_vp.append(mv)
        wq16, wo16 = w_xq[layer].astype(BF16), w_xo[layer].astype(BF16)
        xp = xattn(xp, g_xattn[layer], wq16, wo16, mk16.reshape(bp, n_mem, d),
                   mv16.reshape(bp, n_mem, d), seq=sp, tm=tmp)
        xs = xattn(xs, g_xattn[layer], wq16, wo16,
                   cache_mem_k[layer].reshape(bs, n_mem, d).astype(BF16),
                   cache_mem_v[layer].reshape(bs, n_mem, d).astype(BF16), seq=ss, tm=ss)
        last = layer == depth - 1
        wg16, wu16, wd16 = (w_gate[layer].astype(BF16), w_up[layer].astype(BF16),
                            w_down[layer].astype(BF16))
        xp = ffn(xp, g_ffn[layer], wg16, wu16, wd16, g_final, tm=tmp, tf=tf, final_norm=last)
        xs = ffn(xs, g_ffn[layer], wg16, wu16, wd16, g_final, tm=tms, tf=tf, final_norm=last)

    hd = HEAD_DIM
    xa_hd = d // XA_HEADS
    r5 = lambda lst, b, s, dd: jnp.stack([a.reshape(b, s, -1, dd) for a in lst])
    r4 = lambda lst, b, s: jnp.stack([a.reshape(b, s, -1) for a in lst])
    keep = min(A_PAST, sp)
    return (xp.reshape(bp, sp, d), xs.reshape(bs, ss, d),
            r5(a_kp, bp, keep, hd), r5(a_vp, bp, keep, hd), r5(b_kp, bp, sp, hd), r5(b_vp, bp, sp, hd),
            r5(c_kp, bp, sp, hd), r5(c_vp, bp, sp, hd), r4(c_lfp, bp, sp),
            r5(mem_kp, bp, n_mem, xa_hd), r5(mem_vp, bp, n_mem, xa_hd),
            r5(a_ks, bs, ss, hd), r5(a_vs, bs, ss, hd), r5(b_ks, bs, ss, hd), r5(b_vs, bs, ss, hd),
            r5(c_ks, bs, ss, hd), r5(c_vs, bs, ss, hd), r4(c_lfs, bs, ss))
```

```python
import functools

import jax
import jax.numpy as jnp
from jax import lax
from jax.experimental import pallas as pl
from jax.experimental.pallas import tpu as pltpu

F32 = jnp.float32
BF16 = jnp.bfloat16

RMS_EPS = 1e-6
NEG_INF = -1e30
LOG2E = 1.4426950408889634
HEAD_DIM = 64
CHUNK = 64
A_PAST_CHUNKS = 8
A_PAST = A_PAST_CHUNKS * CHUNK
REL_CLIP = 128
XA_HEADS = 4

LANES = 128
HEADS_PER_VREG = LANES // HEAD_DIM
KEY_CHUNK = 256
VMEM_LIMIT = 56 * 1024 * 1024


def _cparams(*sem):
    return pltpu.CompilerParams(dimension_semantics=sem, vmem_limit_bytes=VMEM_LIMIT)


def _rms_bf16(x, g):
    y = x * lax.rsqrt(jnp.mean(x * x, axis=-1, keepdims=True) + RMS_EPS)
    return (y * g).astype(BF16)


def _log_sigmoid(z):
    return jnp.minimum(z, 0.0) - jnp.log1p(jnp.exp(-jnp.abs(z)))


def _split3(x):
    hi = x.astype(BF16)
    r = x - hi.astype(F32)
    mid = r.astype(BF16)
    lo = (r - mid.astype(F32)).astype(BF16)
    return hi, mid, lo


def _split2(x):
    hi = x.astype(BF16)
    lo = (x - hi.astype(F32)).astype(BF16)
    return hi, lo


def _dot(a, b):
    return jnp.dot(a, b, preferred_element_type=F32)


def _dot_nt(a, b):
    return lax.dot_general(a, b, (((1,), (1,)), ((), ())), preferred_element_type=F32)


def _norm_proj_kernel(x_ref, g_ref, b_ref, *refs, n_w, outs, tiles_per_seq):
    w_refs, o_refs = refs[:n_w], refs[n_w:]
    h = _rms_bf16(x_ref[...], g_ref[...])
    ys = {}
    for o_ref, (grp, kind, arg) in zip(o_refs, outs):
        if grp not in ys:
            ys[grp] = _dot(h, w_refs[grp][...])
        y = ys[grp]
        if kind == 'f32':
            o_ref[...] = y
        elif kind == 'bf16':
            o_ref[...] = (y * arg).astype(BF16)
        elif kind == 'logf':
            o_ref[...] = _log_sigmoid(y + b_ref[...])
        else:
            @pl.when(pl.program_id(0) % tiles_per_seq == tiles_per_seq - 1)
            def _(o_ref=o_ref, y=y):
                o_ref[...] = y


def norm_proj(x, g, ws, outs, *, seq, tm, bias=None):
    m, d = x.shape
    tiles_per_seq = max(seq // tm, 1)
    if bias is None:
        bias = jnp.zeros((1, 16), F32)
    in_specs = [pl.BlockSpec((tm, d), lambda i: (i, 0)),
                pl.BlockSpec((1, d), lambda i: (0, 0)),
                pl.BlockSpec(bias.shape, lambda i: (0, 0))]
    in_specs += [pl.BlockSpec(w.shape, lambda i: (0, 0)) for w in ws]
    out_shape, out_specs = [], []
    for grp, kind, arg in outs:
        n = ws[grp].shape[1]
        if kind == 'tail':
            assert arg == tm
            out_shape.append(jax.ShapeDtypeStruct((m // tiles_per_seq, n), F32))
            out_specs.append(pl.BlockSpec((tm, n), lambda i: (i // tiles_per_seq, 0)))
        else:
            out_shape.append(jax.ShapeDtypeStruct((m, n), BF16 if kind == 'bf16' else F32))
            out_specs.append(pl.BlockSpec((tm, n), lambda i: (i, 0)))
    kern = functools.partial(_norm_proj_kernel, n_w=len(ws), outs=tuple(outs),
                             tiles_per_seq=tiles_per_seq)
    return pl.pallas_call(
        kern, grid=(m // tm,), in_specs=in_specs, out_specs=out_specs, out_shape=out_shape,
        compiler_params=_cparams("arbitrary"), name="norm_proj")(x, g.reshape(1, d), bias, *ws)


def _proj_res_kernel(*refs, n_in):
    a_refs, w_refs, r_ref, o_ref = refs[:n_in], refs[n_in:2 * n_in], refs[-2], refs[-1]
    acc = r_ref[...]
    for a_ref, w_ref in zip(a_refs, w_refs):
        acc = acc + _dot(a_ref[...], w_ref[...])
    o_ref[...] = acc


def proj_res(a_list, w_list, res, *, tm):
    m, n = res.shape
    return pl.pallas_call(
        functools.partial(_proj_res_kernel, n_in=len(a_list)), grid=(m // tm,),
        in_specs=([pl.BlockSpec((tm, a.shape[1]), lambda i: (i, 0)) for a in a_list]
                  + [pl.BlockSpec(w.shape, lambda i: (0, 0)) for w in w_list]
                  + [pl.BlockSpec((tm, n), lambda i: (i, 0))]),
        out_specs=pl.BlockSpec((tm, n), lambda i: (i, 0)),
        out_shape=jax.ShapeDtypeStruct((m, n), F32),
        compiler_params=_cparams("parallel"), name="proj_res")(*a_list, *w_list, res)


def _band_bias_kernel(rb_ref, off_ref, o_ref, *, n_real):
    h = pl.program_id(1)
    rows, cols = o_ref.shape[2], o_ref.shape[3]
    off = off_ref[pl.program_id(0)]
    r0 = pl.program_id(2) * cols
    i = lax.broadcasted_iota(jnp.int32, (rows, cols), 0)
    r = lax.broadcasted_iota(jnp.int32, (rows, cols), 1) + r0
    d = jnp.clip(off + i - r, -REL_CLIP, REL_CLIP) + REL_CLIP
    lo = jnp.clip(off - (r0 + cols - 1), -REL_CLIP, REL_CLIP) + REL_CLIP
    hi = jnp.clip(off + rows - 1 - r0, -REL_CLIP, REL_CLIP) + REL_CLIP

    def body(u, tbl):
        return jnp.where(d == u, rb_ref[h, u], tbl)

    tbl = lax.fori_loop(lo, hi + 1, body, jnp.zeros((rows, cols), F32))
    first = (i // CHUNK - A_PAST_CHUNKS) * CHUNK + off
    last = jnp.minimum((i // CHUNK + 1) * CHUNK + off, n_real)
    o_ref[0, 0] = jnp.where((r >= first) & (r < last), LOG2E * tbl, NEG_INF)


def band_bias(rel_bias, offsets, *, rows, cols, n_real):
    n_rel, n_heads = rel_bias.shape
    return pl.pallas_call(
        functools.partial(_band_bias_kernel, n_real=n_real),
        grid=(len(offsets), n_heads, cols // LANES),
        in_specs=[pl.BlockSpec(memory_space=pltpu.SMEM), pl.BlockSpec(memory_space=pltpu.SMEM)],
        out_specs=pl.BlockSpec((1, 1, rows, LANES), lambda o, h, c: (o, h, 0, c)),
        out_shape=jax.ShapeDtypeStruct((len(offsets), n_heads, rows, cols), F32),
        compiler_params=_cparams("arbitrary", "arbitrary", "arbitrary"),
        name="band_bias")(rel_bias.T, jnp.asarray(offsets, jnp.int32))


def _head_masks(shape):
    lane = lax.broadcasted_iota(jnp.int32, shape, len(shape) - 1)
    return [(lane % LANES) // HEAD_DIM == j for j in range(HEADS_PER_VREG)]


def _ones_lanes():
    return [((j + 1) % HEADS_PER_VREG) * HEAD_DIM for j in range(HEADS_PER_VREG)]


def _masked_values(v, j):
    lane = lax.broadcasted_iota(jnp.int32, v.shape, 1) % LANES
    vj = jnp.where(lane // HEAD_DIM == j, v, jnp.zeros((), BF16))
    return jnp.where(lane == _ones_lanes()[j], jnp.ones((), BF16), vj)


def _band_attn_kernel(q_ref, k_ref, v_ref, bias_ref, o_ref, v16, s_scr, p_scr, *, tq, band):
    step = pl.program_id(1)
    ck = KEY_CHUNK

    @pl.when(step == 0)
    def _():
        v = v_ref[0]
        for j in range(HEADS_PER_VREG):
            v16[j] = _masked_values(v, j)

    start = pl.multiple_of(jnp.maximum(step * (tq // CHUNK) - A_PAST_CHUNKS, 0) * CHUNK, CHUNK)
    qmask = _head_masks((tq, LANES))
    ones_lane = _ones_lanes()
    cols = [slice(c * ck, (c + 1) * ck) for c in range(band // ck)]
    heads = [(hp, j) for hp in range(q_ref.shape[2] // LANES) for j in range(HEADS_PER_VREG)]
    lanes = [slice(hp * LANES, (hp + 1) * LANES) for hp, _ in heads]
    qs = [jnp.where(qmask[j], q_ref[0, :, lanes[n]], jnp.zeros((), BF16))
          for n, (_, j) in enumerate(heads)]

    def logits(n, c, mrun):
        k = k_ref[0, pl.ds(pl.multiple_of(start + c * ck, CHUNK), ck), lanes[n]]
        s = _dot_nt(qs[n], k) + bias_ref[0, n, :, cols[c]]
        s_scr[n % 2, :, cols[c]] = s
        for part in range(ck // LANES):
            mrun = jnp.maximum(mrun, s[:, part * LANES:(part + 1) * LANES])
        return mrun

    def probs(n, c, row_max):
        p_scr[n % 2, :, cols[c]] = jnp.exp2(s_scr[n % 2, :, cols[c]] - row_max).astype(BF16)

    def pv(n):
        return _dot(p_scr[n % 2], v16[heads[n][1], pl.ds(start, band), lanes[n]])

    neg = jnp.full((tq, LANES), NEG_INF, F32)
    outs = []
    mrun = functools.reduce(lambda m, c: logits(0, c, m), range(len(cols)), neg)
    for n in range(len(heads)):
        row_max = jnp.max(mrun, axis=-1, keepdims=True)
        mrun = neg
        for c in range(len(cols)):
            if n + 1 < len(heads):
                mrun = logits(n + 1, c, mrun)
            probs(n, c, row_max)
        o = pv(n)
        lane = ones_lane[heads[n][1]]
        outs.append(o / o[:, lane:lane + 1])
    for n in range(0, len(heads), HEADS_PER_VREG):
        o_ref[0, :, lanes[n]] = jnp.where(qmask[0], outs[n], outs[n + 1]).astype(BF16)


def band_attn(q, k, v, bias, *, tq, band):
    b, sq, w = q.shape
    sk = k.shape[1]
    n_off, n_heads = bias.shape[:2]
    assert band % KEY_CHUNK == 0 and tq % CHUNK == 0 or sq == tq
    kern = functools.partial(_band_attn_kernel, tq=tq, band=band)
    return pl.pallas_call(
        kern, grid=(b, sq // tq),
        in_specs=[pl.BlockSpec((1, tq, w), lambda i, c: (i, c, 0)),
                  pl.BlockSpec((1, sk, w), lambda i, c: (i, 0, 0)),
                  pl.BlockSpec((1, sk, w), lambda i, c: (i, 0, 0)),
                  pl.BlockSpec((1, n_heads, tq, band),
                               lambda i, c: (jnp.minimum(c, n_off - 1), 0, 0, 0))],
        out_specs=pl.BlockSpec((1, tq, w), lambda i, c: (i, c, 0)),
        out_shape=jax.ShapeDtypeStruct((b, sq, w), BF16),
        scratch_shapes=[pltpu.VMEM((HEADS_PER_VREG, sk, w), BF16),
                        pltpu.VMEM((2, tq, band), F32),
                        pltpu.VMEM((2, tq, band), BF16)],
        compiler_params=_cparams("parallel", "arbitrary"), name="band_attn")(q, k, v, bias)


STICK_UNDERFLOW = 110.0
STICK_NEAR_CHUNKS = 2


def _stick_kernel(q_ref, k_ref, v_ref, o_ref, k16, v16, zl_scr, hl_scr, w_scr, *,
                  tq, q_offset, variants):
    qi = pl.program_id(2)
    ck = KEY_CHUNK
    n_heads = HEADS_PER_VREG

    @pl.when(qi == 0)
    def _():
        k16[...] = k_ref[0].astype(BF16)
        v = v_ref[0].astype(BF16)
        vmask = _head_masks(v.shape)
        for j in range(n_heads):
            v16[j] = jnp.where(vmask[j], v, jnp.zeros((), BF16))

    q = q_ref[0]
    qmask = _head_masks((tq, LANES))
    qs = [jnp.where(qmask[j], q, jnp.zeros((), BF16)) for j in range(n_heads)]
    q_pos = q_offset + qi * tq + lax.broadcasted_iota(jnp.int32, (tq, ck), 0)
    k_iota = lax.broadcasted_iota(jnp.int32, (tq, ck), 1)
    rr = lax.broadcasted_iota(jnp.int32, (ck, ck), 0)
    cc = lax.broadcasted_iota(jnp.int32, (ck, ck), 1)
    tri = jnp.where(rr > cc, 1.0, 0.0).astype(BF16)
    n_chunks = (q_offset + (qi + 1) * tq + ck - 1) // ck

    def logits(j, slot, c, masked):
        k0 = pl.multiple_of(c * ck, ck)
        z = _dot_nt(qs[j], k16[pl.ds(k0, ck), :])
        sp = jnp.maximum(z, 0.0) + jnp.log(1.0 + jnp.exp(-jnp.abs(z)))
        if masked:
            sp = jnp.where(k0 + k_iota < q_pos, sp, 0.0)
        zl_scr[j, :, slot] = z - sp
        hi, lo = _split2(sp)
        hl_scr[0, j, :, slot] = hi
        hl_scr[1, j, :, slot] = lo
        return jnp.sum(sp, axis=-1, keepdims=True)

    def weights(j, slot, c, later, masked):
        sums = _dot(hl_scr[0, j, :, slot], tri) + _dot(hl_scr[1, j, :, slot], tri)
        w = jnp.exp(zl_scr[j, :, slot] - sums - later)
        if masked:
            w = jnp.where(c * ck + k_iota < q_pos, w, 0.0)
        w_scr[j, :, slot] = w.astype(BF16)

    def more(laters):
        return (jnp.min(functools.reduce(jnp.minimum, laters)) < STICK_UNDERFLOW).astype(jnp.int32)

    def attend(near, n_masked):
        slots = [slice((near - 1 - i) * ck, (near - i) * ck) for i in range(near)]
        row_sums = [[logits(j, slots[i], n_chunks - 1 - i, i < n_masked) for i in range(near)]
                    for j in range(n_heads)]
        laters = []
        for j in range(n_heads):
            later = jnp.zeros((tq, 1), F32)
            for i in range(near):
                weights(j, slots[i], n_chunks - 1 - i, later, i < n_masked)
                later = later + row_sums[j][i]
            laters.append(later)
        k0 = pl.multiple_of((n_chunks - near) * ck, ck)
        acc = sum(_dot(w_scr[j, :, :near * ck], v16[j, pl.ds(k0, near * ck), :])
                  for j in range(n_heads))

        def cond(carry):
            return (carry[0] >= 0) & (carry[1] > 0)

        def body(carry):
            c, _, acc, laters = carry
            k0 = pl.multiple_of(c * ck, ck)
            new = []
            for j in range(n_heads):
                rs = logits(j, slice(0, ck), c, False)
                weights(j, slice(0, ck), c, laters[j], False)
                acc = acc + _dot(w_scr[j, :, :ck], v16[j, pl.ds(k0, ck), :])
                new.append(laters[j] + rs)
            return c - 1, more(new), acc, tuple(new)

        carry = lax.while_loop(cond, body, (n_chunks - 1 - near, more(laters), acc, tuple(laters)))
        o_ref[0] = carry[2].astype(BF16)

    for n_total, near, n_masked in variants:
        if n_total is None:
            pl.when(n_chunks >= near)(functools.partial(attend, near, n_masked))
        else:
            pl.when(n_chunks == n_total)(functools.partial(attend, near, n_masked))


def stick_attn(q, k, v, *, tq, q_offset):
    b, sq, w = q.shape
    sk = k.shape[1]
    ck = KEY_CHUNK
    assert q_offset % ck == 0 and (tq % ck == 0 or sq == tq <= ck) and sk % ck == 0
    n_masked = -(-tq // ck)
    totals = sorted({-(-(q_offset + (t + 1) * tq) // ck) for t in range(sq // tq)})
    variants = [(n, n, min(n_masked, n)) for n in totals if n < STICK_NEAR_CHUNKS]
    if totals[-1] >= STICK_NEAR_CHUNKS:
        variants.append((None, STICK_NEAR_CHUNKS, n_masked))
    near_cols = STICK_NEAR_CHUNKS * ck
    kern = functools.partial(_stick_kernel, tq=tq, q_offset=q_offset, variants=tuple(variants))
    return pl.pallas_call(
        kern, grid=(b, w // LANES, sq // tq),
        in_specs=[pl.BlockSpec((1, tq, LANES), lambda i, h, t: (i, t, h)),
                  pl.BlockSpec((1, sk, LANES), lambda i, h, t: (i, 0, h)),
                  pl.BlockSpec((1, sk, LANES), lambda i, h, t: (i, 0, h))],
        out_specs=pl.BlockSpec((1, tq, LANES), lambda i, h, t: (i, t, h)),
        out_shape=jax.ShapeDtypeStruct((b, sq, w), BF16),
        scratch_shapes=[pltpu.VMEM((sk, LANES), BF16),
                        pltpu.VMEM((HEADS_PER_VREG, sk, LANES), BF16),
                        pltpu.VMEM((HEADS_PER_VREG, tq, near_cols), F32),
                        pltpu.VMEM((2, HEADS_PER_VREG, tq, near_cols), BF16),
                        pltpu.VMEM((HEADS_PER_VREG, tq, near_cols), BF16)],
        compiler_params=_cparams("parallel", "parallel", "arbitrary"), name="stick_attn")(q, k, v)


def _cumsum_kernel(lf_ref, lft_ref, col_ref, row_ref, *, blk):
    s_len, n_heads = lf_ref.shape[1], lf_ref.shape[2]
    rr = lax.broadcasted_iota(jnp.int32, (blk, blk), 0)
    cc = lax.broadcasted_iota(jnp.int32, (blk, blk), 1)
    lower = jnp.where(rr >= cc, 1.0, 0.0).astype(BF16)
    upper = jnp.where(rr <= cc, 1.0, 0.0).astype(BF16)
    carry_col = jnp.zeros((1, n_heads), F32)
    carry_row = jnp.zeros((n_heads, 1), F32)
    for n in range(s_len // blk):
        rows = slice(n * blk, (n + 1) * blk)
        c = sum(_dot(lower, p) for p in _split3(lf_ref[0, rows, :])) + carry_col
        col_ref[0, rows, :] = c
        carry_col = c[blk - 1:blk, :]
        ct = sum(_dot(p, upper) for p in _split3(lft_ref[0, :, rows])) + carry_row
        row_ref[0, :, rows] = ct
        carry_row = ct[:, blk - 1:blk]


def cumsum_logf(lf):
    b, s_len, n_heads = lf.shape
    kern = functools.partial(_cumsum_kernel, blk=LANES)
    return pl.pallas_call(
        kern, grid=(b,),
        in_specs=[pl.BlockSpec((1, s_len, n_heads), lambda i: (i, 0, 0)),
                  pl.BlockSpec((1, n_heads, s_len), lambda i: (i, 0, 0))],
        out_specs=[pl.BlockSpec((1, s_len, n_heads), lambda i: (i, 0, 0)),
                   pl.BlockSpec((1, n_heads, s_len), lambda i: (i, 0, 0))],
        out_shape=[jax.ShapeDtypeStruct((b, s_len, n_heads), F32),
                   jax.ShapeDtypeStruct((b, n_heads, s_len), F32)],
        compiler_params=_cparams("parallel"), name="cumsum_logf")(lf, jnp.swapaxes(lf, 1, 2))


def _key_extent(qi, *, tq, q_offset, bucket, sk):
    return jnp.minimum((q_offset + (qi + 1) * tq + bucket - 1) // bucket * bucket, sk)


def _forget_kernel(q_ref, k_ref, v_ref, cq_ref, ck_ref, o_ref, k16, v16, s_scr, p_scr, *,
                   tq, q_offset, bucket, variants):
    hp = pl.program_id(1)
    qi = pl.program_id(2)
    sk = k_ref.shape[1]
    ck_w = KEY_CHUNK

    ones_lane = [((j + 1) % HEADS_PER_VREG) * HEAD_DIM for j in range(HEADS_PER_VREG)]

    @pl.when(qi == 0)
    def _():
        k16[...] = k_ref[0].astype(BF16)
        v = v_ref[0].astype(BF16)
        vmask = _head_masks(v.shape)
        lane = lax.broadcasted_iota(jnp.int32, v.shape, 1)
        for j in range(HEADS_PER_VREG):
            vj = jnp.where(vmask[j], v, jnp.zeros((), BF16))
            v16[j] = jnp.where(lane == ones_lane[j], jnp.ones((), BF16), vj)

    q = q_ref[0]
    qmask = _head_masks((tq, LANES))
    q_pos = q_offset + qi * tq + lax.broadcasted_iota(jnp.int32, (tq, ck_w), 0)
    k_iota = lax.broadcasted_iota(jnp.int32, (tq, ck_w), 1)
    cq_all = cq_ref[0]
    head_lane = lax.broadcasted_iota(jnp.int32, cq_all.shape, 1)
    n_k_here = _key_extent(qi, tq=tq, q_offset=q_offset, bucket=bucket, sk=sk)

    def attend(n_k, mask_from):
        n_c = n_k // ck_w
        cols = [slice(c * ck_w, (c + 1) * ck_w) for c in range(n_c)]
        allowed = {c: c * ck_w + k_iota <= q_pos for c in range(mask_from, n_c)}
        qj = [jnp.where(qmask[j], q, jnp.zeros((), BF16)) for j in range(HEADS_PER_VREG)]
        cq = [LOG2E * jnp.sum(jnp.where(head_lane == hp * HEADS_PER_VREG + j, cq_all, 0.0),
                              axis=-1, keepdims=True) for j in range(HEADS_PER_VREG)]

        def logits(j, c, mrun):
            s = _dot_nt(qj[j], k16[cols[c], :]) - LOG2E * ck_ref[0, 0, j:j + 1, cols[c]]
            if c in allowed:
                s = jnp.where(allowed[c], s, NEG_INF)
            s_scr[j, :, cols[c]] = s
            for part in range(ck_w // LANES):
                mrun = jnp.maximum(mrun, s[:, part * LANES:(part + 1) * LANES])
            return mrun

        def row_term(j, mrun):
            m = jnp.max(mrun, axis=-1, keepdims=True) + cq[j]
            return cq[j] - m

        def probs(j, c, row):
            p_scr[j, :, cols[c]] = jnp.exp2(s_scr[j, :, cols[c]] + row).astype(BF16)

        def pv(j):
            return _dot(p_scr[j, :, :n_k], v16[j, :n_k, :])

        start = jnp.full((tq, LANES), NEG_INF, F32)
        row0 = row_term(0, functools.reduce(lambda m, c: logits(0, c, m), range(n_c), start))
        mrun = start
        for c in range(n_c):
            mrun = logits(1, c, mrun)
            probs(0, c, row0)
        row1 = row_term(1, mrun)
        o0 = pv(0)
        for c in range(n_c):
            probs(1, c, row1)
        o1 = pv(1)
        l0 = o0[:, ones_lane[0]:ones_lane[0] + 1]
        l1 = o1[:, ones_lane[1]:ones_lane[1] + 1]
        o_ref[0] = jnp.where(qmask[0], o0 / l0, o1 / l1).astype(BF16)

    for n_k, mask_from in variants:
        pl.when(n_k_here == n_k)(functools.partial(attend, n_k, mask_from))


def forget_attn(q, k, v, cum_q, cum_k_rows, *, tq, bucket, q_offset):
    b, sq, w = q.shape
    sk = k.shape[1]
    n_heads = cum_q.shape[2]
    assert sk % KEY_CHUNK == 0 and bucket % KEY_CHUNK == 0
    extent = lambda t: min(-(-(q_offset + (t + 1) * tq) // bucket) * bucket, sk)
    variants = []
    for n_k in sorted({extent(t) for t in range(sq // tq)}):
        first_row = min(q_offset + t * tq for t in range(sq // tq) if extent(t) == n_k)
        variants.append((n_k, first_row // KEY_CHUNK))
    ck = cum_k_rows.reshape(b, n_heads // HEADS_PER_VREG, HEADS_PER_VREG, sk)
    kern = functools.partial(_forget_kernel, tq=tq, q_offset=q_offset, bucket=bucket,
                             variants=tuple(variants))
    n_k_max = variants[-1][0]
    return pl.pallas_call(
        kern, grid=(b, w // LANES, sq // tq),
        in_specs=[pl.BlockSpec((1, tq, LANES), lambda i, h, t: (i, t, h)),
                  pl.BlockSpec((1, sk, LANES), lambda i, h, t: (i, 0, h)),
                  pl.BlockSpec((1, sk, LANES), lambda i, h, t: (i, 0, h)),
                  pl.BlockSpec((1, tq, n_heads), lambda i, h, t: (i, t, 0)),
                  pl.BlockSpec((1, 1, HEADS_PER_VREG, sk), lambda i, h, t: (i, h, 0, 0))],
        out_specs=pl.BlockSpec((1, tq, LANES), lambda i, h, t: (i, t, h)),
        out_shape=jax.ShapeDtypeStruct((b, sq, w), BF16),
        scratch_shapes=[pltpu.VMEM((sk, LANES), BF16),
                        pltpu.VMEM((HEADS_PER_VREG, sk, LANES), BF16),
                        pltpu.VMEM((HEADS_PER_VREG, tq, n_k_max), F32),
                        pltpu.VMEM((HEADS_PER_VREG, tq, n_k_max), BF16)],
        compiler_params=_cparams("parallel", "parallel", "arbitrary"),
        name="forget_attn")(q, k, v, cum_q, ck)


def _xattn_kernel(x_ref, g_ref, wq_ref, wo_ref, mk_ref, mv_ref, o_ref, *, n_heads):
    x = x_ref[...]
    d_model = x.shape[1]
    hd = d_model // n_heads
    h = _rms_bf16(x, g_ref[...])
    q = (_dot(h, wq_ref[...]) * (hd ** -0.5)).astype(BF16)
    outs = []
    for j in range(n_heads):
        cols = slice(j * hd, (j + 1) * hd)
        s = _dot_nt(q[:, cols], mk_ref[0, :, cols])
        p = jnp.exp(s - jnp.max(s, axis=-1, keepdims=True))
        p = p / jnp.sum(p, axis=-1, keepdims=True)
        outs.append(_dot(p.astype(BF16), mv_ref[0, :, cols]).astype(BF16))
    o = jnp.concatenate(outs, axis=-1)
    o_ref[...] = x + _dot(o, wo_ref[...])


def xattn(x, g, wq, wo, mk, mv, *, seq, tm):
    m, d = x.shape
    n_mem = mk.shape[1]
    tiles_per_seq = seq // tm
    kern = functools.partial(_xattn_kernel, n_heads=XA_HEADS)
    return pl.pallas_call(
        kern, grid=(m // tm,),
        in_specs=[pl.BlockSpec((tm, d), lambda i: (i, 0)),
                  pl.BlockSpec((1, d), lambda i: (0, 0)),
                  pl.BlockSpec((d, d), lambda i: (0, 0)),
                  pl.BlockSpec((d, d), lambda i: (0, 0)),
                  pl.BlockSpec((1, n_mem, d), lambda i: (i // tiles_per_seq, 0, 0)),
                  pl.BlockSpec((1, n_mem, d), lambda i: (i // tiles_per_seq, 0, 0))],
        out_specs=pl.BlockSpec((tm, d), lambda i: (i, 0)),
        out_shape=jax.ShapeDtypeStruct((m, d), F32),
        compiler_params=_cparams("parallel"), name="xattn")(x, g.reshape(1, d), wq, wo, mk, mv)


def _ffn_kernel(x_ref, g_ref, wg_ref, wu_ref, wd_ref, gf_ref, o_ref, *, tf, final_norm):
    x = x_ref[...]
    h = _rms_bf16(x, g_ref[...])
    acc = x
    for c in range(wg_ref.shape[1] // tf):
        cols = slice(c * tf, (c + 1) * tf)
        gate = _dot(h, wg_ref[:, cols])
        up = _dot(h, wu_ref[:, cols])
        a = (gate * jax.nn.sigmoid(gate) * up).astype(BF16)
        acc = acc + _dot(a, wd_ref[cols, :])
    if final_norm:
        acc = acc * lax.rsqrt(jnp.mean(acc * acc, axis=-1, keepdims=True) + RMS_EPS) * gf_ref[...]
    o_ref[...] = acc


def ffn(x, g, wg, wu, wd, g_final, *, tm, tf, final_norm):
    m, d = x.shape
    dff = wg.shape[1]
    kern = functools.partial(_ffn_kernel, tf=tf, final_norm=final_norm)
    return pl.pallas_call(
        kern, grid=(m // tm,),
        in_specs=[pl.BlockSpec((tm, d), lambda i: (i, 0)),
                  pl.BlockSpec((1, d), lambda i: (0, 0)),
                  pl.BlockSpec((d, dff), lambda i: (0, 0)),
                  pl.BlockSpec((d, dff), lambda i: (0, 0)),
                  pl.BlockSpec((dff, d), lambda i: (0, 0)),
                  pl.BlockSpec((1, d), lambda i: (0, 0))],
        out_specs=pl.BlockSpec((tm, d), lambda i: (i, 0)),
        out_shape=jax.ShapeDtypeStruct((m, d), F32),
        compiler_params=_cparams("parallel"),
        name="ffn")(x, g.reshape(1, d), wg, wu, wd, g_final.reshape(1, d))


def _pad_rows(a, rows):
    return jnp.pad(a, ((0, 0), (0, rows - a.shape[1]), (0, 0)))


def _row_tile(m, seq, cap):
    tm = min(cap, seq)
    assert seq % tm == 0 and m % tm == 0
    return tm


def _layer_ab(x, seq, g, w_in, w_out, rel_bias, cache, *, tm):
    m, d = x.shape
    b = m // seq
    wa = w_in.shape[1] // 6
    ws = [w_in[:, n * wa:(n + 1) * wa].astype(BF16) for n in range(6)]
    scale = HEAD_DIM ** -0.5
    keep = min(A_PAST, seq)
    outs = [(0, 'bf16', scale * LOG2E), (1, 'bf16', 1.0), (2, 'bf16', 1.0),
            (1, 'tail' if keep < seq else 'f32', keep), (2, 'tail' if keep < seq else 'f32', keep),
            (3, 'bf16', scale), (4, 'f32', None), (5, 'f32', None)]
    qa, ka16, va16, ka_keep, va_keep, qb, kb, vb = norm_proj(x, g, ws, outs, seq=seq, tm=tm)
    shp = lambda a: a.reshape(b, -1, a.shape[-1])
    if cache is None:
        tq = 4 * CHUNK
        band = -(-(A_PAST + tq) // KEY_CHUNK) * KEY_CHUNK
        offsets = sorted({t * tq - max(t * (tq // CHUNK) - A_PAST_CHUNKS, 0) * CHUNK
                          for t in range(seq // tq)})
        bias = band_bias(rel_bias, offsets, rows=tq, cols=band, n_real=band)
        oa = band_attn(shp(qa), shp(ka16), shp(va16), bias, tq=tq, band=band)
        ob = stick_attn(shp(qb), shp(kb), shp(vb), tq=256, q_offset=0)
    else:
        ca_k, ca_v, cb_k, cb_v = cache
        n_past = ca_k.shape[1]
        n_keys = n_past + seq
        band = -(-n_keys // KEY_CHUNK) * KEY_CHUNK
        flat = lambda a: a.reshape(a.shape[0], a.shape[1], -1)
        k_all = _pad_rows(jnp.concatenate([flat(ca_k).astype(BF16), shp(ka16)], axis=1), band)
        v_all = _pad_rows(jnp.concatenate([flat(ca_v).astype(BF16), shp(va16)], axis=1), band)
        bias = band_bias(rel_bias, [n_past], rows=seq, cols=band, n_real=n_keys)
        oa = band_attn(shp(qa), k_all, v_all, bias, tq=seq, band=band)
        n_pastb = cb_k.shape[1]
        sk = -(-(n_pastb + seq) // KEY_CHUNK) * KEY_CHUNK
        kb_all = _pad_rows(jnp.concatenate([flat(cb_k), shp(kb)], axis=1), sk)
        vb_all = _pad_rows(jnp.concatenate([flat(cb_v), shp(vb)], axis=1), sk)
        ob = stick_attn(shp(qb), kb_all, vb_all, tq=seq, q_offset=n_pastb)
    w_out = w_out.astype(BF16)
    wo_a, wo_b = w_out[:oa.shape[-1]], w_out[oa.shape[-1]:]
    x = proj_res([oa.reshape(m, -1), ob.reshape(m, -1)], [wo_a, wo_b], x, tm=tm)
    return x, (ka_keep, va_keep, kb, vb)


def _layer_c(x, seq, g, w_in, b_f, w_out, cache, *, tm):
    m, d = x.shape
    b = m // seq
    n_heads = b_f.shape[0]
    wq, wk, wv, wf = (w_in[:, :d], w_in[:, d:2 * d], w_in[:, 2 * d:3 * d], w_in[:, 3 * d:])
    ws = [w.astype(BF16) for w in (wq, wk, wv, wf)]
    outs = [(0, 'bf16', HEAD_DIM ** -0.5 * LOG2E), (1, 'f32', None), (2, 'f32', None),
            (3, 'logf', None)]
    q, k, v, lf = norm_proj(x, g, ws, outs, seq=seq, tm=tm, bias=b_f.reshape(1, n_heads))
    shp = lambda a: a.reshape(b, -1, a.shape[-1])
    if cache is None:
        cum_col, cum_row = cumsum_logf(shp(lf))
        o = forget_attn(shp(q), shp(k), shp(v), cum_col, cum_row, tq=256, bucket=KEY_CHUNK, q_offset=0)
    else:
        c_k, c_v, c_lf = cache
        n_past = c_k.shape[1]
        sk = -(-(n_past + seq) // KEY_CHUNK) * KEY_CHUNK
        flat = lambda a: a.reshape(a.shape[0], a.shape[1], -1)
        k_all = _pad_rows(jnp.concatenate([flat(c_k), shp(k)], axis=1), sk)
        v_all = _pad_rows(jnp.concatenate([flat(c_v), shp(v)], axis=1), sk)
        lf_all = _pad_rows(jnp.concatenate([c_lf, shp(lf)], axis=1), sk)
        cum_col, cum_row = cumsum_logf(lf_all)
        o = forget_attn(shp(q), k_all, v_all, cum_col[:, n_past:n_past + seq], cum_row,
                        tq=seq, bucket=KEY_CHUNK, q_offset=n_past)
    x = proj_res([o.reshape(m, d)], [w_out.astype(BF16)], x, tm=tm)
    return x, (k, v, lf)


def kernel(x_prompt, x_sample, cache_a_k, cache_a_v, cache_b_k, cache_b_v, cache_c_k, cache_c_v, cache_c_logf, cache_mem_k, cache_mem_v, mem_prompt, w_in_ab, w_out_ab, rel_bias_a, w_in_c, b_f_c, w_out_c, g_mix, g_xattn, g_mem, w_xq, w_xk, w_xv, w_xo, g_ffn, w_gate, w_up, w_down, g_final):
    bp, sp, d = x_prompt.shape
    bs, ss, _ = x_sample.shape
    depth = g_mix.shape[0]
    n_mem = mem_prompt.shape[1]
    xp = x_prompt.reshape(bp * sp, d)
    xs = x_sample.reshape(bs * ss, d)
    tmp = _row_tile(bp * sp, sp, 512)
    tms = bs * ss
    assert tms <= 512
    mem = mem_prompt.reshape(bp * n_mem, d)
    tmm = _row_tile(bp * n_mem, n_mem, 512)
    dff = w_gate.shape[2]
    tf = 256 if dff % 256 == 0 else dff

    a_kp, a_vp, b_kp, b_vp, a_ks, a_vs, b_ks, b_vs = [], [], [], [], [], [], [], []
    c_kp, c_vp, c_lfp, c_ks, c_vs, c_lfs = [], [], [], [], [], []
    mem_kp, mem_vp = [], []
    for layer in range(depth):
        if layer % 2 == 0:
            e = layer // 2
            xp, (ka, va, kb, vb) = _layer_ab(xp, sp, g_mix[layer], w_in_ab[e], w_out_ab[e],
                                             rel_bias_a[e], None, tm=tmp)
            a_kp.append(ka); a_vp.append(va); b_kp.append(kb); b_vp.append(vb)
            xs, (ka, va, kb, vb) = _layer_ab(xs, ss, g_mix[layer], w_in_ab[e], w_out_ab[e],
                                             rel_bias_a[e],
                                             (cache_a_k[e], cache_a_v[e], cache_b_k[e], cache_b_v[e]),
                                             tm=tms)
            a_ks.append(ka); a_vs.append(va); b_ks.append(kb); b_vs.append(vb)
        else:
            c = layer // 2
            xp, (k, v, lf) = _layer_c(xp, sp, g_mix[layer], w_in_c[c], b_f_c[c], w_out_c[c], None,
                                      tm=tmp)
            c_kp.append(k); c_vp.append(v); c_lfp.append(lf)
            xs, (k, v, lf) = _layer_c(xs, ss, g_mix[layer], w_in_c[c], b_f_c[c], w_out_c[c],
                                      (cache_c_k[c], cache_c_v[c], cache_c_logf[c]), tm=tms)
            c_ks.append(k); c_vs.append(v); c_lfs.append(lf)
        mk, mv, mk16, mv16 = norm_proj(
            mem, g_mem[layer], [w_xk[layer].astype(BF16), w_xv[layer].astype(BF16)],
            [(0, 'f32', None), (1, 'f32', None), (0, 'bf16', 1.0), (1, 'bf16', 1.0)],
            seq=n_mem, tm=tmm)
        mem_kp.append(mk); mem_vp.append(mv)
        wq16, wo16 = w_xq[layer].astype(BF16), w_xo[layer].astype(BF16)
        xp = xattn(xp, g_xattn[layer], wq16, wo16, mk16.reshape(bp, n_mem, d),
                   mv16.reshape(bp, n_mem, d), seq=sp, tm=tmp)
        xs = xattn(xs, g_xattn[layer], wq16, wo16,
                   cache_mem_k[layer].reshape(bs, n_mem, d).astype(BF16),
                   cache_mem_v[layer].reshape(bs, n_mem, d).astype(BF16), seq=ss, tm=ss)
        last = layer == depth - 1
        wg16, wu16, wd16 = (w_gate[layer].astype(BF16), w_up[layer].astype(BF16),
                            w_down[layer].astype(BF16))
        xp = ffn(xp, g_ffn[layer], wg16, wu16, wd16, g_final, tm=tmp, tf=tf, final_norm=last)
        xs = ffn(xs, g_ffn[layer], wg16, wu16, wd16, g_final, tm=tms, tf=tf, final_norm=last)

    hd = HEAD_DIM
    xa_hd = d // XA_HEADS
    r5 = lambda lst, b, s, dd: jnp.stack([a.reshape(b, s, -1, dd) for a in lst])
    r4 = lambda lst, b, s: jnp.stack([a.reshape(b, s, -1) for a in lst])
    keep = min(A_PAST, sp)
    return (xp.reshape(bp, sp, d), xs.reshape(bs, ss, d),
            r5(a_kp, bp, keep, hd), r5(a_vp, bp, keep, hd), r5(b_kp, bp, sp, hd), r5(b_vp, bp, sp, hd),
            r5(c_kp, bp, sp, hd), r5(c_vp, bp, sp, hd), r4(c_lfp, bp, sp),
            r5(mem_kp, bp, n_mem, xa_hd), r5(mem_vp, bp, n_mem, xa_hd),
            r5(a_ks, bs, ss, hd), r5(a_vs, bs, ss, hd), r5(b_ks, bs, ss, hd), r5(b_vs, bs, ss, hd),
            r5(c_ks, bs, ss, hd), r5(c_vs, bs, ss, hd), r4(c_lfs, bs, ss))
```

```python
import functools

import jax
import jax.numpy as jnp
from jax import lax
from jax.experimental import pallas as pl
from jax.experimental.pallas import tpu as pltpu

F32 = jnp.float32
BF16 = jnp.bfloat16

RMS_EPS = 1e-6
NEG_INF = -1e30
LOG2E = 1.4426950408889634
HEAD_DIM = 64
CHUNK = 64
A_PAST_CHUNKS = 8
A_PAST = A_PAST_CHUNKS * CHUNK
REL_CLIP = 128
XA_HEADS = 4

LANES = 128
HEADS_PER_VREG = LANES // HEAD_DIM
KEY_CHUNK = 256
VMEM_LIMIT = 56 * 1024 * 1024


def _cparams(*sem):
    return pltpu.CompilerParams(dimension_semantics=sem, vmem_limit_bytes=VMEM_LIMIT)


def _resident_spec(shape):
    return pl.BlockSpec(shape, lambda *_: (0,) * len(shape), pipeline_mode=pl.Buffered(1))


def _rms_bf16(x, g):
    y = x * lax.rsqrt(jnp.mean(x * x, axis=-1, keepdims=True) + RMS_EPS)
    return (y * g).astype(BF16)


def _log_sigmoid(z):
    return jnp.minimum(z, 0.0) - jnp.log1p(jnp.exp(-jnp.abs(z)))


def _split3(x):
    hi = x.astype(BF16)
    r = x - hi.astype(F32)
    mid = r.astype(BF16)
    lo = (r - mid.astype(F32)).astype(BF16)
    return hi, mid, lo


def _split2(x):
    hi = x.astype(BF16)
    lo = (x - hi.astype(F32)).astype(BF16)
    return hi, lo


def _dot(a, b):
    return jnp.dot(a, b, preferred_element_type=F32)


def _dot_nt(a, b):
    return lax.dot_general(a, b, (((1,), (1,)), ((), ())), preferred_element_type=F32)


def _norm_proj_kernel(x_ref, g_ref, b_ref, *refs, n_w, outs, tiles_per_seq):
    w_refs, o_refs = refs[:n_w], refs[n_w:]
    h = _rms_bf16(x_ref[...], g_ref[...])
    ys = {}
    for o_ref, (grp, kind, arg) in zip(o_refs, outs):
        if grp not in ys:
            ys[grp] = _dot(h, w_refs[grp][...])
        y = ys[grp]
        if kind == 'f32':
            o_ref[...] = y
        elif kind == 'bf16':
            o_ref[...] = (y * arg).astype(BF16)
        elif kind == 'logf':
            o_ref[...] = _log_sigmoid(y + b_ref[...])
        elif kind == 'heads':
            o_ref[...] = y.reshape(o_ref.shape)
        else:
            @pl.when(pl.program_id(0) % tiles_per_seq == tiles_per_seq - 1)
            def _(o_ref=o_ref, y=y):
                o_ref[...] = y.reshape(o_ref.shape)


def norm_proj(x, g, ws, outs, *, seq, tm, bias=None):
    m, d = x.shape
    tiles_per_seq = max(seq // tm, 1)
    if bias is None:
        bias = jnp.zeros((1, 16), F32)
    in_specs = [pl.BlockSpec((tm, d), lambda i: (i, 0)),
                pl.BlockSpec((1, d), lambda i: (0, 0)),
                pl.BlockSpec(bias.shape, lambda i: (0, 0))]
    in_specs += [_resident_spec(w.shape) for w in ws]
    out_shape, out_specs = [], []
    for grp, kind, arg in outs:
        n = ws[grp].shape[1]
        if kind == 'tail':
            assert tm == min(A_PAST, seq)
            out_shape.append(jax.ShapeDtypeStruct((m // tiles_per_seq, n // arg, arg), F32))
            out_specs.append(pl.BlockSpec((tm, n // arg, arg), lambda i: (i // tiles_per_seq, 0, 0)))
        elif kind == 'heads':
            out_shape.append(jax.ShapeDtypeStruct((m, n // arg, arg), F32))
            out_specs.append(pl.BlockSpec((tm, n // arg, arg), lambda i: (i, 0, 0)))
        else:
            out_shape.append(jax.ShapeDtypeStruct((m, n), BF16 if kind == 'bf16' else F32))
            out_specs.append(pl.BlockSpec((tm, n), lambda i: (i, 0)))
    kern = functools.partial(_norm_proj_kernel, n_w=len(ws), outs=tuple(outs),
                             tiles_per_seq=tiles_per_seq)
    return pl.pallas_call(
        kern, grid=(m // tm,), in_specs=in_specs, out_specs=out_specs, out_shape=out_shape,
        compiler_params=_cparams("arbitrary"), name="norm_proj")(x, g.reshape(1, d), bias, *ws)


def _proj_res_kernel(*refs, n_in):
    a_refs, w_refs, r_ref, o_ref = refs[:n_in], refs[n_in:2 * n_in], refs[-2], refs[-1]
    acc = r_ref[...]
    for a_ref, w_ref in zip(a_refs, w_refs):
        acc = acc + _dot(a_ref[...], w_ref[...])
    o_ref[...] = acc


def proj_res(a_list, w_list, res, *, tm):
    m, n = res.shape
    return pl.pallas_call(
        functools.partial(_proj_res_kernel, n_in=len(a_list)), grid=(m // tm,),
        in_specs=([pl.BlockSpec((tm, a.shape[1]), lambda i: (i, 0)) for a in a_list]
                  + [pl.BlockSpec(w.shape, lambda i: (0, 0)) for w in w_list]
                  + [pl.BlockSpec((tm, n), lambda i: (i, 0))]),
        out_specs=pl.BlockSpec((tm, n), lambda i: (i, 0)),
        out_shape=jax.ShapeDtypeStruct((m, n), F32),
        compiler_params=_cparams("parallel"), name="proj_res")(*a_list, *w_list, res)


BIAS_BLOCK_ROWS = 32


def _band_bias_kernel(rb_ref, off_ref, o_ref, *, n_real):
    n_heads, rows, cols = o_ref.shape[1:]
    off = off_ref[pl.program_id(0)]
    i0 = pl.program_id(1) * rows
    r0 = pl.program_id(2) * cols
    i = lax.broadcasted_iota(jnp.int32, (rows, cols), 0) + i0
    r = lax.broadcasted_iota(jnp.int32, (rows, cols), 1) + r0
    d = jnp.clip(off + i - r, -REL_CLIP, REL_CLIP) + REL_CLIP
    lo = jnp.clip(off + i0 - (r0 + cols - 1), -REL_CLIP, REL_CLIP) + REL_CLIP
    hi = jnp.clip(off + i0 + rows - 1 - r0, -REL_CLIP, REL_CLIP) + REL_CLIP

    def body(u, tbls):
        hit = d == u
        return tuple(jnp.where(hit, rb_ref[h, u], t) for h, t in enumerate(tbls))

    tbls = lax.fori_loop(lo, hi + 1, body,
                         tuple(jnp.zeros((rows, cols), F32) for _ in range(n_heads)))
    first = (i // CHUNK - A_PAST_CHUNKS) * CHUNK + off
    last = jnp.minimum((i // CHUNK + 1) * CHUNK + off, n_real)
    visible = (r >= first) & (r < last)
    for h in range(n_heads):
        o_ref[0, h] = jnp.where(visible, LOG2E * tbls[h], NEG_INF)


def band_bias(rel_bias, offsets, *, rows, cols, n_real):
    n_rel, n_heads = rel_bias.shape
    blk = min(rows, BIAS_BLOCK_ROWS)
    return pl.pallas_call(
        functools.partial(_band_bias_kernel, n_real=n_real),
        grid=(len(offsets), rows // blk, cols // LANES),
        in_specs=[pl.BlockSpec(memory_space=pltpu.SMEM), pl.BlockSpec(memory_space=pltpu.SMEM)],
        out_specs=pl.BlockSpec((1, n_heads, blk, LANES), lambda o, t, c: (o, 0, t, c)),
        out_shape=jax.ShapeDtypeStruct((len(offsets), n_heads, rows, cols), F32),
        compiler_params=_cparams("arbitrary", "arbitrary", "arbitrary"),
        name="band_bias")(rel_bias.T, jnp.asarray(offsets, jnp.int32))


def _head_masks(shape):
    lane = lax.broadcasted_iota(jnp.int32, shape, len(shape) - 1)
    return [(lane % LANES) // HEAD_DIM == j for j in range(HEADS_PER_VREG)]


def _ones_lanes():
    return [((j + 1) % HEADS_PER_VREG) * HEAD_DIM for j in range(HEADS_PER_VREG)]


def _masked_values(v, j):
    lane = lax.broadcasted_iota(jnp.int32, v.shape, 1) % LANES
    vj = jnp.where(lane // HEAD_DIM == j, v, jnp.zeros((), BF16))
    return jnp.where(lane == _ones_lanes()[j], jnp.ones((), BF16), vj)


def _band_attn_kernel(q_ref, k_ref, v_ref, bias_ref, o_ref, v16, s_scr, p_scr, *, tq, band):
    step = pl.program_id(1)
    ck = KEY_CHUNK

    @pl.when(step == 0)
    def _():
        v = v_ref[0]
        for j in range(HEADS_PER_VREG):
            v16[j] = _masked_values(v, j)

    start = pl.multiple_of(jnp.maximum(step * (tq // CHUNK) - A_PAST_CHUNKS, 0) * CHUNK, CHUNK)
    qmask = _head_masks((tq, LANES))
    ones_lane = _ones_lanes()
    cols = [slice(c * ck, (c + 1) * ck) for c in range(band // ck)]
    heads = [(hp, j) for hp in range(q_ref.shape[2] // LANES) for j in range(HEADS_PER_VREG)]
    lanes = [slice(hp * LANES, (hp + 1) * LANES) for hp, _ in heads]
    qs = [jnp.where(qmask[j], q_ref[0, :, lanes[n]], jnp.zeros((), BF16))
          for n, (_, j) in enumerate(heads)]

    def logits(n, c, mrun):
        k = k_ref[0, pl.ds(pl.multiple_of(start + c * ck, CHUNK), ck), lanes[n]]
        s = _dot_nt(qs[n], k) + bias_ref[0, n, :, cols[c]]
        s_scr[n % 2, :, cols[c]] = s
        for part in range(ck // LANES):
            mrun = jnp.maximum(mrun, s[:, part * LANES:(part + 1) * LANES])
        return mrun

    def probs(n, c, row_max):
        p_scr[n % 2, :, cols[c]] = jnp.exp2(s_scr[n % 2, :, cols[c]] - row_max).astype(BF16)

    def pv(n):
        return _dot(p_scr[n % 2], v16[heads[n][1], pl.ds(start, band), lanes[n]])

    neg = jnp.full((tq, LANES), NEG_INF, F32)
    outs = []
    mrun = functools.reduce(lambda m, c: logits(0, c, m), range(len(cols)), neg)
    for n in range(len(heads)):
        row_max = jnp.max(mrun, axis=-1, keepdims=True)
        mrun = neg
        for c in range(len(cols)):
            if n + 1 < len(heads):
                mrun = logits(n + 1, c, mrun)
            probs(n, c, row_max)
        o = pv(n)
        lane = ones_lane[heads[n][1]]
        outs.append(o / o[:, lane:lane + 1])
    for n in range(0, len(heads), HEADS_PER_VREG):
        o_ref[0, :, lanes[n]] = jnp.where(qmask[0], outs[n], outs[n + 1]).astype(BF16)


def band_attn(q, k, v, bias, *, tq, band):
    b, sq, w = q.shape
    sk = k.shape[1]
    n_off, n_heads = bias.shape[:2]
    assert band % KEY_CHUNK == 0 and tq % CHUNK == 0 or sq == tq
    kern = functools.partial(_band_attn_kernel, tq=tq, band=band)
    return pl.pallas_call(
        kern, grid=(b, sq // tq),
        in_specs=[pl.BlockSpec((1, tq, w), lambda i, c: (i, c, 0)),
                  pl.BlockSpec((1, sk, w), lambda i, c: (i, 0, 0)),
                  pl.BlockSpec((1, sk, w), lambda i, c: (i, 0, 0)),
                  pl.BlockSpec((1, n_heads, tq, band),
                               lambda i, c: (jnp.minimum(c, n_off - 1), 0, 0, 0))],
        out_specs=pl.BlockSpec((1, tq, w), lambda i, c: (i, c, 0)),
        out_shape=jax.ShapeDtypeStruct((b, sq, w), BF16),
        scratch_shapes=[pltpu.VMEM((HEADS_PER_VREG, sk, w), BF16),
                        pltpu.VMEM((2, tq, band), F32),
                        pltpu.VMEM((2, tq, band), BF16)],
        compiler_params=_cparams("parallel", "arbitrary"), name="band_attn")(q, k, v, bias)


STICK_UNDERFLOW_LOG2 = 160.0
STICK_NEAR_CHUNKS = 2


def _stick_kernel(q_ref, k_ref, v_ref, o_ref, v16, zl_scr, hl_scr, w_scr, *,
                  tq, q_offset, variants):
    qi = pl.program_id(2)
    ck = KEY_CHUNK
    n_heads = HEADS_PER_VREG

    @pl.when(qi == 0)
    def _():
        v = v_ref[0]
        vmask = _head_masks(v.shape)
        for j in range(n_heads):
            v16[j] = jnp.where(vmask[j], v, jnp.zeros((), BF16))

    q = q_ref[0]
    qmask = _head_masks((tq, LANES))
    qs = [jnp.where(qmask[j], q, jnp.zeros((), BF16)) for j in range(n_heads)]
    q_pos = q_offset + qi * tq + lax.broadcasted_iota(jnp.int32, (tq, ck), 0)
    k_iota = lax.broadcasted_iota(jnp.int32, (tq, ck), 1)
    rr = lax.broadcasted_iota(jnp.int32, (ck, ck), 0)
    cc = lax.broadcasted_iota(jnp.int32, (ck, ck), 1)
    tri = jnp.where(rr > cc, 1.0, 0.0).astype(BF16)
    n_chunks = (q_offset + (qi + 1) * tq + ck - 1) // ck

    def logits(j, slot, c, masked):
        k0 = pl.multiple_of(c * ck, ck)
        z = _dot_nt(qs[j], k_ref[0, pl.ds(k0, ck), :])
        sp = jnp.maximum(z, 0.0) + jnp.log2(1.0 + jnp.exp2(-jnp.abs(z)))
        if masked:
            sp = jnp.where(k0 + k_iota < q_pos, sp, 0.0)
        zl_scr[j, :, slot] = z - sp
        hi, lo = _split2(sp)
        hl_scr[0, j, :, slot] = hi
        hl_scr[1, j, :, slot] = lo
        return jnp.sum(sp, axis=-1, keepdims=True)

    def weights(j, slot, c, later, masked):
        sums = _dot(hl_scr[0, j, :, slot], tri) + _dot(hl_scr[1, j, :, slot], tri)
        w = jnp.exp2(zl_scr[j, :, slot] - sums - later)
        if masked:
            w = jnp.where(c * ck + k_iota < q_pos, w, 0.0)
        w_scr[j, :, slot] = w.astype(BF16)

    def more(laters):
        return (jnp.min(functools.reduce(jnp.minimum, laters))
                < STICK_UNDERFLOW_LOG2).astype(jnp.int32)

    def attend(near, n_masked):
        slots = [slice((near - 1 - i) * ck, (near - i) * ck) for i in range(near)]
        chunk = [n_chunks - 1 - i for i in range(near)]
        k0 = pl.multiple_of((n_chunks - near) * ck, ck)

        def pv(j):
            return _dot(w_scr[j, :, :near * ck], v16[j, pl.ds(k0, near * ck), :])

        laters, acc = [], None
        row_sums = [logits(0, slots[i], chunk[i], i < n_masked) for i in range(near)]
        for j in range(n_heads):
            later = jnp.zeros((tq, 1), F32)
            next_sums = []
            for i in range(near):
                if j + 1 < n_heads:
                    next_sums.append(logits(j + 1, slots[i], chunk[i], i < n_masked))
                weights(j, slots[i], chunk[i], later, i < n_masked)
                later = later + row_sums[i]
            laters.append(later)
            row_sums = next_sums
            acc = pv(j) if acc is None else acc + pv(j)

        def cond(carry):
            return (carry[0] >= 0) & (carry[1] > 0)

        def body(carry):
            c, _, acc, laters = carry
            k0 = pl.multiple_of(c * ck, ck)
            new = []
            for j in range(n_heads):
                rs = logits(j, slice(0, ck), c, False)
                weights(j, slice(0, ck), c, laters[j], False)
                acc = acc + _dot(w_scr[j, :, :ck], v16[j, pl.ds(k0, ck), :])
                new.append(laters[j] + rs)
            return c - 1, more(new), acc, tuple(new)

        carry = lax.while_loop(cond, body, (n_chunks - 1 - near, more(laters), acc, tuple(laters)))
        o_ref[0] = carry[2].astype(BF16)

    for n_total, near, n_masked in variants:
        if n_total is None:
            pl.when(n_chunks >= near)(functools.partial(attend, near, n_masked))
        else:
            pl.when(n_chunks == n_total)(functools.partial(attend, near, n_masked))


def stick_attn(q, k, v, *, tq, q_offset):
    b, sq, w = q.shape
    sk = k.shape[1]
    ck = KEY_CHUNK
    assert q_offset % ck == 0 and (tq % ck == 0 or sq == tq <= ck) and sk % ck == 0
    n_masked = -(-tq // ck)
    totals = sorted({-(-(q_offset + (t + 1) * tq) // ck) for t in range(sq // tq)})
    variants = [(n, n, min(n_masked, n)) for n in totals if n < STICK_NEAR_CHUNKS]
    if totals[-1] >= STICK_NEAR_CHUNKS:
        variants.append((None, STICK_NEAR_CHUNKS, n_masked))
    near_cols = STICK_NEAR_CHUNKS * ck
    kern = functools.partial(_stick_kernel, tq=tq, q_offset=q_offset, variants=tuple(variants))
    return pl.pallas_call(
        kern, grid=(b, w // LANES, sq // tq),
        in_specs=[pl.BlockSpec((1, tq, LANES), lambda i, h, t: (i, t, h)),
                  pl.BlockSpec((1, sk, LANES), lambda i, h, t: (i, 0, h)),
                  pl.BlockSpec((1, sk, LANES), lambda i, h, t: (i, 0, h))],
        out_specs=pl.BlockSpec((1, tq, LANES), lambda i, h, t: (i, t, h)),
        out_shape=jax.ShapeDtypeStruct((b, sq, w), BF16),
        scratch_shapes=[pltpu.VMEM((HEADS_PER_VREG, sk, LANES), BF16),
                        pltpu.VMEM((HEADS_PER_VREG, tq, near_cols), F32),
                        pltpu.VMEM((2, HEADS_PER_VREG, tq, near_cols), BF16),
                        pltpu.VMEM((HEADS_PER_VREG, tq, near_cols), BF16)],
        compiler_params=_cparams("parallel", "parallel", "arbitrary"), name="stick_attn")(q, k, v)


def _cumsum_kernel(lf_ref, lft_ref, col_ref, row_ref, *, blk):
    s_len, n_heads = lf_ref.shape[1], lf_ref.shape[2]
    rr = lax.broadcasted_iota(jnp.int32, (blk, blk), 0)
    cc = lax.broadcasted_iota(jnp.int32, (blk, blk), 1)
    lower = jnp.where(rr >= cc, 1.0, 0.0).astype(BF16)
    upper = jnp.where(rr <= cc, 1.0, 0.0).astype(BF16)
    carry_col = jnp.zeros((1, n_heads), F32)
    carry_row = jnp.zeros((n_heads, 1), F32)
    for n in range(s_len // blk):
        rows = slice(n * blk, (n + 1) * blk)
        c = sum(_dot(lower, p) for p in _split3(lf_ref[0, rows, :])) + carry_col
        col_ref[0, rows, :] = c
        carry_col = c[blk - 1:blk, :]
        ct = sum(_dot(p, upper) for p in _split3(lft_ref[0, :, rows])) + carry_row
        row_ref[0, :, rows] = ct
        carry_row = ct[:, blk - 1:blk]


def cumsum_logf(lf):
    b, s_len, n_heads = lf.shape
    kern = functools.partial(_cumsum_kernel, blk=LANES)
    return pl.pallas_call(
        kern, grid=(b,),
        in_specs=[pl.BlockSpec((1, s_len, n_heads), lambda i: (i, 0, 0)),
                  pl.BlockSpec((1, n_heads, s_len), lambda i: (i, 0, 0))],
        out_specs=[pl.BlockSpec((1, s_len, n_heads), lambda i: (i, 0, 0)),
                   pl.BlockSpec((1, n_heads, s_len), lambda i: (i, 0, 0))],
        out_shape=[jax.ShapeDtypeStruct((b, s_len, n_heads), F32),
                   jax.ShapeDtypeStruct((b, n_heads, s_len), F32)],
        compiler_params=_cparams("parallel"), name="cumsum_logf")(lf, jnp.swapaxes(lf, 1, 2))


def _key_extent(qi, *, tq, q_offset, bucket, sk):
    return jnp.minimum((q_offset + (qi + 1) * tq + bucket - 1) // bucket * bucket, sk)


def _forget_kernel(q_ref, k_ref, v_ref, cq_ref, ck_ref, o_ref, v16, s_scr, p_scr, *,
                   tq, q_offset, bucket, variants):
    hp = pl.program_id(1)
    qi = pl.program_id(2)
    sk = k_ref.shape[1]
    ck_w = KEY_CHUNK
    ones_lane = _ones_lanes()

    @pl.when(qi == 0)
    def _():
        v = v_ref[0]
        for j in range(HEADS_PER_VREG):
            v16[j] = _masked_values(v, j)

    q = q_ref[0]
    qmask = _head_masks((tq, LANES))
    q_pos = q_offset + qi * tq + lax.broadcasted_iota(jnp.int32, (tq, ck_w), 0)
    k_iota = lax.broadcasted_iota(jnp.int32, (tq, ck_w), 1)
    cq_all = cq_ref[0]
    head_lane = lax.broadcasted_iota(jnp.int32, cq_all.shape, 1)
    n_k_here = _key_extent(qi, tq=tq, q_offset=q_offset, bucket=bucket, sk=sk)

    def attend(n_k, mask_from):
        n_c = n_k // ck_w
        cols = [slice(c * ck_w, (c + 1) * ck_w) for c in range(n_c)]
        allowed = {c: c * ck_w + k_iota <= q_pos for c in range(mask_from, n_c)}
        qj = [jnp.where(qmask[j], q, jnp.zeros((), BF16)) for j in range(HEADS_PER_VREG)]
        cq = [LOG2E * jnp.sum(jnp.where(head_lane == hp * HEADS_PER_VREG + j, cq_all, 0.0),
                              axis=-1, keepdims=True) for j in range(HEADS_PER_VREG)]

        def logits(j, c, mrun):
            s = _dot_nt(qj[j], k_ref[0, cols[c], :]) - LOG2E * ck_ref[0, 0, j:j + 1, cols[c]]
            if c in allowed:
                s = jnp.where(allowed[c], s, NEG_INF)
            s_scr[j, :, cols[c]] = s
            for part in range(ck_w // LANES):
                mrun = jnp.maximum(mrun, s[:, part * LANES:(part + 1) * LANES])
            return mrun

        def row_term(j, mrun):
            m = jnp.max(mrun, axis=-1, keepdims=True) + cq[j]
            return cq[j] - m

        def probs(j, c, row):
            p_scr[j, :, cols[c]] = jnp.exp2(s_scr[j, :, cols[c]] + row).astype(BF16)

        def pv(j):
            return _dot(p_scr[j, :, :n_k], v16[j, :n_k, :])

        start = jnp.full((tq, LANES), NEG_INF, F32)
        row0 = row_term(0, functools.reduce(lambda m, c: logits(0, c, m), range(n_c), start))
        mrun = start
        for c in range(n_c):
            mrun = logits(1, c, mrun)
            probs(0, c, row0)
        row1 = row_term(1, mrun)
        o0 = pv(0)
        for c in range(n_c):
            probs(1, c, row1)
        o1 = pv(1)
        l0 = o0[:, ones_lane[0]:ones_lane[0] + 1]
        l1 = o1[:, ones_lane[1]:ones_lane[1] + 1]
        o_ref[0] = jnp.where(qmask[0], o0 / l0, o1 / l1).astype(BF16)

    for n_k, mask_from in variants:
        pl.when(n_k_here == n_k)(functools.partial(attend, n_k, mask_from))


def forget_attn(q, k, v, cum_q, cum_k_rows, *, tq, bucket, q_offset):
    b, sq, w = q.shape
    sk = k.shape[1]
    n_heads = cum_q.shape[2]
    assert sk % KEY_CHUNK == 0 and bucket % KEY_CHUNK == 0
    extent = lambda t: min(-(-(q_offset + (t + 1) * tq) // bucket) * bucket, sk)
    variants = []
    for n_k in sorted({extent(t) for t in range(sq // tq)}):
        first_row = min(q_offset + t * tq for t in range(sq // tq) if extent(t) == n_k)
        variants.append((n_k, first_row // KEY_CHUNK))
    ck = cum_k_rows.reshape(b, n_heads // HEADS_PER_VREG, HEADS_PER_VREG, sk)
    kern = functools.partial(_forget_kernel, tq=tq, q_offset=q_offset, bucket=bucket,
                             variants=tuple(variants))
    n_k_max = variants[-1][0]
    return pl.pallas_call(
        kern, grid=(b, w // LANES, sq // tq),
        in_specs=[pl.BlockSpec((1, tq, LANES), lambda i, h, t: (i, t, h)),
                  pl.BlockSpec((1, sk, LANES), lambda i, h, t: (i, 0, h)),
                  pl.BlockSpec((1, sk, LANES), lambda i, h, t: (i, 0, h)),
                  pl.BlockSpec((1, tq, n_heads), lambda i, h, t: (i, t, 0)),
                  pl.BlockSpec((1, 1, HEADS_PER_VREG, sk), lambda i, h, t: (i, h, 0, 0))],
        out_specs=pl.BlockSpec((1, tq, LANES), lambda i, h, t: (i, t, h)),
        out_shape=jax.ShapeDtypeStruct((b, sq, w), BF16),
        scratch_shapes=[pltpu.VMEM((HEADS_PER_VREG, sk, LANES), BF16),
                        pltpu.VMEM((HEADS_PER_VREG, tq, n_k_max), F32),
                        pltpu.VMEM((HEADS_PER_VREG, tq, n_k_max), BF16)],
        compiler_params=_cparams("parallel", "parallel", "arbitrary"),
        name="forget_attn")(q, k, v, cum_q, ck)


def _xattn_kernel(x_ref, g_ref, wq_ref, wo_ref, mk_ref, mv_ref, o_ref, *, n_heads):
    x = x_ref[...]
    d_model = x.shape[1]
    hd = d_model // n_heads
    h = _rms_bf16(x, g_ref[...])
    q = (_dot(h, wq_ref[...]) * (hd ** -0.5)).astype(BF16)
    outs = []
    for j in range(n_heads):
        cols = slice(j * hd, (j + 1) * hd)
        s = _dot_nt(q[:, cols], mk_ref[0, :, cols])
        p = jnp.exp(s - jnp.max(s, axis=-1, keepdims=True))
        p = p / jnp.sum(p, axis=-1, keepdims=True)
        outs.append(_dot(p.astype(BF16), mv_ref[0, :, cols]).astype(BF16))
    o = jnp.concatenate(outs, axis=-1)
    o_ref[...] = x + _dot(o, wo_ref[...])


def xattn(x, g, wq, wo, mk, mv, *, seq, tm):
    m, d = x.shape
    n_mem = mk.shape[1]
    tiles_per_seq = seq // tm
    kern = functools.partial(_xattn_kernel, n_heads=XA_HEADS)
    return pl.pallas_call(
        kern, grid=(m // tm,),
        in_specs=[pl.BlockSpec((tm, d), lambda i: (i, 0)),
                  pl.BlockSpec((1, d), lambda i: (0, 0)),
                  pl.BlockSpec((d, d), lambda i: (0, 0)),
                  pl.BlockSpec((d, d), lambda i: (0, 0)),
                  pl.BlockSpec((1, n_mem, d), lambda i: (i // tiles_per_seq, 0, 0)),
                  pl.BlockSpec((1, n_mem, d), lambda i: (i // tiles_per_seq, 0, 0))],
        out_specs=pl.BlockSpec((tm, d), lambda i: (i, 0)),
        out_shape=jax.ShapeDtypeStruct((m, d), F32),
        compiler_params=_cparams("parallel"), name="xattn")(x, g.reshape(1, d), wq, wo, mk, mv)


def _ffn_kernel(x_ref, g_ref, wg_ref, wu_ref, wd_ref, gf_ref, o_ref, *, tf, final_norm):
    x = x_ref[...]
    h = _rms_bf16(x, g_ref[...])
    acc = x
    for c in range(wg_ref.shape[1] // tf):
        cols = slice(c * tf, (c + 1) * tf)
        gate = _dot(h, wg_ref[:, cols])
        up = _dot(h, wu_ref[:, cols])
        a = (gate * jax.nn.sigmoid(gate) * up).astype(BF16)
        acc = acc + _dot(a, wd_ref[cols, :])
    if final_norm:
        acc = acc * lax.rsqrt(jnp.mean(acc * acc, axis=-1, keepdims=True) + RMS_EPS) * gf_ref[...]
    o_ref[...] = acc


def ffn(x, g, wg, wu, wd, g_final, *, tm, tf, final_norm):
    m, d = x.shape
    dff = wg.shape[1]
    kern = functools.partial(_ffn_kernel, tf=tf, final_norm=final_norm)
    return pl.pallas_call(
        kern, grid=(m // tm,),
        in_specs=[pl.BlockSpec((tm, d), lambda i: (i, 0)),
                  pl.BlockSpec((1, d), lambda i: (0, 0)),
                  pl.BlockSpec((d, dff), lambda i: (0, 0)),
                  pl.BlockSpec((d, dff), lambda i: (0, 0)),
                  pl.BlockSpec((dff, d), lambda i: (0, 0)),
                  pl.BlockSpec((1, d), lambda i: (0, 0))],
        out_specs=pl.BlockSpec((tm, d), lambda i: (i, 0)),
        out_shape=jax.ShapeDtypeStruct((m, d), F32),
        compiler_params=_cparams("parallel"),
        name="ffn")(x, g.reshape(1, d), wg, wu, wd, g_final.reshape(1, d))


def _pad_rows(a, rows):
    return jnp.pad(a, ((0, 0), (0, rows - a.shape[1]), (0, 0)))


def _row_tile(m, seq, cap):
    tm = min(cap, seq)
    assert seq % tm == 0 and m % tm == 0
    return tm


def _layer_ab(x, seq, g, w_in, w_out, rel_bias, cache, *, tm):
    m, d = x.shape
    b = m // seq
    wa = w_in.shape[1] // 6
    ws = [w_in[:, n * wa:(n + 1) * wa].astype(BF16) for n in range(6)]
    scale = HEAD_DIM ** -0.5 * LOG2E
    kept = 'tail' if min(A_PAST, seq) < seq else 'heads'
    outs = [(0, 'bf16', scale), (1, 'bf16', 1.0), (2, 'bf16', 1.0),
            (1, kept, HEAD_DIM), (2, kept, HEAD_DIM),
            (3, 'bf16', scale), (4, 'bf16', 1.0), (5, 'bf16', 1.0),
            (4, 'heads', HEAD_DIM), (5, 'heads', HEAD_DIM)]
    qa, ka16, va16, ka_keep, va_keep, qb, kb16, vb16, kb, vb = norm_proj(x, g, ws, outs, seq=seq,
                                                                         tm=tm)
    shp = lambda a: a.reshape(b, -1, a.shape[-1])
    if cache is None:
        tq = 4 * CHUNK
        band = -(-(A_PAST + tq) // KEY_CHUNK) * KEY_CHUNK
        offsets = sorted({t * tq - max(t * (tq // CHUNK) - A_PAST_CHUNKS, 0) * CHUNK
                          for t in range(seq // tq)})
        bias = band_bias(rel_bias, offsets, rows=tq, cols=band, n_real=band)
        oa = band_attn(shp(qa), shp(ka16), shp(va16), bias, tq=tq, band=band)
        ob = stick_attn(shp(qb), shp(kb16), shp(vb16), tq=256, q_offset=0)
    else:
        ca_k, ca_v, cb_k, cb_v = cache
        n_past = ca_k.shape[1]
        n_keys = n_past + seq
        band = -(-n_keys // KEY_CHUNK) * KEY_CHUNK
        flat = lambda a: a.reshape(a.shape[0], a.shape[1], -1)
        k_all = _pad_rows(jnp.concatenate([flat(ca_k).astype(BF16), shp(ka16)], axis=1), band)
        v_all = _pad_rows(jnp.concatenate([flat(ca_v).astype(BF16), shp(va16)], axis=1), band)
        bias = band_bias(rel_bias, [n_past], rows=seq, cols=band, n_real=n_keys)
        oa = band_attn(shp(qa), k_all, v_all, bias, tq=seq, band=band)
        n_pastb = cb_k.shape[1]
        sk = -(-(n_pastb + seq) // KEY_CHUNK) * KEY_CHUNK
        kb_all = _pad_rows(jnp.concatenate([flat(cb_k).astype(BF16), shp(kb16)], axis=1), sk)
        vb_all = _pad_rows(jnp.concatenate([flat(cb_v).astype(BF16), shp(vb16)], axis=1), sk)
        ob = stick_attn(shp(qb), kb_all, vb_all, tq=seq, q_offset=n_pastb)
    w_out = w_out.astype(BF16)
    wo_a, wo_b = w_out[:oa.shape[-1]], w_out[oa.shape[-1]:]
    x = proj_res([oa.reshape(m, -1), ob.reshape(m, -1)], [wo_a, wo_b], x, tm=tm)
    return x, (ka_keep, va_keep, kb, vb)


def _layer_c(x, seq, g, w_in, b_f, w_out, cache, *, tm):
    m, d = x.shape
    b = m // seq
    n_heads = b_f.shape[0]
    wq, wk, wv, wf = (w_in[:, :d], w_in[:, d:2 * d], w_in[:, 2 * d:3 * d], w_in[:, 3 * d:])
    ws = [w.astype(BF16) for w in (wq, wk, wv, wf)]
    outs = [(0, 'bf16', HEAD_DIM ** -0.5 * LOG2E), (1, 'bf16', 1.0), (2, 'bf16', 1.0),
            (1, 'heads', HEAD_DIM), (2, 'heads', HEAD_DIM), (3, 'logf', None)]
    q, k16, v16, k, v, lf = norm_proj(x, g, ws, outs, seq=seq, tm=tm, bias=b_f.reshape(1, n_heads))
    shp = lambda a: a.reshape(b, -1, a.shape[-1])
    if cache is None:
        cum_col, cum_row = cumsum_logf(shp(lf))
        o = forget_attn(shp(q), shp(k16), shp(v16), cum_col, cum_row, tq=256, bucket=KEY_CHUNK,
                        q_offset=0)
    else:
        c_k, c_v, c_lf = cache
        n_past = c_k.shape[1]
        sk = -(-(n_past + seq) // KEY_CHUNK) * KEY_CHUNK
        flat = lambda a: a.reshape(a.shape[0], a.shape[1], -1)
        k_all = _pad_rows(jnp.concatenate([flat(c_k).astype(BF16), shp(k16)], axis=1), sk)
        v_all = _pad_rows(jnp.concatenate([flat(c_v).astype(BF16), shp(v16)], axis=1), sk)
        lf_all = _pad_rows(jnp.concatenate([c_lf, shp(lf)], axis=1), sk)
        cum_col, cum_row = cumsum_logf(lf_all)
        o = forget_attn(shp(q), k_all, v_all, cum_col[:, n_past:n_past + seq], cum_row,
                        tq=seq, bucket=KEY_CHUNK, q_offset=n_past)
    x = proj_res([o.reshape(m, d)], [w_out.astype(BF16)], x, tm=tm)
    return x, (k, v, lf)


def kernel(x_prompt, x_sample, cache_a_k, cache_a_v, cache_b_k, cache_b_v, cache_c_k, cache_c_v, cache_c_logf, cache_mem_k, cache_mem_v, mem_prompt, w_in_ab, w_out_ab, rel_bias_a, w_in_c, b_f_c, w_out_c, g_mix, g_xattn, g_mem, w_xq, w_xk, w_xv, w_xo, g_ffn, w_gate, w_up, w_down, g_final):
    bp, sp, d = x_prompt.shape
    bs, ss, _ = x_sample.shape
    depth = g_mix.shape[0]
    n_mem = mem_prompt.shape[1]
    xp = x_prompt.reshape(bp * sp, d)
    xs = x_sample.reshape(bs * ss, d)
    tmp = _row_tile(bp * sp, sp, 512)
    tms = bs * ss
    assert tms <= 512
    mem = mem_prompt.reshape(bp * n_mem, d)
    tmm = _row_tile(bp * n_mem, n_mem, 512)
    dff = w_gate.shape[2]
    tf = 256 if dff % 256 == 0 else dff

    a_kp, a_vp, b_kp, b_vp, a_ks, a_vs, b_ks, b_vs = [], [], [], [], [], [], [], []
    c_kp, c_vp, c_lfp, c_ks, c_vs, c_lfs = [], [], [], [], [], []
    mem_kp, mem_vp = [], []
    for layer in range(depth):
        if layer % 2 == 0:
            e = layer // 2
            xp, (ka, va, kb, vb) = _layer_ab(xp, sp, g_mix[layer], w_in_ab[e], w_out_ab[e],
                                             rel_bias_a[e], None, tm=tmp)
            a_kp.append(ka); a_vp.append(va); b_kp.append(kb); b_vp.append(vb)
            xs, (ka, va, kb, vb) = _layer_ab(xs, ss, g_mix[layer], w_in_ab[e], w_out_ab[e],
                                             rel_bias_a[e],
                                             (cache_a_k[e], cache_a_v[e], cache_b_k[e], cache_b_v[e]),
                                             tm=tms)
            a_ks.append(ka); a_vs.append(va); b_ks.append(kb); b_vs.append(vb)
        else:
            c = layer // 2
            xp, (k, v, lf) = _layer_c(xp, sp, g_mix[layer], w_in_c[c], b_f_c[c], w_out_c[c], None,
                                      tm=tmp)
            c_kp.append(k); c_vp.append(v); c_lfp.append(lf)
            xs, (k, v, lf) = _layer_c(xs, ss, g_mix[layer], w_in_c[c], b_f_c[c], w_out_c[c],
                                      (cache_c_k[c], cache_c_v[c], cache_c_logf[c]), tm=tms)
            c_ks.append(k); c_vs.append(v); c_lfs.append(lf)
        mk, mv, mk16, mv16 = norm_proj(
            mem, g_mem[layer], [w_xk[layer].astype(BF16), w_xv[layer].astype(BF16)],
            [(0, 'heads', d // XA_HEADS), (1, 'heads', d // XA_HEADS), (0, 'bf16', 1.0),
             (1, 'bf16', 1.0)],
            seq=n_mem, tm=tmm)
        mem_kp.append(mk); mem_vp.append(mv)
        wq16, wo16 = w_xq[layer].astype(BF16), w_xo[layer].astype(BF16)
        xp = xattn(xp, g_xattn[layer], wq16, wo16, mk16.reshape(bp, n_mem, d),
                   mv16.reshape(bp, n_mem, d), seq=sp, tm=tmp)
        xs = xattn(xs, g_xattn[layer], wq16, wo16,
                   cache_mem_k[layer].reshape(bs, n_mem, d).astype(BF16),
                   cache_mem_v[layer].reshape(bs, n_mem, d).astype(BF16), seq=ss, tm=ss)
        last = layer == depth - 1
        wg16, wu16, wd16 = (w_gate[layer].astype(BF16), w_up[layer].astype(BF16),
                            w_down[layer].astype(BF16))
        xp = ffn(xp, g_ffn[layer], wg16, wu16, wd16, g_final, tm=tmp, tf=tf, final_norm=last)
        xs = ffn(xs, g_ffn[layer], wg16, wu16, wd16, g_final, tm=tms, tf=tf, final_norm=last)

    hd = HEAD_DIM
    xa_hd = d // XA_HEADS
    r5 = lambda lst, b, s, dd: jnp.stack([a.reshape(b, s, -1, dd) for a in lst])
    r4 = lambda lst, b, s: jnp.stack([a.reshape(b, s, -1) for a in lst])
    keep = min(A_PAST, sp)
    return (xp.reshape(bp, sp, d), xs.reshape(bs, ss, d),
            r5(a_kp, bp, keep, hd), r5(a_vp, bp, keep, hd), r5(b_kp, bp, sp, hd), r5(b_vp, bp, sp, hd),
            r5(c_kp, bp, sp, hd), r5(c_vp, bp, sp, hd), r4(c_lfp, bp, sp),
            r5(mem_kp, bp, n_mem, xa_hd), r5(mem_vp, bp, n_mem, xa_hd),
            r5(a_ks, bs, ss, hd), r5(a_vs, bs, ss, hd), r5(b_ks, bs, ss, hd), r5(b_vs, bs, ss, hd),
            r5(c_ks, bs, ss, hd), r5(c_vs, bs, ss, hd), r4(c_lfs, bs, ss))
```

```python
import functools

import jax
import jax.numpy as jnp
from jax import lax
from jax.experimental import pallas as pl
from jax.experimental.pallas import tpu as pltpu

F32 = jnp.float32
BF16 = jnp.bfloat16

RMS_EPS = 1e-6
NEG_INF = -1e30
LOG2E = 1.4426950408889634
HEAD_DIM = 64
CHUNK = 64
A_PAST_CHUNKS = 8
A_PAST = A_PAST_CHUNKS * CHUNK
REL_CLIP = 128
XA_HEADS = 4

LANES = 128
HEADS_PER_VREG = LANES // HEAD_DIM
KEY_CHUNK = 256
ATTN_PAIRS_PER_STEP = 2
VMEM_LIMIT = 56 * 1024 * 1024


def _cparams(*sem):
    return pltpu.CompilerParams(dimension_semantics=sem, vmem_limit_bytes=VMEM_LIMIT)


def _resident_spec(shape):
    return pl.BlockSpec(shape, lambda *_: (0,) * len(shape), pipeline_mode=pl.Buffered(1))


def _rms_bf16(x, g):
    y = x * lax.rsqrt(jnp.mean(x * x, axis=-1, keepdims=True) + RMS_EPS)
    return (y * g).astype(BF16)


def _log_sigmoid(z):
    return jnp.minimum(z, 0.0) - jnp.log1p(jnp.exp(-jnp.abs(z)))


def _split3(x):
    hi = x.astype(BF16)
    r = x - hi.astype(F32)
    mid = r.astype(BF16)
    lo = (r - mid.astype(F32)).astype(BF16)
    return hi, mid, lo


def _split2(x):
    hi = x.astype(BF16)
    lo = (x - hi.astype(F32)).astype(BF16)
    return hi, lo


def _dot(a, b):
    return jnp.dot(a, b, preferred_element_type=F32)


def _dot_nt(a, b):
    return lax.dot_general(a, b, (((1,), (1,)), ((), ())), preferred_element_type=F32)


def _norm_proj_kernel(x_ref, g_ref, b_ref, *refs, n_w, outs, tiles_per_seq):
    w_refs, o_refs = refs[:n_w], refs[n_w:]
    h = _rms_bf16(x_ref[...], g_ref[...])
    ys = {}
    for o_ref, (grp, kind, arg) in zip(o_refs, outs):
        if grp not in ys:
            ys[grp] = _dot(h, w_refs[grp][...])
        y = ys[grp]
        if kind == 'f32':
            o_ref[...] = y
        elif kind == 'bf16':
            o_ref[...] = (y * arg).astype(BF16)
        elif kind == 'logf':
            o_ref[...] = _log_sigmoid(y + b_ref[...])
        elif kind == 'heads':
            o_ref[...] = y.reshape(o_ref.shape)
        else:
            @pl.when(pl.program_id(0) % tiles_per_seq == tiles_per_seq - 1)
            def _(o_ref=o_ref, y=y):
                o_ref[...] = y.reshape(o_ref.shape)


def norm_proj(x, g, ws, outs, *, seq, tm, bias=None):
    m, d = x.shape
    tiles_per_seq = max(seq // tm, 1)
    if bias is None:
        bias = jnp.zeros((1, 16), F32)
    in_specs = [pl.BlockSpec((tm, d), lambda i: (i, 0)),
                pl.BlockSpec((1, d), lambda i: (0, 0)),
                pl.BlockSpec(bias.shape, lambda i: (0, 0))]
    in_specs += [_resident_spec(w.shape) for w in ws]
    out_shape, out_specs = [], []
    for grp, kind, arg in outs:
        n = ws[grp].shape[1]
        if kind == 'tail':
            assert tm == min(A_PAST, seq)
            out_shape.append(jax.ShapeDtypeStruct((m // tiles_per_seq, n // arg, arg), F32))
            out_specs.append(pl.BlockSpec((tm, n // arg, arg), lambda i: (i // tiles_per_seq, 0, 0)))
        elif kind == 'heads':
            out_shape.append(jax.ShapeDtypeStruct((m, n // arg, arg), F32))
            out_specs.append(pl.BlockSpec((tm, n // arg, arg), lambda i: (i, 0, 0)))
        else:
            out_shape.append(jax.ShapeDtypeStruct((m, n), BF16 if kind == 'bf16' else F32))
            out_specs.append(pl.BlockSpec((tm, n), lambda i: (i, 0)))
    kern = functools.partial(_norm_proj_kernel, n_w=len(ws), outs=tuple(outs),
                             tiles_per_seq=tiles_per_seq)
    return pl.pallas_call(
        kern, grid=(m // tm,), in_specs=in_specs, out_specs=out_specs, out_shape=out_shape,
        compiler_params=_cparams("arbitrary"), name="norm_proj")(x, g.reshape(1, d), bias, *ws)


BIAS_BLOCK_ROWS = 32


def _band_bias_kernel(rb_ref, off_ref, o_ref, *, n_real):
    n_heads, rows, cols = o_ref.shape[1:]
    off = off_ref[pl.program_id(0)]
    i0 = pl.program_id(1) * rows
    r0 = pl.program_id(2) * cols
    i = lax.broadcasted_iota(jnp.int32, (rows, cols), 0) + i0
    r = lax.broadcasted_iota(jnp.int32, (rows, cols), 1) + r0
    d = jnp.clip(off + i - r, -REL_CLIP, REL_CLIP) + REL_CLIP
    lo = jnp.clip(off + i0 - (r0 + cols - 1), -REL_CLIP, REL_CLIP) + REL_CLIP
    hi = jnp.clip(off + i0 + rows - 1 - r0, -REL_CLIP, REL_CLIP) + REL_CLIP

    def body(u, tbls):
        hit = d == u
        return tuple(jnp.where(hit, rb_ref[h, u], t) for h, t in enumerate(tbls))

    tbls = lax.fori_loop(lo, hi + 1, body,
                         tuple(jnp.zeros((rows, cols), F32) for _ in range(n_heads)))
    first = (i // CHUNK - A_PAST_CHUNKS) * CHUNK + off
    last = jnp.minimum((i // CHUNK + 1) * CHUNK + off, n_real)
    visible = (r >= first) & (r < last)
    for h in range(n_heads):
        o_ref[0, h] = jnp.where(visible, LOG2E * tbls[h], NEG_INF)


def band_bias(rel_bias, offsets, *, rows, cols, n_real):
    n_rel, n_heads = rel_bias.shape
    blk = min(rows, BIAS_BLOCK_ROWS)
    return pl.pallas_call(
        functools.partial(_band_bias_kernel, n_real=n_real),
        grid=(len(offsets), rows // blk, cols // LANES),
        in_specs=[pl.BlockSpec(memory_space=pltpu.SMEM), pl.BlockSpec(memory_space=pltpu.SMEM)],
        out_specs=pl.BlockSpec((1, n_heads, blk, LANES), lambda o, t, c: (o, 0, t, c)),
        out_shape=jax.ShapeDtypeStruct((len(offsets), n_heads, rows, cols), F32),
        compiler_params=_cparams("arbitrary", "arbitrary", "arbitrary"),
        name="band_bias")(rel_bias.T, jnp.asarray(offsets, jnp.int32))


def _head_masks(shape):
    lane = lax.broadcasted_iota(jnp.int32, shape, len(shape) - 1)
    return [(lane % LANES) // HEAD_DIM == j for j in range(HEADS_PER_VREG)]


def _ones_lanes():
    return [((j + 1) % HEADS_PER_VREG) * HEAD_DIM for j in range(HEADS_PER_VREG)]


def _masked_values(v, j):
    lane = lax.broadcasted_iota(jnp.int32, v.shape, 1) % LANES
    vj = jnp.where(lane // HEAD_DIM == j, v, jnp.zeros((), BF16))
    return jnp.where(lane == _ones_lanes()[j], jnp.ones((), BF16), vj)


def _band_attn_kernel(q_ref, k_ref, v_ref, bias_ref, o_ref, v16, s_scr, p_scr, *, tq, band):
    step = pl.program_id(1)
    ck = KEY_CHUNK

    @pl.when(step == 0)
    def _():
        v = v_ref[0]
        for j in range(HEADS_PER_VREG):
            v16[j] = _masked_values(v, j)

    start = pl.multiple_of(jnp.maximum(step * (tq // CHUNK) - A_PAST_CHUNKS, 0) * CHUNK, CHUNK)
    qmask = _head_masks((tq, LANES))
    ones_lane = _ones_lanes()
    cols = [slice(c * ck, (c + 1) * ck) for c in range(band // ck)]
    heads = [(hp, j) for hp in range(q_ref.shape[2] // LANES) for j in range(HEADS_PER_VREG)]
    lanes = [slice(hp * LANES, (hp + 1) * LANES) for hp, _ in heads]
    qs = [jnp.where(qmask[j], q_ref[0, :, lanes[n]], jnp.zeros((), BF16))
          for n, (_, j) in enumerate(heads)]

    def logits(n, c, mrun):
        k = k_ref[0, pl.ds(pl.multiple_of(start + c * ck, CHUNK), ck), lanes[n]]
        s = _dot_nt(qs[n], k) + bias_ref[0, n, :, cols[c]]
        s_scr[n % 2, :, cols[c]] = s
        for part in range(ck // LANES):
            mrun = jnp.maximum(mrun, s[:, part * LANES:(part + 1) * LANES])
        return mrun

    def probs(n, c, row_max):
        p_scr[n % 2, :, cols[c]] = jnp.exp2(s_scr[n % 2, :, cols[c]] - row_max).astype(BF16)

    def pv(n):
        return _dot(p_scr[n % 2], v16[heads[n][1], pl.ds(start, band), lanes[n]])

    neg = jnp.full((tq, LANES), NEG_INF, F32)
    outs = []
    mrun = functools.reduce(lambda m, c: logits(0, c, m), range(len(cols)), neg)
    for n in range(len(heads)):
        row_max = jnp.max(mrun, axis=-1, keepdims=True)
        mrun = neg
        for c in range(len(cols)):
            if n + 1 < len(heads):
                mrun = logits(n + 1, c, mrun)
            probs(n, c, row_max)
        o = pv(n)
        lane = ones_lane[heads[n][1]]
        outs.append(o / o[:, lane:lane + 1])
    for n in range(0, len(heads), HEADS_PER_VREG):
        o_ref[0, :, lanes[n]] = jnp.where(qmask[0], outs[n], outs[n + 1]).astype(BF16)


def band_attn(q, k, v, bias, *, tq, band):
    b, sq, w = q.shape
    sk = k.shape[1]
    n_off, n_heads = bias.shape[:2]
    assert band % KEY_CHUNK == 0 and tq % CHUNK == 0 or sq == tq
    kern = functools.partial(_band_attn_kernel, tq=tq, band=band)
    return pl.pallas_call(
        kern, grid=(b, sq // tq),
        in_specs=[pl.BlockSpec((1, tq, w), lambda i, c: (i, c, 0)),
                  pl.BlockSpec((1, sk, w), lambda i, c: (i, 0, 0)),
                  pl.BlockSpec((1, sk, w), lambda i, c: (i, 0, 0)),
                  pl.BlockSpec((1, n_heads, tq, band),
                               lambda i, c: (jnp.minimum(c, n_off - 1), 0, 0, 0))],
        out_specs=pl.BlockSpec((1, tq, w), lambda i, c: (i, c, 0)),
        out_shape=jax.ShapeDtypeStruct((b, sq, w), BF16),
        scratch_shapes=[pltpu.VMEM((HEADS_PER_VREG, sk, w), BF16),
                        pltpu.VMEM((2, tq, band), F32),
                        pltpu.VMEM((2, tq, band), BF16)],
        compiler_params=_cparams("parallel", "arbitrary"), name="band_attn")(q, k, v, bias)


STICK_UNDERFLOW_LOG2 = 160.0
STICK_NEAR_CHUNKS = 2


def _stick_kernel(q_ref, k_ref, v_ref, o_ref, v16, zl_scr, hl_scr, w_scr, *,
                  tq, q_offset, variants):
    qi = pl.program_id(2)
    ck = KEY_CHUNK
    heads = [(pp, j) for pp in range(q_ref.shape[2] // LANES) for j in range(HEADS_PER_VREG)]
    n_heads = len(heads)
    lanes = [slice(pp * LANES, (pp + 1) * LANES) for pp, _ in heads]

    @pl.when(qi == 0)
    def _():
        v = v_ref[0]
        vmask = _head_masks(v.shape)
        for j in range(HEADS_PER_VREG):
            v16[j] = jnp.where(vmask[j], v, jnp.zeros((), BF16))

    qmask = _head_masks((tq, LANES))
    qs = [jnp.where(qmask[j], q_ref[0, :, lanes[n]], jnp.zeros((), BF16))
          for n, (_, j) in enumerate(heads)]
    q_pos = q_offset + qi * tq + lax.broadcasted_iota(jnp.int32, (tq, ck), 0)
    k_iota = lax.broadcasted_iota(jnp.int32, (tq, ck), 1)
    rr = lax.broadcasted_iota(jnp.int32, (ck, ck), 0)
    cc = lax.broadcasted_iota(jnp.int32, (ck, ck), 1)
    tri = jnp.where(rr > cc, 1.0, 0.0).astype(BF16)
    n_chunks = (q_offset + (qi + 1) * tq + ck - 1) // ck

    def logits(j, slot, c, masked):
        k0 = pl.multiple_of(c * ck, ck)
        z = _dot_nt(qs[j], k_ref[0, pl.ds(k0, ck), lanes[j]])
        sp = jnp.maximum(z, 0.0) + jnp.log2(1.0 + jnp.exp2(-jnp.abs(z)))
        if masked:
            sp = jnp.where(k0 + k_iota < q_pos, sp, 0.0)
        zl_scr[j, :, slot] = z - sp
        hi, lo = _split2(sp)
        hl_scr[0, j, :, slot] = hi
        hl_scr[1, j, :, slot] = lo
        return jnp.sum(sp, axis=-1, keepdims=True)

    def weights(j, slot, c, later, masked):
        sums = _dot(hl_scr[0, j, :, slot], tri) + _dot(hl_scr[1, j, :, slot], tri)
        w = jnp.exp2(zl_scr[j, :, slot] - sums - later)
        if masked:
            w = jnp.where(c * ck + k_iota < q_pos, w, 0.0)
        w_scr[j, :, slot] = w.astype(BF16)

    def more(laters):
        return (jnp.min(functools.reduce(jnp.minimum, laters))
                < STICK_UNDERFLOW_LOG2).astype(jnp.int32)

    def attend(near, n_masked):
        slots = [slice((near - 1 - i) * ck, (near - i) * ck) for i in range(near)]
        chunk = [n_chunks - 1 - i for i in range(near)]
        k0 = pl.multiple_of((n_chunks - near) * ck, ck)

        def pv(n, k0, width):
            return _dot(w_scr[n, :, :width], v16[heads[n][1], pl.ds(k0, width), lanes[n]])

        laters, accs = [], []
        row_sums = [logits(0, slots[i], chunk[i], i < n_masked) for i in range(near)]
        for n in range(n_heads):
            later = jnp.zeros((tq, 1), F32)
            next_sums = []
            for i in range(near):
                if n + 1 < n_heads:
                    next_sums.append(logits(n + 1, slots[i], chunk[i], i < n_masked))
                weights(n, slots[i], chunk[i], later, i < n_masked)
                later = later + row_sums[i]
            laters.append(later)
            row_sums = next_sums
            o = pv(n, k0, near * ck)
            if heads[n][1] == 0:
                accs.append(o)
            else:
                accs[-1] = accs[-1] + o

        def cond(carry):
            return (carry[0] >= 0) & (carry[1] > 0)

        def body(carry):
            c, _, accs, laters = carry
            accs = list(accs)
            k0 = pl.multiple_of(c * ck, ck)
            new = []
            for n in range(n_heads):
                rs = logits(n, slice(0, ck), c, False)
                weights(n, slice(0, ck), c, laters[n], False)
                accs[heads[n][0]] = accs[heads[n][0]] + pv(n, k0, ck)
                new.append(laters[n] + rs)
            return c - 1, more(new), tuple(accs), tuple(new)

        carry = lax.while_loop(cond, body,
                               (n_chunks - 1 - near, more(laters), tuple(accs), tuple(laters)))
        for pp, acc in enumerate(carry[2]):
            o_ref[0, :, pp * LANES:(pp + 1) * LANES] = acc.astype(BF16)

    for n_total, near, n_masked in variants:
        if n_total is None:
            pl.when(n_chunks >= near)(functools.partial(attend, near, n_masked))
        else:
            pl.when(n_chunks == n_total)(functools.partial(attend, near, n_masked))


def stick_attn(q, k, v, *, tq, q_offset):
    b, sq, w = q.shape
    sk = k.shape[1]
    ck = KEY_CHUNK
    assert q_offset % ck == 0 and (tq % ck == 0 or sq == tq <= ck) and sk % ck == 0
    n_masked = -(-tq // ck)
    totals = sorted({-(-(q_offset + (t + 1) * tq) // ck) for t in range(sq // tq)})
    variants = [(n, n, min(n_masked, n)) for n in totals if n < STICK_NEAR_CHUNKS]
    if totals[-1] >= STICK_NEAR_CHUNKS:
        variants.append((None, STICK_NEAR_CHUNKS, n_masked))
    near_cols = STICK_NEAR_CHUNKS * ck
    wg = ATTN_PAIRS_PER_STEP * LANES
    n_heads = ATTN_PAIRS_PER_STEP * HEADS_PER_VREG
    kern = functools.partial(_stick_kernel, tq=tq, q_offset=q_offset, variants=tuple(variants))
    return pl.pallas_call(
        kern, grid=(b, w // wg, sq // tq),
        in_specs=[pl.BlockSpec((1, tq, wg), lambda i, h, t: (i, t, h)),
                  pl.BlockSpec((1, sk, wg), lambda i, h, t: (i, 0, h)),
                  pl.BlockSpec((1, sk, wg), lambda i, h, t: (i, 0, h))],
        out_specs=pl.BlockSpec((1, tq, wg), lambda i, h, t: (i, t, h)),
        out_shape=jax.ShapeDtypeStruct((b, sq, w), BF16),
        scratch_shapes=[pltpu.VMEM((HEADS_PER_VREG, sk, wg), BF16),
                        pltpu.VMEM((n_heads, tq, near_cols), F32),
                        pltpu.VMEM((2, n_heads, tq, near_cols), BF16),
                        pltpu.VMEM((n_heads, tq, near_cols), BF16)],
        compiler_params=_cparams("parallel", "parallel", "arbitrary"), name="stick_attn")(q, k, v)


def _cumsum_kernel(lf_ref, lft_ref, col_ref, row_ref, *, blk):
    s_len, n_heads = lf_ref.shape[1], lf_ref.shape[2]
    rr = lax.broadcasted_iota(jnp.int32, (blk, blk), 0)
    cc = lax.broadcasted_iota(jnp.int32, (blk, blk), 1)
    lower = jnp.where(rr >= cc, 1.0, 0.0).astype(BF16)
    upper = jnp.where(rr <= cc, 1.0, 0.0).astype(BF16)
    carry_col = jnp.zeros((1, n_heads), F32)
    carry_row = jnp.zeros((n_heads, 1), F32)
    for n in range(s_len // blk):
        rows = slice(n * blk, (n + 1) * blk)
        c = sum(_dot(lower, p) for p in _split3(lf_ref[0, rows, :])) + carry_col
        col_ref[0, rows, :] = c
        carry_col = c[blk - 1:blk, :]
        ct = sum(_dot(p, upper) for p in _split3(lft_ref[0, :, rows])) + carry_row
        row_ref[0, :, rows] = ct
        carry_row = ct[:, blk - 1:blk]


def cumsum_logf(lf):
    b, s_len, n_heads = lf.shape
    kern = functools.partial(_cumsum_kernel, blk=LANES)
    return pl.pallas_call(
        kern, grid=(b,),
        in_specs=[pl.BlockSpec((1, s_len, n_heads), lambda i: (i, 0, 0)),
                  pl.BlockSpec((1, n_heads, s_len), lambda i: (i, 0, 0))],
        out_specs=[pl.BlockSpec((1, s_len, n_heads), lambda i: (i, 0, 0)),
                   pl.BlockSpec((1, n_heads, s_len), lambda i: (i, 0, 0))],
        out_shape=[jax.ShapeDtypeStruct((b, s_len, n_heads), F32),
                   jax.ShapeDtypeStruct((b, n_heads, s_len), F32)],
        compiler_params=_cparams("parallel"), name="cumsum_logf")(lf, jnp.swapaxes(lf, 1, 2))


def _key_extent(qi, *, tq, q_offset, bucket, sk):
    return jnp.minimum((q_offset + (qi + 1) * tq + bucket - 1) // bucket * bucket, sk)


def _forget_kernel(q_ref, k_ref, v_ref, cq_ref, ck_ref, o_ref, v16, s_scr, p_scr, *,
                   tq, q_offset, bucket, variants):
    group = pl.program_id(1)
    qi = pl.program_id(2)
    sk = k_ref.shape[1]
    ck_w = KEY_CHUNK
    ones_lane = _ones_lanes()
    heads = [(pp, j) for pp in range(q_ref.shape[2] // LANES) for j in range(HEADS_PER_VREG)]
    lanes = [slice(pp * LANES, (pp + 1) * LANES) for pp, _ in heads]

    @pl.when(qi == 0)
    def _():
        v = v_ref[0]
        for j in range(HEADS_PER_VREG):
            v16[j] = _masked_values(v, j)

    qmask = _head_masks((tq, LANES))
    q_pos = q_offset + qi * tq + lax.broadcasted_iota(jnp.int32, (tq, ck_w), 0)
    k_iota = lax.broadcasted_iota(jnp.int32, (tq, ck_w), 1)
    cq_all = cq_ref[0]
    head_lane = lax.broadcasted_iota(jnp.int32, cq_all.shape, 1)
    n_k_here = _key_extent(qi, tq=tq, q_offset=q_offset, bucket=bucket, sk=sk)

    def attend(n_k, mask_from):
        n_c = n_k // ck_w
        cols = [slice(c * ck_w, (c + 1) * ck_w) for c in range(n_c)]
        allowed = {c: c * ck_w + k_iota <= q_pos for c in range(mask_from, n_c)}
        qs = [jnp.where(qmask[j], q_ref[0, :, lanes[n]], jnp.zeros((), BF16))
              for n, (_, j) in enumerate(heads)]
        cq = [LOG2E * jnp.sum(jnp.where(head_lane == group * len(heads) + n, cq_all, 0.0),
                              axis=-1, keepdims=True) for n in range(len(heads))]

        def logits(n, c, mrun):
            s = (_dot_nt(qs[n], k_ref[0, cols[c], lanes[n]])
                 - LOG2E * ck_ref[0, 0, n:n + 1, cols[c]])
            if c in allowed:
                s = jnp.where(allowed[c], s, NEG_INF)
            s_scr[n % 2, :, cols[c]] = s
            for part in range(ck_w // LANES):
                mrun = jnp.maximum(mrun, s[:, part * LANES:(part + 1) * LANES])
            return mrun

        def row_term(n, mrun):
            m = jnp.max(mrun, axis=-1, keepdims=True) + cq[n]
            return cq[n] - m

        def probs(n, c, row):
            p_scr[n % 2, :, cols[c]] = jnp.exp2(s_scr[n % 2, :, cols[c]] + row).astype(BF16)

        def pv(n):
            return _dot(p_scr[n % 2, :, :n_k], v16[heads[n][1], :n_k, lanes[n]])

        neg = jnp.full((tq, LANES), NEG_INF, F32)
        outs = []
        mrun = functools.reduce(lambda m, c: logits(0, c, m), range(n_c), neg)
        for n in range(len(heads)):
            row = row_term(n, mrun)
            mrun = neg
            for c in range(n_c):
                if n + 1 < len(heads):
                    mrun = logits(n + 1, c, mrun)
                probs(n, c, row)
            o = pv(n)
            lane = ones_lane[heads[n][1]]
            outs.append(o / o[:, lane:lane + 1])
        for n in range(0, len(heads), HEADS_PER_VREG):
            o_ref[0, :, lanes[n]] = jnp.where(qmask[0], outs[n], outs[n + 1]).astype(BF16)

    for n_k, mask_from in variants:
        pl.when(n_k_here == n_k)(functools.partial(attend, n_k, mask_from))


def forget_attn(q, k, v, cum_q, cum_k_rows, *, tq, bucket, q_offset):
    b, sq, w = q.shape
    sk = k.shape[1]
    n_heads = cum_q.shape[2]
    assert sk % KEY_CHUNK == 0 and bucket % KEY_CHUNK == 0
    extent = lambda t: min(-(-(q_offset + (t + 1) * tq) // bucket) * bucket, sk)
    variants = []
    for n_k in sorted({extent(t) for t in range(sq // tq)}):
        first_row = min(q_offset + t * tq for t in range(sq // tq) if extent(t) == n_k)
        variants.append((n_k, first_row // KEY_CHUNK))
    wg = ATTN_PAIRS_PER_STEP * LANES
    heads_per_step = ATTN_PAIRS_PER_STEP * HEADS_PER_VREG
    ck = cum_k_rows.reshape(b, n_heads // heads_per_step, heads_per_step, sk)
    kern = functools.partial(_forget_kernel, tq=tq, q_offset=q_offset, bucket=bucket,
                             variants=tuple(variants))
    n_k_max = variants[-1][0]
    return pl.pallas_call(
        kern, grid=(b, w // wg, sq // tq),
        in_specs=[pl.BlockSpec((1, tq, wg), lambda i, h, t: (i, t, h)),
                  pl.BlockSpec((1, sk, wg), lambda i, h, t: (i, 0, h)),
                  pl.BlockSpec((1, sk, wg), lambda i, h, t: (i, 0, h)),
                  pl.BlockSpec((1, tq, n_heads), lambda i, h, t: (i, t, 0)),
                  pl.BlockSpec((1, 1, heads_per_step, sk), lambda i, h, t: (i, h, 0, 0))],
        out_specs=pl.BlockSpec((1, tq, wg), lambda i, h, t: (i, t, h)),
        out_shape=jax.ShapeDtypeStruct((b, sq, w), BF16),
        scratch_shapes=[pltpu.VMEM((HEADS_PER_VREG, sk, wg), BF16),
                        pltpu.VMEM((2, tq, n_k_max), F32),
                        pltpu.VMEM((2, tq, n_k_max), BF16)],
        compiler_params=_cparams("parallel", "parallel", "arbitrary"),
        name="forget_attn")(q, k, v, cum_q, ck)


def _layer_tail_kernel(*refs, n_in, n_heads, rows_per_seq, tf, final_norm):
    a_refs, w_refs = refs[:n_in], refs[n_in:2 * n_in]
    (x_ref, gx_ref, wq_ref, wo_ref, mk_ref, mv_ref, gf_ref, wg_ref, wu_ref, wd_ref, gfin_ref,
     o_ref) = refs[2 * n_in:]
    x = x_ref[...]
    for a_ref, w_ref in zip(a_refs, w_refs):
        x = x + _dot(a_ref[...], w_ref[...])

    hd = x.shape[1] // n_heads
    q = (_dot(_rms_bf16(x, gx_ref[...]), wq_ref[...]) * (hd ** -0.5)).astype(BF16)
    per_seq = []
    for b in range(mk_ref.shape[0]):
        rows = slice(b * rows_per_seq, (b + 1) * rows_per_seq)
        outs = []
        for j in range(n_heads):
            cols = slice(j * hd, (j + 1) * hd)
            s = _dot_nt(q[rows, cols], mk_ref[b, :, cols])
            p = jnp.exp(s - jnp.max(s, axis=-1, keepdims=True))
            p = p / jnp.sum(p, axis=-1, keepdims=True)
            outs.append(_dot(p.astype(BF16), mv_ref[b, :, cols]).astype(BF16))
        per_seq.append(jnp.concatenate(outs, axis=-1))
    x = x + _dot(jnp.concatenate(per_seq, axis=0), wo_ref[...])

    h = _rms_bf16(x, gf_ref[...])
    for c in range(wg_ref.shape[1] // tf):
        cols = slice(c * tf, (c + 1) * tf)
        gate = _dot(h, wg_ref[:, cols])
        up = _dot(h, wu_ref[:, cols])
        a = (gate * jax.nn.sigmoid(gate) * up).astype(BF16)
        x = x + _dot(a, wd_ref[cols, :])
    if final_norm:
        x = x * lax.rsqrt(jnp.mean(x * x, axis=-1, keepdims=True) + RMS_EPS) * gfin_ref[...]
    o_ref[...] = x


def layer_tail(a_list, w_list, x, g_x, wq, wo, mk, mv, g_f, wg, wu, wd, g_final, *, seq, tm, tf,
               final_norm):
    m, d = x.shape
    n_mem = mk.shape[1]
    rows_per_seq = min(tm, seq)
    seqs = tm // rows_per_seq
    assert seq % rows_per_seq == 0 and m % tm == 0
    row = lambda i: (i, 0)
    mem = lambda i: (i * tm // (seq * seqs), 0, 0)
    gain = lambda g: g.reshape(1, d)
    kern = functools.partial(_layer_tail_kernel, n_in=len(a_list), n_heads=XA_HEADS,
                             rows_per_seq=rows_per_seq, tf=tf, final_norm=final_norm)
    in_specs = ([pl.BlockSpec((tm, a.shape[1]), row) for a in a_list]
                + [_resident_spec(w.shape) for w in w_list]
                + [pl.BlockSpec((tm, d), row), _resident_spec((1, d)),
                   _resident_spec(wq.shape), _resident_spec(wo.shape),
                   pl.BlockSpec((seqs, n_mem, d), mem), pl.BlockSpec((seqs, n_mem, d), mem),
                   _resident_spec((1, d)), _resident_spec(wg.shape), _resident_spec(wu.shape),
                   _resident_spec(wd.shape), _resident_spec((1, d))])
    return pl.pallas_call(
        kern, grid=(m // tm,), in_specs=in_specs,
        out_specs=pl.BlockSpec((tm, d), row), out_shape=jax.ShapeDtypeStruct((m, d), F32),
        compiler_params=_cparams("parallel"), name="layer_tail")(
            *a_list, *w_list, x, gain(g_x), wq, wo, mk, mv, gain(g_f), wg, wu, wd, gain(g_final))


def _pad_rows(a, rows):
    return jnp.pad(a, ((0, 0), (0, rows - a.shape[1]), (0, 0)))


def _row_tile(m, seq, cap):
    tm = min(cap, seq)
    assert seq % tm == 0 and m % tm == 0
    return tm


def _layer_ab(x, seq, g, w_in, w_out, rel_bias, cache, *, tm):
    m, d = x.shape
    b = m // seq
    wa = w_in.shape[1] // 6
    ws = [w_in[:, n * wa:(n + 1) * wa].astype(BF16) for n in range(6)]
    scale = HEAD_DIM ** -0.5 * LOG2E
    kept = 'tail' if min(A_PAST, seq) < seq else 'heads'
    outs = [(0, 'bf16', scale), (1, 'bf16', 1.0), (2, 'bf16', 1.0),
            (1, kept, HEAD_DIM), (2, kept, HEAD_DIM),
            (3, 'bf16', scale), (4, 'bf16', 1.0), (5, 'bf16', 1.0),
            (4, 'heads', HEAD_DIM), (5, 'heads', HEAD_DIM)]
    qa, ka16, va16, ka_keep, va_keep, qb, kb16, vb16, kb, vb = norm_proj(x, g, ws, outs, seq=seq,
                                                                         tm=tm)
    shp = lambda a: a.reshape(b, -1, a.shape[-1])
    if cache is None:
        tq = 4 * CHUNK
        band = -(-(A_PAST + tq) // KEY_CHUNK) * KEY_CHUNK
        offsets = sorted({t * tq - max(t * (tq // CHUNK) - A_PAST_CHUNKS, 0) * CHUNK
                          for t in range(seq // tq)})
        bias = band_bias(rel_bias, offsets, rows=tq, cols=band, n_real=band)
        oa = band_attn(shp(qa), shp(ka16), shp(va16), bias, tq=tq, band=band)
        ob = stick_attn(shp(qb), shp(kb16), shp(vb16), tq=256, q_offset=0)
    else:
        ca_k, ca_v, cb_k, cb_v = cache
        n_past = ca_k.shape[1]
        n_keys = n_past + seq
        band = -(-n_keys // KEY_CHUNK) * KEY_CHUNK
        flat = lambda a: a.reshape(a.shape[0], a.shape[1], -1)
        k_all = _pad_rows(jnp.concatenate([flat(ca_k).astype(BF16), shp(ka16)], axis=1), band)
        v_all = _pad_rows(jnp.concatenate([flat(ca_v).astype(BF16), shp(va16)], axis=1), band)
        bias = band_bias(rel_bias, [n_past], rows=seq, cols=band, n_real=n_keys)
        oa = band_attn(shp(qa), k_all, v_all, bias, tq=seq, band=band)
        n_pastb = cb_k.shape[1]
        sk = -(-(n_pastb + seq) // KEY_CHUNK) * KEY_CHUNK
        kb_all = _pad_rows(jnp.concatenate([flat(cb_k).astype(BF16), shp(kb16)], axis=1), sk)
        vb_all = _pad_rows(jnp.concatenate([flat(cb_v).astype(BF16), shp(vb16)], axis=1), sk)
        ob = stick_attn(shp(qb), kb_all, vb_all, tq=seq, q_offset=n_pastb)
    w_out = w_out.astype(BF16)
    wo_a, wo_b = w_out[:oa.shape[-1]], w_out[oa.shape[-1]:]
    return ([oa.reshape(m, -1), ob.reshape(m, -1)], [wo_a, wo_b]), (ka_keep, va_keep, kb, vb)


def _layer_c(x, seq, g, w_in, b_f, w_out, cache, *, tm):
    m, d = x.shape
    b = m // seq
    n_heads = b_f.shape[0]
    wq, wk, wv, wf = (w_in[:, :d], w_in[:, d:2 * d], w_in[:, 2 * d:3 * d], w_in[:, 3 * d:])
    ws = [w.astype(BF16) for w in (wq, wk, wv, wf)]
    outs = [(0, 'bf16', HEAD_DIM ** -0.5 * LOG2E), (1, 'bf16', 1.0), (2, 'bf16', 1.0),
            (1, 'heads', HEAD_DIM), (2, 'heads', HEAD_DIM), (3, 'logf', None)]
    q, k16, v16, k, v, lf = norm_proj(x, g, ws, outs, seq=seq, tm=tm, bias=b_f.reshape(1, n_heads))
    shp = lambda a: a.reshape(b, -1, a.shape[-1])
    if cache is None:
        cum_col, cum_row = cumsum_logf(shp(lf))
        o = forget_attn(shp(q), shp(k16), shp(v16), cum_col, cum_row, tq=256, bucket=KEY_CHUNK,
                        q_offset=0)
    else:
        c_k, c_v, c_lf = cache
        n_past = c_k.shape[1]
        sk = -(-(n_past + seq) // KEY_CHUNK) * KEY_CHUNK
        flat = lambda a: a.reshape(a.shape[0], a.shape[1], -1)
        k_all = _pad_rows(jnp.concatenate([flat(c_k).astype(BF16), shp(k16)], axis=1), sk)
        v_all = _pad_rows(jnp.concatenate([flat(c_v).astype(BF16), shp(v16)], axis=1), sk)
        lf_all = _pad_rows(jnp.concatenate([c_lf, shp(lf)], axis=1), sk)
        cum_col, cum_row = cumsum_logf(lf_all)
        o = forget_attn(shp(q), k_all, v_all, cum_col[:, n_past:n_past + seq], cum_row,
                        tq=seq, bucket=KEY_CHUNK, q_offset=n_past)
    return ([o.reshape(m, d)], [w_out.astype(BF16)]), (k, v, lf)


def kernel(x_prompt, x_sample, cache_a_k, cache_a_v, cache_b_k, cache_b_v, cache_c_k, cache_c_v, cache_c_logf, cache_mem_k, cache_mem_v, mem_prompt, w_in_ab, w_out_ab, rel_bias_a, w_in_c, b_f_c, w_out_c, g_mix, g_xattn, g_mem, w_xq, w_xk, w_xv, w_xo, g_ffn, w_gate, w_up, w_down, g_final):
    bp, sp, d = x_prompt.shape
    bs, ss, _ = x_sample.shape
    depth = g_mix.shape[0]
    n_mem = mem_prompt.shape[1]
    xp = x_prompt.reshape(bp * sp, d)
    xs = x_sample.reshape(bs * ss, d)
    tmp = _row_tile(bp * sp, sp, 512)
    tms = bs * ss
    assert tms <= 512
    mem = mem_prompt.reshape(bp * n_mem, d)
    tmm = _row_tile(bp * n_mem, n_mem, 512)
    dff = w_gate.shape[2]
    tf = 256 if dff % 256 == 0 else dff

    a_kp, a_vp, b_kp, b_vp, a_ks, a_vs, b_ks, b_vs = [], [], [], [], [], [], [], []
    c_kp, c_vp, c_lfp, c_ks, c_vs, c_lfs = [], [], [], [], [], []
    mem_kp, mem_vp = [], []
    for layer in range(depth):
        if layer % 2 == 0:
            e = layer // 2
            mix_p, (ka, va, kb, vb) = _layer_ab(xp, sp, g_mix[layer], w_in_ab[e], w_out_ab[e],
                                                rel_bias_a[e], None, tm=tmp)
            a_kp.append(ka); a_vp.append(va); b_kp.append(kb); b_vp.append(vb)
            mix_s, (ka, va, kb, vb) = _layer_ab(
                xs, ss, g_mix[layer], w_in_ab[e], w_out_ab[e], rel_bias_a[e],
                (cache_a_k[e], cache_a_v[e], cache_b_k[e], cache_b_v[e]), tm=tms)
            a_ks.append(ka); a_vs.append(va); b_ks.append(kb); b_vs.append(vb)
        else:
            c = layer // 2
            mix_p, (k, v, lf) = _layer_c(xp, sp, g_mix[layer], w_in_c[c], b_f_c[c], w_out_c[c],
                                         None, tm=tmp)
            c_kp.append(k); c_vp.append(v); c_lfp.append(lf)
            mix_s, (k, v, lf) = _layer_c(xs, ss, g_mix[layer], w_in_c[c], b_f_c[c], w_out_c[c],
                                         (cache_c_k[c], cache_c_v[c], cache_c_logf[c]), tm=tms)
            c_ks.append(k); c_vs.append(v); c_lfs.append(lf)
        mk, mv, mk16, mv16 = norm_proj(
            mem, g_mem[layer], [w_xk[layer].astype(BF16), w_xv[layer].astype(BF16)],
            [(0, 'heads', d // XA_HEADS), (1, 'heads', d // XA_HEADS), (0, 'bf16', 1.0),
             (1, 'bf16', 1.0)],
            seq=n_mem, tm=tmm)
        mem_kp.append(mk); mem_vp.append(mv)
        tail = functools.partial(
            layer_tail, g_x=g_xattn[layer], wq=w_xq[layer].astype(BF16),
            wo=w_xo[layer].astype(BF16), g_f=g_ffn[layer], wg=w_gate[layer].astype(BF16),
            wu=w_up[layer].astype(BF16), wd=w_down[layer].astype(BF16), g_final=g_final, tf=tf,
            final_norm=layer == depth - 1)
        xp = tail(*mix_p, xp, mk=mk16.reshape(bp, n_mem, d), mv=mv16.reshape(bp, n_mem, d),
                  seq=sp, tm=tmp)
        xs = tail(*mix_s, xs, mk=cache_mem_k[layer].reshape(bs, n_mem, d).astype(BF16),
                  mv=cache_mem_v[layer].reshape(bs, n_mem, d).astype(BF16), seq=ss, tm=tms)

    hd = HEAD_DIM
    xa_hd = d // XA_HEADS
    r5 = lambda lst, b, s, dd: jnp.stack([a.reshape(b, s, -1, dd) for a in lst])
    r4 = lambda lst, b, s: jnp.stack([a.reshape(b, s, -1) for a in lst])
    keep = min(A_PAST, sp)
    return (xp.reshape(bp, sp, d), xs.reshape(bs, ss, d),
            r5(a_kp, bp, keep, hd), r5(a_vp, bp, keep, hd), r5(b_kp, bp, sp, hd), r5(b_vp, bp, sp, hd),
            r5(c_kp, bp, sp, hd), r5(c_vp, bp, sp, hd), r4(c_lfp, bp, sp),
            r5(mem_kp, bp, n_mem, xa_hd), r5(mem_vp, bp, n_mem, xa_hd),
            r5(a_ks, bs, ss, hd), r5(a_vs, bs, ss, hd), r5(b_ks, bs, ss, hd), r5(b_vs, bs, ss, hd),
            r5(c_ks, bs, ss, hd), r5(c_vs, bs, ss, hd), r4(c_lfs, bs, ss))
```

```python
import functools

import jax
import jax.numpy as jnp
from jax import lax
from jax.experimental import pallas as pl
from jax.experimental.pallas import tpu as pltpu

F32 = jnp.float32
BF16 = jnp.bfloat16

RMS_EPS = 1e-6
NEG_INF = -1e30
LOG2E = 1.4426950408889634
HEAD_DIM = 64
CHUNK = 64
A_PAST_CHUNKS = 8
A_PAST = A_PAST_CHUNKS * CHUNK
REL_CLIP = 128
XA_HEADS = 4

LANES = 128
HEADS_PER_VREG = LANES // HEAD_DIM
KEY_CHUNK = 256
ATTN_PAIRS_PER_STEP = 4
VMEM_LIMIT = 56 * 1024 * 1024


def _cparams(*sem):
    return pltpu.CompilerParams(dimension_semantics=sem, vmem_limit_bytes=VMEM_LIMIT)


def _resident_spec(shape):
    return pl.BlockSpec(shape, lambda *_: (0,) * len(shape), pipeline_mode=pl.Buffered(1))


def _rms_bf16(x, g):
    y = x * lax.rsqrt(jnp.mean(x * x, axis=-1, keepdims=True) + RMS_EPS)
    return (y * g).astype(BF16)


def _log_sigmoid(z):
    return jnp.minimum(z, 0.0) - jnp.log1p(jnp.exp(-jnp.abs(z)))


def _split3(x):
    hi = x.astype(BF16)
    r = x - hi.astype(F32)
    mid = r.astype(BF16)
    lo = (r - mid.astype(F32)).astype(BF16)
    return hi, mid, lo


def _split2(x):
    hi = x.astype(BF16)
    lo = (x - hi.astype(F32)).astype(BF16)
    return hi, lo


def _dot(a, b):
    return jnp.dot(a, b, preferred_element_type=F32)


def _dot_nt(a, b):
    return lax.dot_general(a, b, (((1,), (1,)), ((), ())), preferred_element_type=F32)


def _norm_proj_kernel(x_ref, g_ref, b_ref, *refs, n_w, outs, tiles_per_seq):
    w_refs, o_refs = refs[:n_w], refs[n_w:]
    h = _rms_bf16(x_ref[...], g_ref[...])
    ys = {}
    for o_ref, (grp, kind, arg) in zip(o_refs, outs):
        if grp not in ys:
            ys[grp] = _dot(h, w_refs[grp][...])
        y = ys[grp]
        if kind == 'f32':
            o_ref[...] = y
        elif kind == 'bf16':
            o_ref[...] = (y * arg).astype(BF16)
        elif kind == 'logf':
            o_ref[...] = _log_sigmoid(y + b_ref[...])
        elif kind == 'heads':
            o_ref[...] = y.reshape(o_ref.shape)
        elif kind == 'rowmax':
            hd, scale = arg
            n_cols, n_heads = y.shape[1], y.shape[1] // hd
            sel_l = lax.broadcasted_iota(jnp.int32, (n_cols, n_heads), 0) // hd
            sel_h = lax.broadcasted_iota(jnp.int32, (n_cols, n_heads), 1)
            scaled = y * scale
            norms = jnp.sqrt(_dot((scaled * scaled).astype(BF16),
                                  jnp.where(sel_l == sel_h, 1.0, 0.0).astype(BF16)))
            for r in range(o_ref.shape[1]):
                o_ref[0, r:r + 1, :] = jnp.max(norms[r * KEY_CHUNK:(r + 1) * KEY_CHUNK],
                                               axis=0, keepdims=True)
        else:
            @pl.when(pl.program_id(0) % tiles_per_seq == tiles_per_seq - 1)
            def _(o_ref=o_ref, y=y):
                o_ref[...] = y.reshape(o_ref.shape)


def norm_proj(x, g, ws, outs, *, seq, tm, bias=None):
    m, d = x.shape
    tiles_per_seq = max(seq // tm, 1)
    if bias is None:
        bias = jnp.zeros((1, 16), F32)
    in_specs = [pl.BlockSpec((tm, d), lambda i: (i, 0)),
                pl.BlockSpec((1, d), lambda i: (0, 0)),
                pl.BlockSpec(bias.shape, lambda i: (0, 0))]
    in_specs += [_resident_spec(w.shape) for w in ws]
    out_shape, out_specs = [], []
    for grp, kind, arg in outs:
        n = ws[grp].shape[1]
        if kind == 'tail':
            assert tm == min(A_PAST, seq)
            out_shape.append(jax.ShapeDtypeStruct((m // tiles_per_seq, n // arg, arg), F32))
            out_specs.append(pl.BlockSpec((tm, n // arg, arg), lambda i: (i // tiles_per_seq, 0, 0)))
        elif kind == 'heads':
            out_shape.append(jax.ShapeDtypeStruct((m, n // arg, arg), F32))
            out_specs.append(pl.BlockSpec((tm, n // arg, arg), lambda i: (i, 0, 0)))
        elif kind == 'rowmax':
            assert tm % KEY_CHUNK == 0
            blocks = (tm // KEY_CHUNK, n // arg[0])
            out_shape.append(jax.ShapeDtypeStruct((m // tm,) + blocks, F32))
            out_specs.append(pl.BlockSpec((1,) + blocks, lambda i: (i, 0, 0)))
        else:
            out_shape.append(jax.ShapeDtypeStruct((m, n), BF16 if kind == 'bf16' else F32))
            out_specs.append(pl.BlockSpec((tm, n), lambda i: (i, 0)))
    kern = functools.partial(_norm_proj_kernel, n_w=len(ws), outs=tuple(outs),
                             tiles_per_seq=tiles_per_seq)
    return pl.pallas_call(
        kern, grid=(m // tm,), in_specs=in_specs, out_specs=out_specs, out_shape=out_shape,
        compiler_params=_cparams("arbitrary"), name="norm_proj")(x, g.reshape(1, d), bias, *ws)


BIAS_BLOCK_ROWS = 32


def _band_bias_kernel(rb_ref, off_ref, o_ref, *, n_real):
    n_heads, rows, cols = o_ref.shape[1:]
    off = off_ref[pl.program_id(0)]
    i0 = pl.program_id(1) * rows
    r0 = pl.program_id(2) * cols
    i = lax.broadcasted_iota(jnp.int32, (rows, cols), 0) + i0
    r = lax.broadcasted_iota(jnp.int32, (rows, cols), 1) + r0
    d = jnp.clip(off + i - r, -REL_CLIP, REL_CLIP) + REL_CLIP
    lo = jnp.clip(off + i0 - (r0 + cols - 1), -REL_CLIP, REL_CLIP) + REL_CLIP
    hi = jnp.clip(off + i0 + rows - 1 - r0, -REL_CLIP, REL_CLIP) + REL_CLIP

    def body(u, tbls):
        hit = d == u
        return tuple(jnp.where(hit, rb_ref[h, u], t) for h, t in enumerate(tbls))

    tbls = lax.fori_loop(lo, hi + 1, body,
                         tuple(jnp.zeros((rows, cols), F32) for _ in range(n_heads)))
    first = (i // CHUNK - A_PAST_CHUNKS) * CHUNK + off
    last = jnp.minimum((i // CHUNK + 1) * CHUNK + off, n_real)
    visible = (r >= first) & (r < last)
    for h in range(n_heads):
        o_ref[0, h] = jnp.where(visible, LOG2E * tbls[h], NEG_INF)


def band_bias(rel_bias, offsets, *, rows, cols, n_real):
    n_rel, n_heads = rel_bias.shape
    blk = min(rows, BIAS_BLOCK_ROWS)
    return pl.pallas_call(
        functools.partial(_band_bias_kernel, n_real=n_real),
        grid=(len(offsets), rows // blk, cols // LANES),
        in_specs=[pl.BlockSpec(memory_space=pltpu.SMEM), pl.BlockSpec(memory_space=pltpu.SMEM)],
        out_specs=pl.BlockSpec((1, n_heads, blk, LANES), lambda o, t, c: (o, 0, t, c)),
        out_shape=jax.ShapeDtypeStruct((len(offsets), n_heads, rows, cols), F32),
        compiler_params=_cparams("arbitrary", "arbitrary", "arbitrary"),
        name="band_bias")(rel_bias.T, jnp.asarray(offsets, jnp.int32))


def _head_masks(shape):
    lane = lax.broadcasted_iota(jnp.int32, shape, len(shape) - 1)
    return [(lane % LANES) // HEAD_DIM == j for j in range(HEADS_PER_VREG)]


def _ones_lanes():
    return [((j + 1) % HEADS_PER_VREG) * HEAD_DIM for j in range(HEADS_PER_VREG)]


def _masked_values(v, j):
    lane = lax.broadcasted_iota(jnp.int32, v.shape, 1) % LANES
    vj = jnp.where(lane // HEAD_DIM == j, v, jnp.zeros((), BF16))
    return jnp.where(lane == _ones_lanes()[j], jnp.ones((), BF16), vj)


def _band_attn_kernel(q_ref, k_ref, v_ref, bias_ref, o_ref, v16, s_scr, p_scr, *, tq, band):
    step = pl.program_id(1)
    ck = KEY_CHUNK

    @pl.when(step == 0)
    def _():
        v = v_ref[0]
        for j in range(HEADS_PER_VREG):
            v16[j] = _masked_values(v, j)

    start = pl.multiple_of(jnp.maximum(step * (tq // CHUNK) - A_PAST_CHUNKS, 0) * CHUNK, CHUNK)
    qmask = _head_masks((tq, LANES))
    ones_lane = _ones_lanes()
    cols = [slice(c * ck, (c + 1) * ck) for c in range(band // ck)]
    heads = [(hp, j) for hp in range(q_ref.shape[2] // LANES) for j in range(HEADS_PER_VREG)]
    lanes = [slice(hp * LANES, (hp + 1) * LANES) for hp, _ in heads]
    qs = [jnp.where(qmask[j], q_ref[0, :, lanes[n]], jnp.zeros((), BF16))
          for n, (_, j) in enumerate(heads)]

    def logits(n, c, mrun):
        k = k_ref[0, pl.ds(pl.multiple_of(start + c * ck, CHUNK), ck), lanes[n]]
        s = _dot_nt(qs[n], k) + bias_ref[0, n, :, cols[c]]
        s_scr[n % 2, :, cols[c]] = s
        for part in range(ck // LANES):
            mrun = jnp.maximum(mrun, s[:, part * LANES:(part + 1) * LANES])
        return mrun

    def probs(n, c, row_max):
        p_scr[n % 2, :, cols[c]] = jnp.exp2(s_scr[n % 2, :, cols[c]] - row_max).astype(BF16)

    def pv(n):
        return _dot(p_scr[n % 2], v16[heads[n][1], pl.ds(start, band), lanes[n]])

    neg = jnp.full((tq, LANES), NEG_INF, F32)
    outs = []
    mrun = functools.reduce(lambda m, c: logits(0, c, m), range(len(cols)), neg)
    for n in range(len(heads)):
        row_max = jnp.max(mrun, axis=-1, keepdims=True)
        mrun = neg
        for c in range(len(cols)):
            if n + 1 < len(heads):
                mrun = logits(n + 1, c, mrun)
            probs(n, c, row_max)
        o = pv(n)
        lane = ones_lane[heads[n][1]]
        outs.append(o / o[:, lane:lane + 1])
    for n in range(0, len(heads), HEADS_PER_VREG):
        o_ref[0, :, lanes[n]] = jnp.where(qmask[0], outs[n], outs[n + 1]).astype(BF16)


def band_attn(q, k, v, bias, *, tq, band):
    b, sq, w = q.shape
    sk = k.shape[1]
    n_off, n_heads = bias.shape[:2]
    assert band % KEY_CHUNK == 0 and tq % CHUNK == 0 or sq == tq
    kern = functools.partial(_band_attn_kernel, tq=tq, band=band)
    return pl.pallas_call(
        kern, grid=(b, sq // tq),
        in_specs=[pl.BlockSpec((1, tq, w), lambda i, c: (i, c, 0)),
                  pl.BlockSpec((1, sk, w), lambda i, c: (i, 0, 0)),
                  pl.BlockSpec((1, sk, w), lambda i, c: (i, 0, 0)),
                  pl.BlockSpec((1, n_heads, tq, band),
                               lambda i, c: (jnp.minimum(c, n_off - 1), 0, 0, 0))],
        out_specs=pl.BlockSpec((1, tq, w), lambda i, c: (i, c, 0)),
        out_shape=jax.ShapeDtypeStruct((b, sq, w), BF16),
        scratch_shapes=[pltpu.VMEM((HEADS_PER_VREG, sk, w), BF16),
                        pltpu.VMEM((2, tq, band), F32),
                        pltpu.VMEM((2, tq, band), BF16)],
        compiler_params=_cparams("parallel", "arbitrary"), name="band_attn")(q, k, v, bias)


STICK_UNDERFLOW_LOG2 = 160.0
STICK_NEAR_CHUNKS = 2


def _stick_kernel(q_ref, k_ref, v_ref, o_ref, v16, zl_scr, hl_scr, w_scr, *,
                  tq, q_offset, variants):
    qi = pl.program_id(2)
    ck = KEY_CHUNK
    heads = [(pp, j) for pp in range(q_ref.shape[2] // LANES) for j in range(HEADS_PER_VREG)]
    n_heads = len(heads)
    lanes = [slice(pp * LANES, (pp + 1) * LANES) for pp, _ in heads]

    @pl.when(qi == 0)
    def _():
        v = v_ref[0]
        vmask = _head_masks(v.shape)
        for j in range(HEADS_PER_VREG):
            v16[j] = jnp.where(vmask[j], v, jnp.zeros((), BF16))

    qmask = _head_masks((tq, LANES))
    qs = [jnp.where(qmask[j], q_ref[0, :, lanes[n]], jnp.zeros((), BF16))
          for n, (_, j) in enumerate(heads)]
    q_pos = q_offset + qi * tq + lax.broadcasted_iota(jnp.int32, (tq, ck), 0)
    k_iota = lax.broadcasted_iota(jnp.int32, (tq, ck), 1)
    rr = lax.broadcasted_iota(jnp.int32, (ck, ck), 0)
    cc = lax.broadcasted_iota(jnp.int32, (ck, ck), 1)
    tri = jnp.where(rr > cc, 1.0, 0.0).astype(BF16)
    n_chunks = (q_offset + (qi + 1) * tq + ck - 1) // ck

    def logits(j, slot, c, masked):
        k0 = pl.multiple_of(c * ck, ck)
        z = _dot_nt(qs[j], k_ref[0, pl.ds(k0, ck), lanes[j]])
        sp = jnp.maximum(z, 0.0) + jnp.log2(1.0 + jnp.exp2(-jnp.abs(z)))
        if masked:
            sp = jnp.where(k0 + k_iota < q_pos, sp, 0.0)
        zl_scr[j, :, slot] = z - sp
        hi, lo = _split2(sp)
        hl_scr[0, j, :, slot] = hi
        hl_scr[1, j, :, slot] = lo
        return jnp.sum(sp, axis=-1, keepdims=True)

    def weights(j, slot, c, later, masked):
        sums = _dot(hl_scr[0, j, :, slot], tri) + _dot(hl_scr[1, j, :, slot], tri)
        w = jnp.exp2(zl_scr[j, :, slot] - sums - later)
        if masked:
            w = jnp.where(c * ck + k_iota < q_pos, w, 0.0)
        w_scr[j, :, slot] = w.astype(BF16)

    def more(laters):
        return (jnp.min(functools.reduce(jnp.minimum, laters))
                < STICK_UNDERFLOW_LOG2).astype(jnp.int32)

    def attend(near, n_masked):
        slots = [slice((near - 1 - i) * ck, (near - i) * ck) for i in range(near)]
        chunk = [n_chunks - 1 - i for i in range(near)]
        k0 = pl.multiple_of((n_chunks - near) * ck, ck)

        def pv(n, k0, width):
            return _dot(w_scr[n, :, :width], v16[heads[n][1], pl.ds(k0, width), lanes[n]])

        laters, accs = [], []
        row_sums = [logits(0, slots[i], chunk[i], i < n_masked) for i in range(near)]
        for n in range(n_heads):
            later = jnp.zeros((tq, 1), F32)
            next_sums = []
            for i in range(near):
                if n + 1 < n_heads:
                    next_sums.append(logits(n + 1, slots[i], chunk[i], i < n_masked))
                weights(n, slots[i], chunk[i], later, i < n_masked)
                later = later + row_sums[i]
            laters.append(later)
            row_sums = next_sums
            o = pv(n, k0, near * ck)
            if heads[n][1] == 0:
                accs.append(o)
            else:
                accs[-1] = accs[-1] + o

        def cond(carry):
            return (carry[0] >= 0) & (carry[1] > 0)

        def body(carry):
            c, _, accs, laters = carry
            accs = list(accs)
            k0 = pl.multiple_of(c * ck, ck)
            new = []
            for n in range(n_heads):
                rs = logits(n, slice(0, ck), c, False)
                weights(n, slice(0, ck), c, laters[n], False)
                accs[heads[n][0]] = accs[heads[n][0]] + pv(n, k0, ck)
                new.append(laters[n] + rs)
            return c - 1, more(new), tuple(accs), tuple(new)

        carry = lax.while_loop(cond, body,
                               (n_chunks - 1 - near, more(laters), tuple(accs), tuple(laters)))
        for pp, acc in enumerate(carry[2]):
            o_ref[0, :, pp * LANES:(pp + 1) * LANES] = acc.astype(BF16)

    for n_total, near, n_masked in variants:
        if n_total is None:
            pl.when(n_chunks >= near)(functools.partial(attend, near, n_masked))
        else:
            pl.when(n_chunks == n_total)(functools.partial(attend, near, n_masked))


def stick_attn(q, k, v, *, tq, q_offset):
    b, sq, w = q.shape
    sk = k.shape[1]
    ck = KEY_CHUNK
    assert q_offset % ck == 0 and (tq % ck == 0 or sq == tq <= ck) and sk % ck == 0
    n_masked = -(-tq // ck)
    totals = sorted({-(-(q_offset + (t + 1) * tq) // ck) for t in range(sq // tq)})
    variants = [(n, n, min(n_masked, n)) for n in totals if n < STICK_NEAR_CHUNKS]
    if totals[-1] >= STICK_NEAR_CHUNKS:
        variants.append((None, STICK_NEAR_CHUNKS, n_masked))
    near_cols = STICK_NEAR_CHUNKS * ck
    wg = ATTN_PAIRS_PER_STEP * LANES
    n_heads = ATTN_PAIRS_PER_STEP * HEADS_PER_VREG
    kern = functools.partial(_stick_kernel, tq=tq, q_offset=q_offset, variants=tuple(variants))
    return pl.pallas_call(
        kern, grid=(b, w // wg, sq // tq),
        in_specs=[pl.BlockSpec((1, tq, wg), lambda i, h, t: (i, t, h)),
                  pl.BlockSpec((1, sk, wg), lambda i, h, t: (i, 0, h)),
                  pl.BlockSpec((1, sk, wg), lambda i, h, t: (i, 0, h))],
        out_specs=pl.BlockSpec((1, tq, wg), lambda i, h, t: (i, t, h)),
        out_shape=jax.ShapeDtypeStruct((b, sq, w), BF16),
        scratch_shapes=[pltpu.VMEM((HEADS_PER_VREG, sk, wg), BF16),
                        pltpu.VMEM((n_heads, tq, near_cols), F32),
                        pltpu.VMEM((2, n_heads, tq, near_cols), BF16),
                        pltpu.VMEM((n_heads, tq, near_cols), BF16)],
        compiler_params=_cparams("parallel", "parallel", "arbitrary"), name="stick_attn")(q, k, v)


def _cumsum_kernel(lf_ref, lft_ref, col_ref, row_ref, *, blk):
    s_len, n_heads = lf_ref.shape[1], lf_ref.shape[2]
    rr = lax.broadcasted_iota(jnp.int32, (blk, blk), 0)
    cc = lax.broadcasted_iota(jnp.int32, (blk, blk), 1)
    lower = jnp.where(rr >= cc, 1.0, 0.0).astype(BF16)
    upper = jnp.where(rr <= cc, 1.0, 0.0).astype(BF16)
    carry_col = jnp.zeros((1, n_heads), F32)
    carry_row = jnp.zeros((n_heads, 1), F32)
    for n in range(s_len // blk):
        rows = slice(n * blk, (n + 1) * blk)
        c = sum(_dot(lower, p) for p in _split3(lf_ref[0, rows, :])) + carry_col
        col_ref[0, rows, :] = c
        carry_col = c[blk - 1:blk, :]
        ct = sum(_dot(p, upper) for p in _split3(lft_ref[0, :, rows])) + carry_row
        row_ref[0, :, rows] = ct
        carry_row = ct[:, blk - 1:blk]


def cumsum_logf(lf):
    b, s_len, n_heads = lf.shape
    kern = functools.partial(_cumsum_kernel, blk=LANES)
    return pl.pallas_call(
        kern, grid=(b,),
        in_specs=[pl.BlockSpec((1, s_len, n_heads), lambda i: (i, 0, 0)),
                  pl.BlockSpec((1, n_heads, s_len), lambda i: (i, 0, 0))],
        out_specs=[pl.BlockSpec((1, s_len, n_heads), lambda i: (i, 0, 0)),
                   pl.BlockSpec((1, n_heads, s_len), lambda i: (i, 0, 0))],
        out_shape=[jax.ShapeDtypeStruct((b, s_len, n_heads), F32),
                   jax.ShapeDtypeStruct((b, n_heads, s_len), F32)],
        compiler_params=_cparams("parallel"), name="cumsum_logf")(lf, jnp.swapaxes(lf, 1, 2))


FORGET_SKIP_LOG2 = 152.0
FORGET_BOUND_SLACK = 1.02
FORGET_BOUND_MARGIN = 2.0


def _forget_skip_kernel(qmax_ref, kmax_ref, cum_ref, o_ref, *, heads_per_group):
    n_blk, n_heads = kmax_ref.shape[1], kmax_ref.shape[2]
    blk = cum_ref.shape[1] // n_blk
    kmax = kmax_ref[0]
    k_term = kmax + jnp.max(kmax, axis=0, keepdims=True)
    cum_end = cum_ref[0, pl.ds(blk - 1, n_blk, stride=blk), :]
    cum_start = cum_ref[0, pl.ds(0, n_blk, stride=blk), :]
    chunk_id = lax.broadcasted_iota(jnp.int32, (n_blk, 1), 0)
    head_id = lax.broadcasted_iota(jnp.int32, (n_blk, n_heads), 1)
    row = lax.broadcasted_iota(jnp.int32, o_ref.shape[1:], 0)
    col = lax.broadcasted_iota(jnp.int32, o_ref.shape[1:], 1)
    table = jnp.zeros(o_ref.shape[1:], F32)
    for qi in range(n_blk):
        bound = (FORGET_BOUND_SLACK * qmax_ref[0, qi:qi + 1, :] * k_term
                 + LOG2E * (cum_start[qi:qi + 1, :] - cum_end) + FORGET_BOUND_MARGIN)
        for g in range(n_heads // heads_per_group):
            worst = jnp.max(jnp.where(head_id // heads_per_group == g, bound, NEG_INF),
                            axis=1, keepdims=True)
            needed = (worst >= -FORGET_SKIP_LOG2) | (chunk_id >= qi)
            first = jnp.min(jnp.where(needed, chunk_id, n_blk).astype(F32), axis=0, keepdims=True)
            table = jnp.where((row == qi) & (col == g), first, table)
    o_ref[0] = table.astype(jnp.int32)


def forget_skip_table(qmax, kmax, cum, *, heads_per_group):
    b, n_blk, n_heads = kmax.shape
    s_len = cum.shape[1]
    blk3 = lambda shape: pl.BlockSpec((1,) + shape, lambda i: (i, 0, 0))
    return pl.pallas_call(
        functools.partial(_forget_skip_kernel, heads_per_group=heads_per_group), grid=(b,),
        in_specs=[blk3((n_blk, n_heads)), blk3((n_blk, n_heads)), blk3((s_len, n_heads))],
        out_specs=blk3((n_blk, LANES)),
        out_shape=jax.ShapeDtypeStruct((b, n_blk, LANES), jnp.int32),
        compiler_params=_cparams("parallel"), name="forget_skip_table")(qmax, kmax, cum)


def _forget_kernel(c0_ref, q_ref, k_ref, v_ref, cq_ref, ck_ref, o_ref, v16, s_scr, p_scr, *,
                   tq, q_offset, max_chunks):
    group = pl.program_id(1)
    qi = pl.program_id(2)
    wg = k_ref.shape[2]
    ck_w = KEY_CHUNK
    ones_lane = _ones_lanes()
    heads = [(pp, j) for pp in range(wg // LANES) for j in range(HEADS_PER_VREG)]
    lanes = [slice(pp * LANES, (pp + 1) * LANES) for pp, _ in heads]

    @pl.when(qi == 0)
    def _():
        v = v_ref[0]
        for j in range(HEADS_PER_VREG):
            v16[j] = _masked_values(v, j)

    qmask = _head_masks((tq, LANES))
    q_pos = q_offset + qi * tq + lax.broadcasted_iota(jnp.int32, (tq, ck_w), 0)
    k_iota = lax.broadcasted_iota(jnp.int32, (tq, ck_w), 1)
    cq_all = cq_ref[0]
    head_lane = lax.broadcasted_iota(jnp.int32, cq_all.shape, 1)
    n_c = (q_offset + (qi + 1) * tq + ck_w - 1) // ck_w
    step = (pl.program_id(0) * pl.num_programs(1) + group) * pl.num_programs(2) + qi
    c0 = jnp.minimum(c0_ref[step], n_c - 1)
    n_proc_here = n_c - c0

    def attend(n_proc):
        cols = [slice(c * ck_w, (c + 1) * ck_w) for c in range(n_proc)]
        k0 = [pl.multiple_of((c0 + c) * ck_w, ck_w) for c in range(n_proc)]
        allowed = k0[-1] + k_iota <= q_pos
        qs = [jnp.where(qmask[j], q_ref[0, :, lanes[n]], jnp.zeros((), BF16))
              for n, (_, j) in enumerate(heads)]
        cq = [LOG2E * jnp.sum(jnp.where(head_lane == group * len(heads) + n, cq_all, 0.0),
                              axis=-1, keepdims=True) for n in range(len(heads))]

        def logits(n, c, mrun):
            s = (_dot_nt(qs[n], k_ref[0, pl.ds(k0[c], ck_w), lanes[n]])
                 - LOG2E * ck_ref[0, 0, n:n + 1, pl.ds(k0[c], ck_w)])
            if c == n_proc - 1:
                s = jnp.where(allowed, s, NEG_INF)
            s_scr[n % 2, :, cols[c]] = s
            for part in range(ck_w // LANES):
                mrun = jnp.maximum(mrun, s[:, part * LANES:(part + 1) * LANES])
            return mrun

        def row_term(n, mrun):
            m = jnp.max(mrun, axis=-1, keepdims=True) + cq[n]
            return cq[n] - m

        def probs(n, c, row):
            p_scr[n % 2, :, cols[c]] = jnp.exp2(s_scr[n % 2, :, cols[c]] + row).astype(BF16)

        def pv(n):
            return _dot(p_scr[n % 2, :, :n_proc * ck_w],
                        v16[heads[n][1], pl.ds(k0[0], n_proc * ck_w), lanes[n]])

        neg = jnp.full((tq, LANES), NEG_INF, F32)
        outs = []
        mrun = functools.reduce(lambda m, c: logits(0, c, m), range(n_proc), neg)
        for n in range(len(heads)):
            row = row_term(n, mrun)
            mrun = neg
            for c in range(n_proc):
                if n + 1 < len(heads):
                    mrun = logits(n + 1, c, mrun)
                probs(n, c, row)
            o = pv(n)
            lane = ones_lane[heads[n][1]]
            outs.append(o / o[:, lane:lane + 1])
        for n in range(0, len(heads), HEADS_PER_VREG):
            o_ref[0, :, lanes[n]] = jnp.where(qmask[0], outs[n], outs[n + 1]).astype(BF16)

    for n_proc in range(1, max_chunks + 1):
        pl.when(n_proc_here == n_proc)(functools.partial(attend, n_proc))


def forget_attn(q, k, v, cum_q, cum_k_rows, skip, *, tq, q_offset):
    b, sq, w = q.shape
    sk = k.shape[1]
    n_heads = cum_q.shape[2]
    ck_w = KEY_CHUNK
    diag = [-(-(q_offset + (t + 1) * tq) // ck_w) - 1 for t in range(sq // tq)]
    assert sk % ck_w == 0 and all((q_offset + t * tq) // ck_w == c for t, c in enumerate(diag))
    max_chunks = max(diag) + 1
    wg = ATTN_PAIRS_PER_STEP * LANES
    heads_per_step = ATTN_PAIRS_PER_STEP * HEADS_PER_VREG
    assert skip.shape == (b, w // wg, sq // tq)
    ck = cum_k_rows.reshape(b, n_heads // heads_per_step, heads_per_step, sk)
    kern = functools.partial(_forget_kernel, tq=tq, q_offset=q_offset, max_chunks=max_chunks)
    grid_spec = pltpu.PrefetchScalarGridSpec(
        num_scalar_prefetch=1, grid=(b, w // wg, sq // tq),
        in_specs=[pl.BlockSpec((1, tq, wg), lambda i, h, t, c0: (i, t, h)),
                  pl.BlockSpec((1, sk, wg), lambda i, h, t, c0: (i, 0, h)),
                  pl.BlockSpec((1, sk, wg), lambda i, h, t, c0: (i, 0, h)),
                  pl.BlockSpec((1, tq, n_heads), lambda i, h, t, c0: (i, t, 0)),
                  pl.BlockSpec((1, 1, heads_per_step, sk), lambda i, h, t, c0: (i, h, 0, 0))],
        out_specs=pl.BlockSpec((1, tq, wg), lambda i, h, t, c0: (i, t, h)),
        scratch_shapes=[pltpu.VMEM((HEADS_PER_VREG, sk, wg), BF16),
                        pltpu.VMEM((2, tq, max_chunks * ck_w), F32),
                        pltpu.VMEM((2, tq, max_chunks * ck_w), BF16)])
    return pl.pallas_call(
        kern, grid_spec=grid_spec, out_shape=jax.ShapeDtypeStruct((b, sq, w), BF16),
        compiler_params=_cparams("parallel", "parallel", "arbitrary"),
        name="forget_attn")(skip.reshape(-1), q, k, v, cum_q, ck)


def _layer_tail_kernel(*refs, n_in, n_heads, rows_per_seq, tf, final_norm):
    a_refs, w_refs = refs[:n_in], refs[n_in:2 * n_in]
    (x_ref, gx_ref, wq_ref, wo_ref, mk_ref, mv_ref, gf_ref, wg_ref, wu_ref, wd_ref, gfin_ref,
     o_ref) = refs[2 * n_in:]
    x = x_ref[...]
    for a_ref, w_ref in zip(a_refs, w_refs):
        x = x + _dot(a_ref[...], w_ref[...])

    hd = x.shape[1] // n_heads
    q = (_dot(_rms_bf16(x, gx_ref[...]), wq_ref[...]) * (hd ** -0.5)).astype(BF16)
    per_seq = []
    for b in range(mk_ref.shape[0]):
        rows = slice(b * rows_per_seq, (b + 1) * rows_per_seq)
        outs = []
        for j in range(n_heads):
            cols = slice(j * hd, (j + 1) * hd)
            s = _dot_nt(q[rows, cols], mk_ref[b, :, cols])
            p = jnp.exp(s - jnp.max(s, axis=-1, keepdims=True))
            p = p / jnp.sum(p, axis=-1, keepdims=True)
            outs.append(_dot(p.astype(BF16), mv_ref[b, :, cols]).astype(BF16))
        per_seq.append(jnp.concatenate(outs, axis=-1))
    x = x + _dot(jnp.concatenate(per_seq, axis=0), wo_ref[...])

    h = _rms_bf16(x, gf_ref[...])
    for c in range(wg_ref.shape[1] // tf):
        cols = slice(c * tf, (c + 1) * tf)
        gate = _dot(h, wg_ref[:, cols])
        up = _dot(h, wu_ref[:, cols])
        a = (gate * jax.nn.sigmoid(gate) * up).astype(BF16)
        x = x + _dot(a, wd_ref[cols, :])
    if final_norm:
        x = x * lax.rsqrt(jnp.mean(x * x, axis=-1, keepdims=True) + RMS_EPS) * gfin_ref[...]
    o_ref[...] = x


def layer_tail(a_list, w_list, x, g_x, wq, wo, mk, mv, g_f, wg, wu, wd, g_final, *, seq, tm, tf,
               final_norm):
    m, d = x.shape
    n_mem = mk.shape[1]
    rows_per_seq = min(tm, seq)
    seqs = tm // rows_per_seq
    assert seq % rows_per_seq == 0 and m % tm == 0
    row = lambda i: (i, 0)
    mem = lambda i: (i * tm // (seq * seqs), 0, 0)
    gain = lambda g: g.reshape(1, d)
    kern = functools.partial(_layer_tail_kernel, n_in=len(a_list), n_heads=XA_HEADS,
                             rows_per_seq=rows_per_seq, tf=tf, final_norm=final_norm)
    in_specs = ([pl.BlockSpec((tm, a.shape[1]), row) for a in a_list]
                + [_resident_spec(w.shape) for w in w_list]
                + [pl.BlockSpec((tm, d), row), _resident_spec((1, d)),
                   _resident_spec(wq.shape), _resident_spec(wo.shape),
                   pl.BlockSpec((seqs, n_mem, d), mem), pl.BlockSpec((seqs, n_mem, d), mem),
                   _resident_spec((1, d)), _resident_spec(wg.shape), _resident_spec(wu.shape),
                   _resident_spec(wd.shape), _resident_spec((1, d))])
    return pl.pallas_call(
        kern, grid=(m // tm,), in_specs=in_specs,
        out_specs=pl.BlockSpec((tm, d), row), out_shape=jax.ShapeDtypeStruct((m, d), F32),
        compiler_params=_cparams("parallel"), name="layer_tail")(
            *a_list, *w_list, x, gain(g_x), wq, wo, mk, mv, gain(g_f), wg, wu, wd, gain(g_final))


def _pad_rows(a, rows):
    return jnp.pad(a, ((0, 0), (0, rows - a.shape[1]), (0, 0)))


def _row_tile(m, seq, cap):
    tm = min(cap, seq)
    assert seq % tm == 0 and m % tm == 0
    return tm


def _layer_ab(x, seq, g, w_in, w_out, rel_bias, cache, *, tm):
    m, d = x.shape
    b = m // seq
    wa = w_in.shape[1] // 6
    ws = [w_in[:, n * wa:(n + 1) * wa].astype(BF16) for n in range(6)]
    scale = HEAD_DIM ** -0.5 * LOG2E
    kept = 'tail' if min(A_PAST, seq) < seq else 'heads'
    outs = [(0, 'bf16', scale), (1, 'bf16', 1.0), (2, 'bf16', 1.0),
            (1, kept, HEAD_DIM), (2, kept, HEAD_DIM),
            (3, 'bf16', scale), (4, 'bf16', 1.0), (5, 'bf16', 1.0),
            (4, 'heads', HEAD_DIM), (5, 'heads', HEAD_DIM)]
    qa, ka16, va16, ka_keep, va_keep, qb, kb16, vb16, kb, vb = norm_proj(x, g, ws, outs, seq=seq,
                                                                         tm=tm)
    shp = lambda a: a.reshape(b, -1, a.shape[-1])
    if cache is None:
        tq = 4 * CHUNK
        band = -(-(A_PAST + tq) // KEY_CHUNK) * KEY_CHUNK
        offsets = sorted({t * tq - max(t * (tq // CHUNK) - A_PAST_CHUNKS, 0) * CHUNK
                          for t in range(seq // tq)})
        bias = band_bias(rel_bias, offsets, rows=tq, cols=band, n_real=band)
        oa = band_attn(shp(qa), shp(ka16), shp(va16), bias, tq=tq, band=band)
        ob = stick_attn(shp(qb), shp(kb16), shp(vb16), tq=256, q_offset=0)
    else:
        ca_k, ca_v, cb_k, cb_v = cache
        n_past = ca_k.shape[1]
        n_keys = n_past + seq
        band = -(-n_keys // KEY_CHUNK) * KEY_CHUNK
        flat = lambda a: a.reshape(a.shape[0], a.shape[1], -1)
        k_all = _pad_rows(jnp.concatenate([flat(ca_k).astype(BF16), shp(ka16)], axis=1), band)
        v_all = _pad_rows(jnp.concatenate([flat(ca_v).astype(BF16), shp(va16)], axis=1), band)
        bias = band_bias(rel_bias, [n_past], rows=seq, cols=band, n_real=n_keys)
        oa = band_attn(shp(qa), k_all, v_all, bias, tq=seq, band=band)
        n_pastb = cb_k.shape[1]
        sk = -(-(n_pastb + seq) // KEY_CHUNK) * KEY_CHUNK
        kb_all = _pad_rows(jnp.concatenate([flat(cb_k).astype(BF16), shp(kb16)], axis=1), sk)
        vb_all = _pad_rows(jnp.concatenate([flat(cb_v).astype(BF16), shp(vb16)], axis=1), sk)
        ob = stick_attn(shp(qb), kb_all, vb_all, tq=seq, q_offset=n_pastb)
    w_out = w_out.astype(BF16)
    wo_a, wo_b = w_out[:oa.shape[-1]], w_out[oa.shape[-1]:]
    return ([oa.reshape(m, -1), ob.reshape(m, -1)], [wo_a, wo_b]), (ka_keep, va_keep, kb, vb)


def _layer_c(x, seq, g, w_in, b_f, w_out, cache, *, tm):
    m, d = x.shape
    b = m // seq
    n_heads = b_f.shape[0]
    wq, wk, wv, wf = (w_in[:, :d], w_in[:, d:2 * d], w_in[:, 2 * d:3 * d], w_in[:, 3 * d:])
    ws = [w.astype(BF16) for w in (wq, wk, wv, wf)]
    scale = HEAD_DIM ** -0.5 * LOG2E
    outs = [(0, 'bf16', scale), (1, 'bf16', 1.0), (2, 'bf16', 1.0),
            (1, 'heads', HEAD_DIM), (2, 'heads', HEAD_DIM), (3, 'logf', None)]
    if cache is None:
        outs += [(0, 'rowmax', (HEAD_DIM, scale)), (1, 'rowmax', (HEAD_DIM, 1.0))]
    q, k16, v16, k, v, lf, *norms = norm_proj(x, g, ws, outs, seq=seq, tm=tm,
                                              bias=b_f.reshape(1, n_heads))
    shp = lambda a: a.reshape(b, -1, a.shape[-1])
    heads_per_group = ATTN_PAIRS_PER_STEP * HEADS_PER_VREG
    n_groups = n_heads // heads_per_group
    if cache is None:
        cum_col, cum_row = cumsum_logf(shp(lf))
        tq = KEY_CHUNK
        qmax, kmax = (a.reshape(b, seq // KEY_CHUNK, n_heads) for a in norms)
        skip = forget_skip_table(qmax, kmax, cum_col, heads_per_group=heads_per_group)
        skip = jnp.swapaxes(skip[:, :, :n_groups], 1, 2)
        o = forget_attn(shp(q), shp(k16), shp(v16), cum_col, cum_row, skip, tq=tq, q_offset=0)
    else:
        c_k, c_v, c_lf = cache
        n_past = c_k.shape[1]
        sk = -(-(n_past + seq) // KEY_CHUNK) * KEY_CHUNK
        flat = lambda a: a.reshape(a.shape[0], a.shape[1], -1)
        k_all = _pad_rows(jnp.concatenate([flat(c_k).astype(BF16), shp(k16)], axis=1), sk)
        v_all = _pad_rows(jnp.concatenate([flat(c_v).astype(BF16), shp(v16)], axis=1), sk)
        lf_all = _pad_rows(jnp.concatenate([c_lf, shp(lf)], axis=1), sk)
        cum_col, cum_row = cumsum_logf(lf_all)
        o = forget_attn(shp(q), k_all, v_all, cum_col[:, n_past:n_past + seq], cum_row,
                        jnp.zeros((b, n_groups, 1), jnp.int32), tq=seq, q_offset=n_past)
    return ([o.reshape(m, d)], [w_out.astype(BF16)]), (k, v, lf)


def kernel(x_prompt, x_sample, cache_a_k, cache_a_v, cache_b_k, cache_b_v, cache_c_k, cache_c_v, cache_c_logf, cache_mem_k, cache_mem_v, mem_prompt, w_in_ab, w_out_ab, rel_bias_a, w_in_c, b_f_c, w_out_c, g_mix, g_xattn, g_mem, w_xq, w_xk, w_xv, w_xo, g_ffn, w_gate, w_up, w_down, g_final):
    bp, sp, d = x_prompt.shape
    bs, ss, _ = x_sample.shape
    depth = g_mix.shape[0]
    n_mem = mem_prompt.shape[1]
    xp = x_prompt.reshape(bp * sp, d)
    xs = x_sample.reshape(bs * ss, d)
    tmp = _row_tile(bp * sp, sp, 512)
    tms = bs * ss
    assert tms <= 512
    mem = mem_prompt.reshape(bp * n_mem, d)
    tmm = _row_tile(bp * n_mem, n_mem, 512)
    dff = w_gate.shape[2]
    tf = 256 if dff % 256 == 0 else dff

    a_kp, a_vp, b_kp, b_vp, a_ks, a_vs, b_ks, b_vs = [], [], [], [], [], [], [], []
    c_kp, c_vp, c_lfp, c_ks, c_vs, c_lfs = [], [], [], [], [], []
    mem_kp, mem_vp = [], []
    for layer in range(depth):
        if layer % 2 == 0:
            e = layer // 2
            mix_p, (ka, va, kb, vb) = _layer_ab(xp, sp, g_mix[layer], w_in_ab[e], w_out_ab[e],
                                                rel_bias_a[e], None, tm=tmp)
            a_kp.append(ka); a_vp.append(va); b_kp.append(kb); b_vp.append(vb)
            mix_s, (ka, va, kb, vb) = _layer_ab(
                xs, ss, g_mix[layer], w_in_ab[e], w_out_ab[e], rel_bias_a[e],
                (cache_a_k[e], cache_a_v[e], cache_b_k[e], cache_b_v[e]), tm=tms)
            a_ks.append(ka); a_vs.append(va); b_ks.append(kb); b_vs.append(vb)
        else:
            c = layer // 2
            mix_p, (k, v, lf) = _layer_c(xp, sp, g_mix[layer], w_in_c[c], b_f_c[c], w_out_c[c],
                                         None, tm=tmp)
            c_kp.append(k); c_vp.append(v); c_lfp.append(lf)
            mix_s, (k, v, lf) = _layer_c(xs, ss, g_mix[layer], w_in_c[c], b_f_c[c], w_out_c[c],
                                         (cache_c_k[c], cache_c_v[c], cache_c_logf[c]), tm=tms)
            c_ks.append(k); c_vs.append(v); c_lfs.append(lf)
        mk, mv, mk16, mv16 = norm_proj(
            mem, g_mem[layer], [w_xk[layer].astype(BF16), w_xv[layer].astype(BF16)],
            [(0, 'heads', d // XA_HEADS), (1, 'heads', d // XA_HEADS), (0, 'bf16', 1.0),
             (1, 'bf16', 1.0)],
            seq=n_mem, tm=tmm)
        mem_kp.append(mk); mem_vp.append(mv)
        tail = functools.partial(
            layer_tail, g_x=g_xattn[layer], wq=w_xq[layer].astype(BF16),
            wo=w_xo[layer].astype(BF16), g_f=g_ffn[layer], wg=w_gate[layer].astype(BF16),
            wu=w_up[layer].astype(BF16), wd=w_down[layer].astype(BF16), g_final=g_final, tf=tf,
            final_norm=layer == depth - 1)
        xp = tail(*mix_p, xp, mk=mk16.reshape(bp, n_mem, d), mv=mv16.reshape(bp, n_mem, d),
                  seq=sp, tm=tmp)
        xs = tail(*mix_s, xs, mk=cache_mem_k[layer].reshape(bs, n_mem, d).astype(BF16),
                  mv=cache_mem_v[layer].reshape(bs, n_mem, d).astype(BF16), seq=ss, tm=tms)

    hd = HEAD_DIM
    xa_hd = d // XA_HEADS
    r5 = lambda lst, b, s, dd: jnp.stack([a.reshape(b, s, -1, dd) for a in lst])
    r4 = lambda lst, b, s: jnp.stack([a.reshape(b, s, -1) for a in lst])
    keep = min(A_PAST, sp)
    return (xp.reshape(bp, sp, d), xs.reshape(bs, ss, d),
            r5(a_kp, bp, keep, hd), r5(a_vp, bp, keep, hd), r5(b_kp, bp, sp, hd), r5(b_vp, bp, sp, hd),
            r5(c_kp, bp, sp, hd), r5(c_vp, bp, sp, hd), r4(c_lfp, bp, sp),
            r5(mem_kp, bp, n_mem, xa_hd), r5(mem_vp, bp, n_mem, xa_hd),
            r5(a_ks, bs, ss, hd), r5(a_vs, bs, ss, hd), r5(b_ks, bs, ss, hd), r5(b_vs, bs, ss, hd),
            r5(c_ks, bs, ss, hd), r5(c_vs, bs, ss, hd), r4(c_lfs, bs, ss))
```

```python
import functools

import jax
import jax.numpy as jnp
from jax import lax
from jax.experimental import pallas as pl
from jax.experimental.pallas import tpu as pltpu

F32 = jnp.float32
BF16 = jnp.bfloat16

RMS_EPS = 1e-6
NEG_INF = -1e30
LOG2E = 1.4426950408889634
HEAD_DIM = 64
CHUNK = 64
A_PAST_CHUNKS = 8
A_PAST = A_PAST_CHUNKS * CHUNK
REL_CLIP = 128
XA_HEADS = 4

LANES = 128
HEADS_PER_VREG = LANES // HEAD_DIM
KEY_CHUNK = 256
ATTN_PAIRS_PER_STEP = 4
VMEM_LIMIT = 56 * 1024 * 1024


def _cparams(*sem):
    return pltpu.CompilerParams(dimension_semantics=sem, vmem_limit_bytes=VMEM_LIMIT)


def _resident_spec(shape):
    return pl.BlockSpec(shape, lambda *_: (0,) * len(shape), pipeline_mode=pl.Buffered(1))


def _rms_bf16(x, g):
    y = x * lax.rsqrt(jnp.mean(x * x, axis=-1, keepdims=True) + RMS_EPS)
    return (y * g).astype(BF16)


def _log_sigmoid(z):
    return jnp.minimum(z, 0.0) - jnp.log1p(jnp.exp(-jnp.abs(z)))


def _split3(x):
    hi = x.astype(BF16)
    r = x - hi.astype(F32)
    mid = r.astype(BF16)
    lo = (r - mid.astype(F32)).astype(BF16)
    return hi, mid, lo


def _split2(x):
    hi = x.astype(BF16)
    lo = (x - hi.astype(F32)).astype(BF16)
    return hi, lo


def _dot(a, b):
    return jnp.dot(a, b, preferred_element_type=F32)


def _dot_nt(a, b):
    return lax.dot_general(a, b, (((1,), (1,)), ((), ())), preferred_element_type=F32)


def _norm_proj_kernel(x_ref, g_ref, b_ref, *refs, n_w, outs, tiles_per_seq):
    w_refs, o_refs = refs[:n_w], refs[n_w:]
    h = _rms_bf16(x_ref[...], g_ref[...])
    ys = {}
    for o_ref, (grp, kind, arg) in zip(o_refs, outs):
        if grp not in ys:
            ys[grp] = _dot(h, w_refs[grp][...])
        y = ys[grp]
        if kind == 'f32':
            o_ref[...] = y
        elif kind == 'bf16':
            o_ref[...] = (y * arg).astype(BF16)
        elif kind == 'logf':
            o_ref[...] = _log_sigmoid(y + b_ref[...])
        elif kind == 'heads':
            o_ref[...] = y.reshape(o_ref.shape)
        elif kind == 'rowmax':
            hd, scale = arg
            n_cols, n_heads = y.shape[1], y.shape[1] // hd
            sel_l = lax.broadcasted_iota(jnp.int32, (n_cols, n_heads), 0) // hd
            sel_h = lax.broadcasted_iota(jnp.int32, (n_cols, n_heads), 1)
            scaled = y * scale
            norms = jnp.sqrt(_dot((scaled * scaled).astype(BF16),
                                  jnp.where(sel_l == sel_h, 1.0, 0.0).astype(BF16)))
            for r in range(o_ref.shape[1]):
                o_ref[0, r:r + 1, :] = jnp.max(norms[r * KEY_CHUNK:(r + 1) * KEY_CHUNK],
                                               axis=0, keepdims=True)
        else:
            @pl.when(pl.program_id(0) % tiles_per_seq == tiles_per_seq - 1)
            def _(o_ref=o_ref, y=y):
                o_ref[...] = y.reshape(o_ref.shape)


def norm_proj(x, g, ws, outs, *, seq, tm, bias=None):
    m, d = x.shape
    tiles_per_seq = max(seq // tm, 1)
    if bias is None:
        bias = jnp.zeros((1, 16), F32)
    in_specs = [pl.BlockSpec((tm, d), lambda i: (i, 0)),
                pl.BlockSpec((1, d), lambda i: (0, 0)),
                pl.BlockSpec(bias.shape, lambda i: (0, 0))]
    in_specs += [_resident_spec(w.shape) for w in ws]
    out_shape, out_specs = [], []
    for grp, kind, arg in outs:
        n = ws[grp].shape[1]
        if kind == 'tail':
            assert tm == min(A_PAST, seq)
            out_shape.append(jax.ShapeDtypeStruct((m // tiles_per_seq, n // arg, arg), F32))
            out_specs.append(pl.BlockSpec((tm, n // arg, arg), lambda i: (i // tiles_per_seq, 0, 0)))
        elif kind == 'heads':
            out_shape.append(jax.ShapeDtypeStruct((m, n // arg, arg), F32))
            out_specs.append(pl.BlockSpec((tm, n // arg, arg), lambda i: (i, 0, 0)))
        elif kind == 'rowmax':
            assert tm % KEY_CHUNK == 0
            blocks = (tm // KEY_CHUNK, n // arg[0])
            out_shape.append(jax.ShapeDtypeStruct((m // tm,) + blocks, F32))
            out_specs.append(pl.BlockSpec((1,) + blocks, lambda i: (i, 0, 0)))
        else:
            out_shape.append(jax.ShapeDtypeStruct((m, n), BF16 if kind == 'bf16' else F32))
            out_specs.append(pl.BlockSpec((tm, n), lambda i: (i, 0)))
    kern = functools.partial(_norm_proj_kernel, n_w=len(ws), outs=tuple(outs),
                             tiles_per_seq=tiles_per_seq)
    return pl.pallas_call(
        kern, grid=(m // tm,), in_specs=in_specs, out_specs=out_specs, out_shape=out_shape,
        compiler_params=_cparams("arbitrary"), name="norm_proj")(x, g.reshape(1, d), bias, *ws)


BIAS_BLOCK_ROWS = 32


def _band_bias_kernel(rb_ref, off_ref, o_ref, *, n_real):
    n_heads, rows, cols = o_ref.shape[1:]
    off = off_ref[pl.program_id(0)]
    i0 = pl.program_id(1) * rows
    r0 = pl.program_id(2) * cols
    i = lax.broadcasted_iota(jnp.int32, (rows, cols), 0) + i0
    r = lax.broadcasted_iota(jnp.int32, (rows, cols), 1) + r0
    d = jnp.clip(off + i - r, -REL_CLIP, REL_CLIP) + REL_CLIP
    lo = jnp.clip(off + i0 - (r0 + cols - 1), -REL_CLIP, REL_CLIP) + REL_CLIP
    hi = jnp.clip(off + i0 + rows - 1 - r0, -REL_CLIP, REL_CLIP) + REL_CLIP
    first = (i // CHUNK - A_PAST_CHUNKS) * CHUNK + off
    last = jnp.minimum((i // CHUNK + 1) * CHUNK + off, n_real)
    visible = (r >= first) & (r < last)
    first0 = (i0 // CHUNK - A_PAST_CHUNKS) * CHUNK + off
    last0 = jnp.minimum(((i0 + rows - 1) // CHUNK + 1) * CHUNK + off, n_real)
    hi = jnp.where((r0 + cols <= first0) | (r0 >= last0), lo - 1, hi)

    def body(u, tbls):
        hit = d == u
        return tuple(jnp.where(hit, rb_ref[h, u], t) for h, t in enumerate(tbls))

    tbls = lax.fori_loop(lo, hi + 1, body,
                         tuple(jnp.zeros((rows, cols), F32) for _ in range(n_heads)))
    for h in range(n_heads):
        o_ref[0, h] = jnp.where(visible, LOG2E * tbls[h], NEG_INF)


def band_bias(rel_bias, offsets, *, rows, cols, n_real):
    n_rel, n_heads = rel_bias.shape
    blk = min(rows, BIAS_BLOCK_ROWS)
    return pl.pallas_call(
        functools.partial(_band_bias_kernel, n_real=n_real),
        grid=(len(offsets), rows // blk, cols // LANES),
        in_specs=[pl.BlockSpec(memory_space=pltpu.SMEM), pl.BlockSpec(memory_space=pltpu.SMEM)],
        out_specs=pl.BlockSpec((1, n_heads, blk, LANES), lambda o, t, c: (o, 0, t, c)),
        out_shape=jax.ShapeDtypeStruct((len(offsets), n_heads, rows, cols), F32),
        compiler_params=_cparams("arbitrary", "arbitrary", "arbitrary"),
        name="band_bias")(rel_bias.T, jnp.asarray(offsets, jnp.int32))


def _head_masks(shape):
    lane = lax.broadcasted_iota(jnp.int32, shape, len(shape) - 1)
    return [(lane % LANES) // HEAD_DIM == j for j in range(HEADS_PER_VREG)]


def _ones_lanes():
    return [((j + 1) % HEADS_PER_VREG) * HEAD_DIM for j in range(HEADS_PER_VREG)]


def _masked_values(v, j):
    lane = lax.broadcasted_iota(jnp.int32, v.shape, 1) % LANES
    vj = jnp.where(lane // HEAD_DIM == j, v, jnp.zeros((), BF16))
    return jnp.where(lane == _ones_lanes()[j], jnp.ones((), BF16), vj)


def _band_attn_kernel(q_ref, k_ref, v_ref, bias_ref, o_ref, v16, s_scr, p_scr, *, tq, band):
    step = pl.program_id(1)
    ck = KEY_CHUNK

    @pl.when(step == 0)
    def _():
        v = v_ref[0]
        for j in range(HEADS_PER_VREG):
            v16[j] = _masked_values(v, j)

    start = pl.multiple_of(jnp.maximum(step * (tq // CHUNK) - A_PAST_CHUNKS, 0) * CHUNK, CHUNK)
    qmask = _head_masks((tq, LANES))
    ones_lane = _ones_lanes()
    cols = [slice(c * ck, (c + 1) * ck) for c in range(band // ck)]
    heads = [(hp, j) for hp in range(q_ref.shape[2] // LANES) for j in range(HEADS_PER_VREG)]
    lanes = [slice(hp * LANES, (hp + 1) * LANES) for hp, _ in heads]
    qs = [jnp.where(qmask[j], q_ref[0, :, lanes[n]], jnp.zeros((), BF16))
          for n, (_, j) in enumerate(heads)]

    def logits(n, c, mrun):
        k = k_ref[0, pl.ds(pl.multiple_of(start + c * ck, CHUNK), ck), lanes[n]]
        s = _dot_nt(qs[n], k) + bias_ref[0, n, :, cols[c]]
        s_scr[n % 2, :, cols[c]] = s
        for part in range(ck // LANES):
            mrun = jnp.maximum(mrun, s[:, part * LANES:(part + 1) * LANES])
        return mrun

    def probs(n, c, row_max):
        p_scr[n % 2, :, cols[c]] = jnp.exp2(s_scr[n % 2, :, cols[c]] - row_max).astype(BF16)

    def pv(n):
        return _dot(p_scr[n % 2], v16[heads[n][1], pl.ds(start, band), lanes[n]])

    neg = jnp.full((tq, LANES), NEG_INF, F32)
    outs = []
    mrun = functools.reduce(lambda m, c: logits(0, c, m), range(len(cols)), neg)
    for n in range(len(heads)):
        row_max = jnp.max(mrun, axis=-1, keepdims=True)
        mrun = neg
        for c in range(len(cols)):
            if n + 1 < len(heads):
                mrun = logits(n + 1, c, mrun)
            probs(n, c, row_max)
        o = pv(n)
        lane = ones_lane[heads[n][1]]
        outs.append(o / o[:, lane:lane + 1])
    for n in range(0, len(heads), HEADS_PER_VREG):
        o_ref[0, :, lanes[n]] = jnp.where(qmask[0], outs[n], outs[n + 1]).astype(BF16)


def band_attn(q, k, v, bias, *, tq, band):
    b, sq, w = q.shape
    sk = k.shape[1]
    n_off, n_heads = bias.shape[:2]
    assert band % KEY_CHUNK == 0 and tq % CHUNK == 0 or sq == tq
    kern = functools.partial(_band_attn_kernel, tq=tq, band=band)
    return pl.pallas_call(
        kern, grid=(b, sq // tq),
        in_specs=[pl.BlockSpec((1, tq, w), lambda i, c: (i, c, 0)),
                  pl.BlockSpec((1, sk, w), lambda i, c: (i, 0, 0)),
                  pl.BlockSpec((1, sk, w), lambda i, c: (i, 0, 0)),
                  pl.BlockSpec((1, n_heads, tq, band),
                               lambda i, c: (jnp.minimum(c, n_off - 1), 0, 0, 0))],
        out_specs=pl.BlockSpec((1, tq, w), lambda i, c: (i, c, 0)),
        out_shape=jax.ShapeDtypeStruct((b, sq, w), BF16),
        scratch_shapes=[pltpu.VMEM((HEADS_PER_VREG, sk, w), BF16),
                        pltpu.VMEM((2, tq, band), F32),
                        pltpu.VMEM((2, tq, band), BF16)],
        compiler_params=_cparams("parallel", "arbitrary"), name="band_attn")(q, k, v, bias)


STICK_UNDERFLOW_LOG2 = 160.0
STICK_NEAR_CHUNKS = 2


def _stick_kernel(q_ref, k_ref, v_ref, o_ref, v16, zl_scr, hl_scr, w_scr, *,
                  tq, q_offset, variants):
    qi = pl.program_id(2)
    ck = KEY_CHUNK
    heads = [(pp, j) for pp in range(q_ref.shape[2] // LANES) for j in range(HEADS_PER_VREG)]
    n_heads = len(heads)
    lanes = [slice(pp * LANES, (pp + 1) * LANES) for pp, _ in heads]

    @pl.when(qi == 0)
    def _():
        v = v_ref[0]
        vmask = _head_masks(v.shape)
        for j in range(HEADS_PER_VREG):
            v16[j] = jnp.where(vmask[j], v, jnp.zeros((), BF16))

    qmask = _head_masks((tq, LANES))
    qs = [jnp.where(qmask[j], q_ref[0, :, lanes[n]], jnp.zeros((), BF16))
          for n, (_, j) in enumerate(heads)]
    q_pos = q_offset + qi * tq + lax.broadcasted_iota(jnp.int32, (tq, ck), 0)
    k_iota = lax.broadcasted_iota(jnp.int32, (tq, ck), 1)
    rr = lax.broadcasted_iota(jnp.int32, (2 * ck, ck), 0) % ck
    cc = lax.broadcasted_iota(jnp.int32, (2 * ck, ck), 1)
    tri2 = jnp.where(rr > cc, 1.0, 0.0).astype(BF16)
    n_chunks = (q_offset + (qi + 1) * tq + ck - 1) // ck

    def split_cols(slot):
        return slice(2 * slot.start, 2 * slot.start + ck), slice(2 * slot.start + ck, 2 * slot.stop)

    def logits(j, slot, c, masked):
        k0 = pl.multiple_of(c * ck, ck)
        z = _dot_nt(qs[j], k_ref[0, pl.ds(k0, ck), lanes[j]])
        neg_abs = lax.bitcast_convert_type(
            lax.bitcast_convert_type(z, jnp.int32) | jnp.int32(-2 ** 31), F32)
        sp = jnp.maximum(z, 0.0) + jnp.log2(1.0 + jnp.exp2(neg_abs))
        if masked:
            sp = jnp.where(k0 + k_iota < q_pos, sp, 0.0)
        zl_scr[j, :, slot] = z - sp
        hi_cols, lo_cols = split_cols(slot)
        hl_scr[j, :, hi_cols], hl_scr[j, :, lo_cols] = _split2(sp)
        return jnp.sum(sp, axis=-1, keepdims=True)

    def weights(j, slot, c, later, masked):
        sums = _dot(hl_scr[j, :, 2 * slot.start:2 * slot.stop], tri2)
        w = jnp.exp2(zl_scr[j, :, slot] - sums - later)
        if masked:
            w = jnp.where(c * ck + k_iota < q_pos, w, 0.0)
        w_scr[j, :, slot] = w.astype(BF16)

    def more(laters):
        return (jnp.min(functools.reduce(jnp.minimum, laters))
                < STICK_UNDERFLOW_LOG2).astype(jnp.int32)

    def attend(near, n_masked):
        slots = [slice((near - 1 - i) * ck, (near - i) * ck) for i in range(near)]
        chunk = [n_chunks - 1 - i for i in range(near)]
        k0 = pl.multiple_of((n_chunks - near) * ck, ck)

        def pv(n, k0, width):
            return _dot(w_scr[n, :, :width], v16[heads[n][1], pl.ds(k0, width), lanes[n]])

        laters, accs = [], []
        row_sums = [logits(0, slots[i], chunk[i], i < n_masked) for i in range(near)]
        for n in range(n_heads):
            later = jnp.zeros((tq, 1), F32)
            next_sums = []
            for i in range(near):
                if n + 1 < n_heads:
                    next_sums.append(logits(n + 1, slots[i], chunk[i], i < n_masked))
                weights(n, slots[i], chunk[i], later, i < n_masked)
                later = later + row_sums[i]
            laters.append(later)
            row_sums = next_sums
            o = pv(n, k0, near * ck)
            if heads[n][1] == 0:
                accs.append(o)
            else:
                accs[-1] = accs[-1] + o

        def cond(carry):
            return (carry[0] >= 0) & (carry[1] > 0)

        def body(carry):
            c, _, accs, laters = carry
            accs = list(accs)
            k0 = pl.multiple_of(c * ck, ck)
            new = []
            for n in range(n_heads):
                rs = logits(n, slice(0, ck), c, False)
                weights(n, slice(0, ck), c, laters[n], False)
                accs[heads[n][0]] = accs[heads[n][0]] + pv(n, k0, ck)
                new.append(laters[n] + rs)
            return c - 1, more(new), tuple(accs), tuple(new)

        carry = lax.while_loop(cond, body,
                               (n_chunks - 1 - near, more(laters), tuple(accs), tuple(laters)))
        for pp, acc in enumerate(carry[2]):
            o_ref[0, :, pp * LANES:(pp + 1) * LANES] = acc.astype(BF16)

    for n_total, near, n_masked in variants:
        if n_total is None:
            pl.when(n_chunks >= near)(functools.partial(attend, near, n_masked))
        else:
            pl.when(n_chunks == n_total)(functools.partial(attend, near, n_masked))


def stick_attn(q, k, v, *, tq, q_offset):
    b, sq, w = q.shape
    sk = k.shape[1]
    ck = KEY_CHUNK
    assert q_offset % ck == 0 and (tq % ck == 0 or sq == tq <= ck) and sk % ck == 0
    n_masked = -(-tq // ck)
    totals = sorted({-(-(q_offset + (t + 1) * tq) // ck) for t in range(sq // tq)})
    variants = [(n, n, min(n_masked, n)) for n in totals if n < STICK_NEAR_CHUNKS]
    if totals[-1] >= STICK_NEAR_CHUNKS:
        variants.append((None, STICK_NEAR_CHUNKS, n_masked))
    near_cols = STICK_NEAR_CHUNKS * ck
    wg = ATTN_PAIRS_PER_STEP * LANES
    n_heads = ATTN_PAIRS_PER_STEP * HEADS_PER_VREG
    kern = functools.partial(_stick_kernel, tq=tq, q_offset=q_offset, variants=tuple(variants))
    return pl.pallas_call(
        kern, grid=(b, w // wg, sq // tq),
        in_specs=[pl.BlockSpec((1, tq, wg), lambda i, h, t: (i, t, h)),
                  pl.BlockSpec((1, sk, wg), lambda i, h, t: (i, 0, h)),
                  pl.BlockSpec((1, sk, wg), lambda i, h, t: (i, 0, h))],
        out_specs=pl.BlockSpec((1, tq, wg), lambda i, h, t: (i, t, h)),
        out_shape=jax.ShapeDtypeStruct((b, sq, w), BF16),
        scratch_shapes=[pltpu.VMEM((HEADS_PER_VREG, sk, wg), BF16),
                        pltpu.VMEM((n_heads, tq, near_cols), F32),
                        pltpu.VMEM((n_heads, tq, 2 * near_cols), BF16),
                        pltpu.VMEM((n_heads, tq, near_cols), BF16)],
        compiler_params=_cparams("parallel", "parallel", "arbitrary"), name="stick_attn")(q, k, v)


def _cumsum_kernel(lf_ref, lft_ref, col_ref, row_ref, *, blk):
    s_len, n_heads = lf_ref.shape[1], lf_ref.shape[2]
    rr = lax.broadcasted_iota(jnp.int32, (blk, blk), 0)
    cc = lax.broadcasted_iota(jnp.int32, (blk, blk), 1)
    lower = jnp.where(rr >= cc, 1.0, 0.0).astype(BF16)
    upper = jnp.where(rr <= cc, 1.0, 0.0).astype(BF16)
    carry_col = jnp.zeros((1, n_heads), F32)
    carry_row = jnp.zeros((n_heads, 1), F32)
    for n in range(s_len // blk):
        rows = slice(n * blk, (n + 1) * blk)
        c = sum(_dot(lower, p) for p in _split3(lf_ref[0, rows, :])) + carry_col
        col_ref[0, rows, :] = c
        carry_col = c[blk - 1:blk, :]
        ct = sum(_dot(p, upper) for p in _split3(lft_ref[0, :, rows])) + carry_row
        row_ref[0, :, rows] = ct
        carry_row = ct[:, blk - 1:blk]


def cumsum_logf(lf):
    b, s_len, n_heads = lf.shape
    kern = functools.partial(_cumsum_kernel, blk=LANES)
    return pl.pallas_call(
        kern, grid=(b,),
        in_specs=[pl.BlockSpec((1, s_len, n_heads), lambda i: (i, 0, 0)),
                  pl.BlockSpec((1, n_heads, s_len), lambda i: (i, 0, 0))],
        out_specs=[pl.BlockSpec((1, s_len, n_heads), lambda i: (i, 0, 0)),
                   pl.BlockSpec((1, n_heads, s_len), lambda i: (i, 0, 0))],
        out_shape=[jax.ShapeDtypeStruct((b, s_len, n_heads), F32),
                   jax.ShapeDtypeStruct((b, n_heads, s_len), F32)],
        compiler_params=_cparams("parallel"), name="cumsum_logf")(lf, jnp.swapaxes(lf, 1, 2))


FORGET_SKIP_LOG2 = 152.0
FORGET_BOUND_SLACK = 1.02
FORGET_BOUND_MARGIN = 2.0


def _forget_skip_kernel(qmax_ref, kmax_ref, cum_ref, o_ref, *, heads_per_group):
    n_blk, n_heads = kmax_ref.shape[1], kmax_ref.shape[2]
    blk = cum_ref.shape[1] // n_blk
    kmax = kmax_ref[0]
    k_term = kmax + jnp.max(kmax, axis=0, keepdims=True)
    cum_end = cum_ref[0, pl.ds(blk - 1, n_blk, stride=blk), :]
    cum_start = cum_ref[0, pl.ds(0, n_blk, stride=blk), :]
    chunk_id = lax.broadcasted_iota(jnp.int32, (n_blk, 1), 0)
    head_id = lax.broadcasted_iota(jnp.int32, (n_blk, n_heads), 1)
    row = lax.broadcasted_iota(jnp.int32, o_ref.shape[1:], 0)
    col = lax.broadcasted_iota(jnp.int32, o_ref.shape[1:], 1)
    table = jnp.zeros(o_ref.shape[1:], F32)
    for qi in range(n_blk):
        bound = (FORGET_BOUND_SLACK * qmax_ref[0, qi:qi + 1, :] * k_term
                 + LOG2E * (cum_start[qi:qi + 1, :] - cum_end) + FORGET_BOUND_MARGIN)
        for g in range(n_heads // heads_per_group):
            worst = jnp.max(jnp.where(head_id // heads_per_group == g, bound, NEG_INF),
                            axis=1, keepdims=True)
            needed = (worst >= -FORGET_SKIP_LOG2) | (chunk_id >= qi)
            first = jnp.min(jnp.where(needed, chunk_id, n_blk).astype(F32), axis=0, keepdims=True)
            table = jnp.where((row == qi) & (col == g), first, table)
    o_ref[0] = table.astype(jnp.int32)


def forget_skip_table(qmax, kmax, cum, *, heads_per_group):
    b, n_blk, n_heads = kmax.shape
    s_len = cum.shape[1]
    blk3 = lambda shape: pl.BlockSpec((1,) + shape, lambda i: (i, 0, 0))
    return pl.pallas_call(
        functools.partial(_forget_skip_kernel, heads_per_group=heads_per_group), grid=(b,),
        in_specs=[blk3((n_blk, n_heads)), blk3((n_blk, n_heads)), blk3((s_len, n_heads))],
        out_specs=blk3((n_blk, LANES)),
        out_shape=jax.ShapeDtypeStruct((b, n_blk, LANES), jnp.int32),
        compiler_params=_cparams("parallel"), name="forget_skip_table")(qmax, kmax, cum)


def _forget_kernel(c0_ref, q_ref, k_ref, v_ref, cq_ref, ck_ref, o_ref, v16, s_scr, p_scr, *,
                   tq, q_offset, max_chunks):
    group = pl.program_id(1)
    qi = pl.program_id(2)
    wg = k_ref.shape[2]
    ck_w = KEY_CHUNK
    ones_lane = _ones_lanes()
    heads = [(pp, j) for pp in range(wg // LANES) for j in range(HEADS_PER_VREG)]
    lanes = [slice(pp * LANES, (pp + 1) * LANES) for pp, _ in heads]

    @pl.when(qi == 0)
    def _():
        v = v_ref[0]
        for j in range(HEADS_PER_VREG):
            v16[j] = _masked_values(v, j)

    qmask = _head_masks((tq, LANES))
    q_pos = q_offset + qi * tq + lax.broadcasted_iota(jnp.int32, (tq, ck_w), 0)
    k_iota = lax.broadcasted_iota(jnp.int32, (tq, ck_w), 1)
    cq_all = cq_ref[0]
    head_lane = lax.broadcasted_iota(jnp.int32, cq_all.shape, 1)
    n_c = (q_offset + (qi + 1) * tq + ck_w - 1) // ck_w
    step = (pl.program_id(0) * pl.num_programs(1) + group) * pl.num_programs(2) + qi
    c0 = jnp.minimum(c0_ref[step], n_c - 1)
    n_proc_here = n_c - c0

    def attend(n_proc):
        cols = [slice(c * ck_w, (c + 1) * ck_w) for c in range(n_proc)]
        k0 = [pl.multiple_of((c0 + c) * ck_w, ck_w) for c in range(n_proc)]
        allowed = k0[-1] + k_iota <= q_pos
        qs = [jnp.where(qmask[j], q_ref[0, :, lanes[n]], jnp.zeros((), BF16))
              for n, (_, j) in enumerate(heads)]
        cq = [LOG2E * jnp.sum(jnp.where(head_lane == group * len(heads) + n, cq_all, 0.0),
                              axis=-1, keepdims=True) for n in range(len(heads))]

        def logits(n, c, mrun):
            s = (_dot_nt(qs[n], k_ref[0, pl.ds(k0[c], ck_w), lanes[n]])
                 - LOG2E * ck_ref[0, 0, n:n + 1, pl.ds(k0[c], ck_w)])
            if c == n_proc - 1:
                s = jnp.where(allowed, s, NEG_INF)
            s_scr[n % 2, :, cols[c]] = s
            for part in range(ck_w // LANES):
                mrun = jnp.maximum(mrun, s[:, part * LANES:(part + 1) * LANES])
            return mrun

        def row_term(n, mrun):
            m = jnp.max(mrun, axis=-1, keepdims=True) + cq[n]
            return cq[n] - m

        def probs(n, c, row):
            p_scr[n % 2, :, cols[c]] = jnp.exp2(s_scr[n % 2, :, cols[c]] + row).astype(BF16)

        def pv(n):
            return _dot(p_scr[n % 2, :, :n_proc * ck_w],
                        v16[heads[n][1], pl.ds(k0[0], n_proc * ck_w), lanes[n]])

        neg = jnp.full((tq, LANES), NEG_INF, F32)
        outs = []
        mrun = functools.reduce(lambda m, c: logits(0, c, m), range(n_proc), neg)
        for n in range(len(heads)):
            row = row_term(n, mrun)
            mrun = neg
            for c in range(n_proc):
                if n + 1 < len(heads):
                    mrun = logits(n + 1, c, mrun)
                probs(n, c, row)
            o = pv(n)
            lane = ones_lane[heads[n][1]]
            outs.append(o / o[:, lane:lane + 1])
        for n in range(0, len(heads), HEADS_PER_VREG):
            o_ref[0, :, lanes[n]] = jnp.where(qmask[0], outs[n], outs[n + 1]).astype(BF16)

    for n_proc in range(1, max_chunks + 1):
        pl.when(n_proc_here == n_proc)(functools.partial(attend, n_proc))


def forget_attn(q, k, v, cum_q, cum_k_rows, skip, *, tq, q_offset):
    b, sq, w = q.shape
    sk = k.shape[1]
    n_heads = cum_q.shape[2]
    ck_w = KEY_CHUNK
    diag = [-(-(q_offset + (t + 1) * tq) // ck_w) - 1 for t in range(sq // tq)]
    assert sk % ck_w == 0 and all((q_offset + t * tq) // ck_w == c for t, c in enumerate(diag))
    max_chunks = max(diag) + 1
    wg = ATTN_PAIRS_PER_STEP * LANES
    heads_per_step = ATTN_PAIRS_PER_STEP * HEADS_PER_VREG
    assert skip.shape == (b, w // wg, sq // tq)
    ck = cum_k_rows.reshape(b, n_heads // heads_per_step, heads_per_step, sk)
    kern = functools.partial(_forget_kernel, tq=tq, q_offset=q_offset, max_chunks=max_chunks)
    grid_spec = pltpu.PrefetchScalarGridSpec(
        num_scalar_prefetch=1, grid=(b, w // wg, sq // tq),
        in_specs=[pl.BlockSpec((1, tq, wg), lambda i, h, t, c0: (i, t, h)),
                  pl.BlockSpec((1, sk, wg), lambda i, h, t, c0: (i, 0, h)),
                  pl.BlockSpec((1, sk, wg), lambda i, h, t, c0: (i, 0, h)),
                  pl.BlockSpec((1, tq, n_heads), lambda i, h, t, c0: (i, t, 0)),
                  pl.BlockSpec((1, 1, heads_per_step, sk), lambda i, h, t, c0: (i, h, 0, 0))],
        out_specs=pl.BlockSpec((1, tq, wg), lambda i, h, t, c0: (i, t, h)),
        scratch_shapes=[pltpu.VMEM((HEADS_PER_VREG, sk, wg), BF16),
                        pltpu.VMEM((2, tq, max_chunks * ck_w), F32),
                        pltpu.VMEM((2, tq, max_chunks * ck_w), BF16)])
    return pl.pallas_call(
        kern, grid_spec=grid_spec, out_shape=jax.ShapeDtypeStruct((b, sq, w), BF16),
        compiler_params=_cparams("parallel", "parallel", "arbitrary"),
        name="forget_attn")(skip.reshape(-1), q, k, v, cum_q, ck)


def _layer_tail_kernel(*refs, n_in, n_heads, rows_per_seq, tf, final_norm):
    a_refs, w_refs = refs[:n_in], refs[n_in:2 * n_in]
    (x_ref, gx_ref, wq_ref, wo_ref, mk_ref, mv_ref, gf_ref, wg_ref, wu_ref, wd_ref, gfin_ref,
     o_ref) = refs[2 * n_in:]
    x = x_ref[...]
    for a_ref, w_ref in zip(a_refs, w_refs):
        x = x + _dot(a_ref[...], w_ref[...])

    hd = x.shape[1] // n_heads
    q = (_dot(_rms_bf16(x, gx_ref[...]), wq_ref[...]) * (hd ** -0.5)).astype(BF16)
    per_seq = []
    for b in range(mk_ref.shape[0]):
        rows = slice(b * rows_per_seq, (b + 1) * rows_per_seq)
        outs = []
        for j in range(n_heads):
            cols = slice(j * hd, (j + 1) * hd)
            s = _dot_nt(q[rows, cols], mk_ref[b, :, cols])
            p = jnp.exp(s - jnp.max(s, axis=-1, keepdims=True))
            p = p / jnp.sum(p, axis=-1, keepdims=True)
            outs.append(_dot(p.astype(BF16), mv_ref[b, :, cols]).astype(BF16))
        per_seq.append(jnp.concatenate(outs, axis=-1))
    x = x + _dot(jnp.concatenate(per_seq, axis=0), wo_ref[...])

    h = _rms_bf16(x, gf_ref[...])
    for c in range(wg_ref.shape[1] // tf):
        cols = slice(c * tf, (c + 1) * tf)
        gate = _dot(h, wg_ref[:, cols])
        up = _dot(h, wu_ref[:, cols])
        a = (gate * jax.nn.sigmoid(gate) * up).astype(BF16)
        x = x + _dot(a, wd_ref[cols, :])
    if final_norm:
        x = x * lax.rsqrt(jnp.mean(x * x, axis=-1, keepdims=True) + RMS_EPS) * gfin_ref[...]
    o_ref[...] = x


def layer_tail(a_list, w_list, x, g_x, wq, wo, mk, mv, g_f, wg, wu, wd, g_final, *, seq, tm, tf,
               final_norm):
    m, d = x.shape
    n_mem = mk.shape[1]
    rows_per_seq = min(tm, seq)
    seqs = tm // rows_per_seq
    assert seq % rows_per_seq == 0 and m % tm == 0
    row = lambda i: (i, 0)
    mem = lambda i: (i * tm // (seq * seqs), 0, 0)
    gain = lambda g: g.reshape(1, d)
    kern = functools.partial(_layer_tail_kernel, n_in=len(a_list), n_heads=XA_HEADS,
                             rows_per_seq=rows_per_seq, tf=tf, final_norm=final_norm)
    in_specs = ([pl.BlockSpec((tm, a.shape[1]), row) for a in a_list]
                + [_resident_spec(w.shape) for w in w_list]
                + [pl.BlockSpec((tm, d), row), _resident_spec((1, d)),
                   _resident_spec(wq.shape), _resident_spec(wo.shape),
                   pl.BlockSpec((seqs, n_mem, d), mem), pl.BlockSpec((seqs, n_mem, d), mem),
                   _resident_spec((1, d)), _resident_spec(wg.shape), _resident_spec(wu.shape),
                   _resident_spec(wd.shape), _resident_spec((1, d))])
    return pl.pallas_call(
        kern, grid=(m // tm,), in_specs=in_specs,
        out_specs=pl.BlockSpec((tm, d), row), out_shape=jax.ShapeDtypeStruct((m, d), F32),
        compiler_params=_cparams("parallel"), name="layer_tail")(
            *a_list, *w_list, x, gain(g_x), wq, wo, mk, mv, gain(g_f), wg, wu, wd, gain(g_final))


def _pad_rows(a, rows):
    return jnp.pad(a, ((0, 0), (0, rows - a.shape[1]), (0, 0)))


def _row_tile(m, seq, cap):
    tm = min(cap, seq)
    assert seq % tm == 0 and m % tm == 0
    return tm


def _layer_ab(x, seq, g, w_in, w_out, rel_bias, cache, *, tm):
    m, d = x.shape
    b = m // seq
    wa = w_in.shape[1] // 6
    ws = [w_in[:, n * wa:(n + 1) * wa].astype(BF16) for n in range(6)]
    scale = HEAD_DIM ** -0.5 * LOG2E
    kept = 'tail' if min(A_PAST, seq) < seq else 'heads'
    outs = [(4, 'heads', HEAD_DIM), (4, 'bf16', 1.0), (5, 'heads', HEAD_DIM), (5, 'bf16', 1.0),
            (0, 'bf16', scale), (3, 'bf16', scale), (1, 'bf16', 1.0), (2, 'bf16', 1.0),
            (1, kept, HEAD_DIM), (2, kept, HEAD_DIM)]
    kb, kb16, vb, vb16, qa, qb, ka16, va16, ka_keep, va_keep = norm_proj(x, g, ws, outs, seq=seq,
                                                                         tm=tm)
    shp = lambda a: a.reshape(b, -1, a.shape[-1])
    if cache is None:
        tq = 4 * CHUNK
        band = -(-(A_PAST + tq) // KEY_CHUNK) * KEY_CHUNK
        offsets = sorted({t * tq - max(t * (tq // CHUNK) - A_PAST_CHUNKS, 0) * CHUNK
                          for t in range(seq // tq)})
        bias = band_bias(rel_bias, offsets, rows=tq, cols=band, n_real=band)
        oa = band_attn(shp(qa), shp(ka16), shp(va16), bias, tq=tq, band=band)
        ob = stick_attn(shp(qb), shp(kb16), shp(vb16), tq=256, q_offset=0)
    else:
        ca_k, ca_v, cb_k, cb_v = cache
        n_past = ca_k.shape[1]
        n_keys = n_past + seq
        band = -(-n_keys // KEY_CHUNK) * KEY_CHUNK
        flat = lambda a: a.reshape(a.shape[0], a.shape[1], -1)
        k_all = _pad_rows(jnp.concatenate([flat(ca_k).astype(BF16), shp(ka16)], axis=1), band)
        v_all = _pad_rows(jnp.concatenate([flat(ca_v).astype(BF16), shp(va16)], axis=1), band)
        bias = band_bias(rel_bias, [n_past], rows=seq, cols=band, n_real=n_keys)
        oa = band_attn(shp(qa), k_all, v_all, bias, tq=seq, band=band)
        n_pastb = cb_k.shape[1]
        sk = -(-(n_pastb + seq) // KEY_CHUNK) * KEY_CHUNK
        kb_all = _pad_rows(jnp.concatenate([flat(cb_k).astype(BF16), shp(kb16)], axis=1), sk)
        vb_all = _pad_rows(jnp.concatenate([flat(cb_v).astype(BF16), shp(vb16)], axis=1), sk)
        ob = stick_attn(shp(qb), kb_all, vb_all, tq=seq, q_offset=n_pastb)
    w_out = w_out.astype(BF16)
    wo_a, wo_b = w_out[:oa.shape[-1]], w_out[oa.shape[-1]:]
    return ([oa.reshape(m, -1), ob.reshape(m, -1)], [wo_a, wo_b]), (ka_keep, va_keep, kb, vb)


def _layer_c(x, seq, g, w_in, b_f, w_out, cache, *, tm):
    m, d = x.shape
    b = m // seq
    n_heads = b_f.shape[0]
    wq, wk, wv, wf = (w_in[:, :d], w_in[:, d:2 * d], w_in[:, 2 * d:3 * d], w_in[:, 3 * d:])
    ws = [w.astype(BF16) for w in (wq, wk, wv, wf)]
    scale = HEAD_DIM ** -0.5 * LOG2E
    outs = [(1, 'heads', HEAD_DIM), (1, 'bf16', 1.0), (2, 'heads', HEAD_DIM), (2, 'bf16', 1.0),
            (0, 'bf16', scale), (3, 'logf', None)]
    if cache is None:
        outs += [(1, 'rowmax', (HEAD_DIM, 1.0)), (0, 'rowmax', (HEAD_DIM, scale))]
    k, k16, v, v16, q, lf, *norms = norm_proj(x, g, ws, outs, seq=seq, tm=tm,
                                              bias=b_f.reshape(1, n_heads))
    norms = norms[::-1]
    shp = lambda a: a.reshape(b, -1, a.shape[-1])
    heads_per_group = ATTN_PAIRS_PER_STEP * HEADS_PER_VREG
    n_groups = n_heads // heads_per_group
    if cache is None:
        cum_col, cum_row = cumsum_logf(shp(lf))
        tq = KEY_CHUNK
        qmax, kmax = (a.reshape(b, seq // KEY_CHUNK, n_heads) for a in norms)
        skip = forget_skip_table(qmax, kmax, cum_col, heads_per_group=heads_per_group)
        skip = jnp.swapaxes(skip[:, :, :n_groups], 1, 2)
        o = forget_attn(shp(q), shp(k16), shp(v16), cum_col, cum_row, skip, tq=tq, q_offset=0)
    else:
        c_k, c_v, c_lf = cache
        n_past = c_k.shape[1]
        sk = -(-(n_past + seq) // KEY_CHUNK) * KEY_CHUNK
        flat = lambda a: a.reshape(a.shape[0], a.shape[1], -1)
        k_all = _pad_rows(jnp.concatenate([flat(c_k).astype(BF16), shp(k16)], axis=1), sk)
        v_all = _pad_rows(jnp.concatenate([flat(c_v).astype(BF16), shp(v16)], axis=1), sk)
        lf_all = _pad_rows(jnp.concatenate([c_lf, shp(lf)], axis=1), sk)
        cum_col, cum_row = cumsum_logf(lf_all)
        o = forget_attn(shp(q), k_all, v_all, cum_col[:, n_past:n_past + seq], cum_row,
                        jnp.zeros((b, n_groups, 1), jnp.int32), tq=seq, q_offset=n_past)
    return ([o.reshape(m, d)], [w_out.astype(BF16)]), (k, v, lf)


def kernel(x_prompt, x_sample, cache_a_k, cache_a_v, cache_b_k, cache_b_v, cache_c_k, cache_c_v, cache_c_logf, cache_mem_k, cache_mem_v, mem_prompt, w_in_ab, w_out_ab, rel_bias_a, w_in_c, b_f_c, w_out_c, g_mix, g_xattn, g_mem, w_xq, w_xk, w_xv, w_xo, g_ffn, w_gate, w_up, w_down, g_final):
    bp, sp, d = x_prompt.shape
    bs, ss, _ = x_sample.shape
    depth = g_mix.shape[0]
    n_mem = mem_prompt.shape[1]
    xp = x_prompt.reshape(bp * sp, d)
    xs = x_sample.reshape(bs * ss, d)
    tmp = _row_tile(bp * sp, sp, 512)
    tms = bs * ss
    assert tms <= 512
    mem = mem_prompt.reshape(bp * n_mem, d)
    tmm = min(512, bp * n_mem)
    assert (bp * n_mem) % tmm == 0
    dff = w_gate.shape[2]
    tf = 256 if dff % 256 == 0 else dff

    a_kp, a_vp, b_kp, b_vp, a_ks, a_vs, b_ks, b_vs = [], [], [], [], [], [], [], []
    c_kp, c_vp, c_lfp, c_ks, c_vs, c_lfs = [], [], [], [], [], []
    mem_kp, mem_vp = [], []
    for layer in range(depth):
        if layer % 2 == 0:
            e = layer // 2
            mix_p, (ka, va, kb, vb) = _layer_ab(xp, sp, g_mix[layer], w_in_ab[e], w_out_ab[e],
                                                rel_bias_a[e], None, tm=tmp)
            a_kp.append(ka); a_vp.append(va); b_kp.append(kb); b_vp.append(vb)
            mix_s, (ka, va, kb, vb) = _layer_ab(
                xs, ss, g_mix[layer], w_in_ab[e], w_out_ab[e], rel_bias_a[e],
                (cache_a_k[e], cache_a_v[e], cache_b_k[e], cache_b_v[e]), tm=tms)
            a_ks.append(ka); a_vs.append(va); b_ks.append(kb); b_vs.append(vb)
        else:
            c = layer // 2
            mix_p, (k, v, lf) = _layer_c(xp, sp, g_mix[layer], w_in_c[c], b_f_c[c], w_out_c[c],
                                         None, tm=tmp)
            c_kp.append(k); c_vp.append(v); c_lfp.append(lf)
            mix_s, (k, v, lf) = _layer_c(xs, ss, g_mix[layer], w_in_c[c], b_f_c[c], w_out_c[c],
                                         (cache_c_k[c], cache_c_v[c], cache_c_logf[c]), tm=tms)
            c_ks.append(k); c_vs.append(v); c_lfs.append(lf)
        mk, mk16, mv, mv16 = norm_proj(
            mem, g_mem[layer], [w_xk[layer].astype(BF16), w_xv[layer].astype(BF16)],
            [(0, 'heads', d // XA_HEADS), (0, 'bf16', 1.0), (1, 'heads', d // XA_HEADS),
             (1, 'bf16', 1.0)],
            seq=n_mem, tm=tmm)
        mem_kp.append(mk); mem_vp.append(mv)
        tail = functools.partial(
            layer_tail, g_x=g_xattn[layer], wq=w_xq[layer].astype(BF16),
            wo=w_xo[layer].astype(BF16), g_f=g_ffn[layer], wg=w_gate[layer].astype(BF16),
            wu=w_up[layer].astype(BF16), wd=w_down[layer].astype(BF16), g_final=g_final, tf=tf,
            final_norm=layer == depth - 1)
        xp = tail(*mix_p, xp, mk=mk16.reshape(bp, n_mem, d), mv=mv16.reshape(bp, n_mem, d),
                  seq=sp, tm=tmp)
        xs = tail(*mix_s, xs, mk=cache_mem_k[layer].reshape(bs, n_mem, d).astype(BF16),
                  mv=cache_mem_v[layer].reshape(bs, n_mem, d).astype(BF16), seq=ss, tm=tms)

    hd = HEAD_DIM
    xa_hd = d // XA_HEADS
    r5 = lambda lst, b, s, dd: jnp.stack([a.reshape(b, s, -1, dd) for a in lst])
    r4 = lambda lst, b, s: jnp.stack([a.reshape(b, s, -1) for a in lst])
    keep = min(A_PAST, sp)
    return (xp.reshape(bp, sp, d), xs.reshape(bs, ss, d),
            r5(a_kp, bp, keep, hd), r5(a_vp, bp, keep, hd), r5(b_kp, bp, sp, hd), r5(b_vp, bp, sp, hd),
            r5(c_kp, bp, sp, hd), r5(c_vp, bp, sp, hd), r4(c_lfp, bp, sp),
            r5(mem_kp, bp, n_mem, xa_hd), r5(mem_vp, bp, n_mem, xa_hd),
            r5(a_ks, bs, ss, hd), r5(a_vs, bs, ss, hd), r5(b_ks, bs, ss, hd), r5(b_vs, bs, ss, hd),
            r5(c_ks, bs, ss, hd), r5(c_vs, bs, ss, hd), r4(c_lfs, bs, ss))
```

```python
import functools

import jax
import jax.numpy as jnp
from jax import lax
from jax.experimental import pallas as pl
from jax.experimental.pallas import tpu as pltpu

F32 = jnp.float32
BF16 = jnp.bfloat16

RMS_EPS = 1e-6
NEG_INF = -1e30
LOG2E = 1.4426950408889634
HEAD_DIM = 64
CHUNK = 64
A_PAST_CHUNKS = 8
A_PAST = A_PAST_CHUNKS * CHUNK
REL_CLIP = 128
XA_HEADS = 4

LANES = 128
HEADS_PER_VREG = LANES // HEAD_DIM
KEY_CHUNK = 256
ATTN_PAIRS_PER_STEP = 4
VMEM_LIMIT = 56 * 1024 * 1024


def _cparams(*sem):
    return pltpu.CompilerParams(dimension_semantics=sem, vmem_limit_bytes=VMEM_LIMIT)


def _resident_spec(shape):
    return pl.BlockSpec(shape, lambda *_: (0,) * len(shape), pipeline_mode=pl.Buffered(1))


def _rms_bf16(x, g):
    y = x * lax.rsqrt(jnp.mean(x * x, axis=-1, keepdims=True) + RMS_EPS)
    return (y * g).astype(BF16)


def _log_sigmoid(z):
    return jnp.minimum(z, 0.0) - jnp.log1p(jnp.exp(-jnp.abs(z)))


def _split3(x):
    hi = x.astype(BF16)
    r = x - hi.astype(F32)
    mid = r.astype(BF16)
    lo = (r - mid.astype(F32)).astype(BF16)
    return hi, mid, lo


def _split2(x):
    hi = x.astype(BF16)
    lo = (x - hi.astype(F32)).astype(BF16)
    return hi, lo


def _dot(a, b):
    return jnp.dot(a, b, preferred_element_type=F32)


def _dot_nt(a, b):
    return lax.dot_general(a, b, (((1,), (1,)), ((), ())), preferred_element_type=F32)


def _norm_proj_kernel(x_ref, g_ref, b_ref, *refs, n_w, outs, tiles_per_seq):
    w_refs, o_refs = refs[:n_w], refs[n_w:]
    h = _rms_bf16(x_ref[...], g_ref[...])
    ys = {}
    for o_ref, (grp, kind, arg) in zip(o_refs, outs):
        if grp not in ys:
            ys[grp] = _dot(h, w_refs[grp][...])
        y = ys[grp]
        if kind == 'f32':
            o_ref[...] = y
        elif kind == 'bf16':
            o_ref[...] = (y * arg).astype(BF16)
        elif kind == 'logf':
            o_ref[...] = _log_sigmoid(y + b_ref[...])
        elif kind == 'heads':
            o_ref[...] = y.reshape(o_ref.shape)
        elif kind == 'rowmax':
            hd, scale = arg
            n_cols, n_heads = y.shape[1], y.shape[1] // hd
            sel_l = lax.broadcasted_iota(jnp.int32, (n_cols, n_heads), 0) // hd
            sel_h = lax.broadcasted_iota(jnp.int32, (n_cols, n_heads), 1)
            scaled = y * scale
            norms = jnp.sqrt(_dot((scaled * scaled).astype(BF16),
                                  jnp.where(sel_l == sel_h, 1.0, 0.0).astype(BF16)))
            for r in range(o_ref.shape[1]):
                o_ref[0, r:r + 1, :] = jnp.max(norms[r * KEY_CHUNK:(r + 1) * KEY_CHUNK],
                                               axis=0, keepdims=True)
        else:
            @pl.when(pl.program_id(0) % tiles_per_seq == tiles_per_seq - 1)
            def _(o_ref=o_ref, y=y):
                o_ref[...] = y.reshape(o_ref.shape)


def norm_proj(x, g, ws, outs, *, seq, tm, bias=None):
    m, d = x.shape
    tiles_per_seq = max(seq // tm, 1)
    if bias is None:
        bias = jnp.zeros((1, 16), F32)
    in_specs = [pl.BlockSpec((tm, d), lambda i: (i, 0)),
                pl.BlockSpec((1, d), lambda i: (0, 0)),
                pl.BlockSpec(bias.shape, lambda i: (0, 0))]
    in_specs += [_resident_spec(w.shape) for w in ws]
    out_shape, out_specs = [], []
    for grp, kind, arg in outs:
        n = ws[grp].shape[1]
        if kind == 'tail':
            assert tm == min(A_PAST, seq)
            out_shape.append(jax.ShapeDtypeStruct((m // tiles_per_seq, n // arg, arg), F32))
            out_specs.append(pl.BlockSpec((tm, n // arg, arg), lambda i: (i // tiles_per_seq, 0, 0)))
        elif kind == 'heads':
            out_shape.append(jax.ShapeDtypeStruct((m, n // arg, arg), F32))
            out_specs.append(pl.BlockSpec((tm, n // arg, arg), lambda i: (i, 0, 0)))
        elif kind == 'rowmax':
            assert tm % KEY_CHUNK == 0
            blocks = (tm // KEY_CHUNK, n // arg[0])
            out_shape.append(jax.ShapeDtypeStruct((m // tm,) + blocks, F32))
            out_specs.append(pl.BlockSpec((1,) + blocks, lambda i: (i, 0, 0)))
        else:
            out_shape.append(jax.ShapeDtypeStruct((m, n), BF16 if kind == 'bf16' else F32))
            out_specs.append(pl.BlockSpec((tm, n), lambda i: (i, 0)))
    kern = functools.partial(_norm_proj_kernel, n_w=len(ws), outs=tuple(outs),
                             tiles_per_seq=tiles_per_seq)
    return pl.pallas_call(
        kern, grid=(m // tm,), in_specs=in_specs, out_specs=out_specs, out_shape=out_shape,
        compiler_params=_cparams("arbitrary"), name="norm_proj")(x, g.reshape(1, d), bias, *ws)


BIAS_BLOCK_ROWS = 32


def _band_bias_kernel(rb_ref, off_ref, o_ref, *, n_real):
    n_heads, rows, cols = o_ref.shape[1:]
    off = off_ref[pl.program_id(0)]
    i0 = pl.program_id(1) * rows
    r0 = pl.program_id(2) * cols
    i = lax.broadcasted_iota(jnp.int32, (rows, cols), 0) + i0
    r = lax.broadcasted_iota(jnp.int32, (rows, cols), 1) + r0
    d = jnp.clip(off + i - r, -REL_CLIP, REL_CLIP) + REL_CLIP
    lo = jnp.clip(off + i0 - (r0 + cols - 1), -REL_CLIP, REL_CLIP) + REL_CLIP
    hi = jnp.clip(off + i0 + rows - 1 - r0, -REL_CLIP, REL_CLIP) + REL_CLIP
    first = (i // CHUNK - A_PAST_CHUNKS) * CHUNK + off
    last = jnp.minimum((i // CHUNK + 1) * CHUNK + off, n_real)
    visible = (r >= first) & (r < last)
    first0 = (i0 // CHUNK - A_PAST_CHUNKS) * CHUNK + off
    last0 = jnp.minimum(((i0 + rows - 1) // CHUNK + 1) * CHUNK + off, n_real)
    hi = jnp.where((r0 + cols <= first0) | (r0 >= last0), lo - 1, hi)

    def body(u, tbls):
        hit = d == u
        return tuple(jnp.where(hit, rb_ref[h, u], t) for h, t in enumerate(tbls))

    tbls = lax.fori_loop(lo, hi + 1, body,
                         tuple(jnp.zeros((rows, cols), F32) for _ in range(n_heads)))
    for h in range(n_heads):
        o_ref[0, h] = jnp.where(visible, LOG2E * tbls[h], NEG_INF)


def band_bias(rel_bias, offsets, *, rows, cols, n_real):
    n_rel, n_heads = rel_bias.shape
    blk = min(rows, BIAS_BLOCK_ROWS)
    return pl.pallas_call(
        functools.partial(_band_bias_kernel, n_real=n_real),
        grid=(len(offsets), rows // blk, cols // LANES),
        in_specs=[pl.BlockSpec(memory_space=pltpu.SMEM), pl.BlockSpec(memory_space=pltpu.SMEM)],
        out_specs=pl.BlockSpec((1, n_heads, blk, LANES), lambda o, t, c: (o, 0, t, c)),
        out_shape=jax.ShapeDtypeStruct((len(offsets), n_heads, rows, cols), F32),
        compiler_params=_cparams("arbitrary", "arbitrary", "arbitrary"),
        name="band_bias")(rel_bias.T, jnp.asarray(offsets, jnp.int32))


def _head_masks(shape):
    lane = lax.broadcasted_iota(jnp.int32, shape, len(shape) - 1)
    return [(lane % LANES) // HEAD_DIM == j for j in range(HEADS_PER_VREG)]


def _ones_lanes():
    return [((j + 1) % HEADS_PER_VREG) * HEAD_DIM for j in range(HEADS_PER_VREG)]


def _masked_values(v, j):
    lane = lax.broadcasted_iota(jnp.int32, v.shape, 1) % LANES
    vj = jnp.where(lane // HEAD_DIM == j, v, jnp.zeros((), BF16))
    return jnp.where(lane == _ones_lanes()[j], jnp.ones((), BF16), vj)


def _band_attn_kernel(q_ref, k_ref, v_ref, bias_ref, o_ref, v16, s_scr, p_scr, *, tq, band):
    step = pl.program_id(1)
    ck = KEY_CHUNK

    @pl.when(step == 0)
    def _():
        v = v_ref[0]
        for j in range(HEADS_PER_VREG):
            v16[j] = _masked_values(v, j)

    start = pl.multiple_of(jnp.maximum(step * (tq // CHUNK) - A_PAST_CHUNKS, 0) * CHUNK, CHUNK)
    qmask = _head_masks((tq, LANES))
    ones_lane = _ones_lanes()
    cols = [slice(c * ck, (c + 1) * ck) for c in range(band // ck)]
    heads = [(hp, j) for hp in range(q_ref.shape[2] // LANES) for j in range(HEADS_PER_VREG)]
    lanes = [slice(hp * LANES, (hp + 1) * LANES) for hp, _ in heads]
    qs = [jnp.where(qmask[j], q_ref[0, :, lanes[n]], jnp.zeros((), BF16))
          for n, (_, j) in enumerate(heads)]

    def logits(n, c, mrun):
        k = k_ref[0, pl.ds(pl.multiple_of(start + c * ck, CHUNK), ck), lanes[n]]
        s = _dot_nt(qs[n], k) + bias_ref[0, n, :, cols[c]]
        s_scr[n % 2, :, cols[c]] = s
        for part in range(ck // LANES):
            mrun = jnp.maximum(mrun, s[:, part * LANES:(part + 1) * LANES])
        return mrun

    def probs(n, c, row_max):
        p_scr[n % 2, :, cols[c]] = jnp.exp2(s_scr[n % 2, :, cols[c]] - row_max).astype(BF16)

    def pv(n):
        return _dot(p_scr[n % 2], v16[heads[n][1], pl.ds(start, band), lanes[n]])

    neg = jnp.full((tq, LANES), NEG_INF, F32)
    outs = []
    mrun = functools.reduce(lambda m, c: logits(0, c, m), range(len(cols)), neg)
    for n in range(len(heads)):
        row_max = jnp.max(mrun, axis=-1, keepdims=True)
        mrun = neg
        for c in range(len(cols)):
            if n + 1 < len(heads):
                mrun = logits(n + 1, c, mrun)
            probs(n, c, row_max)
        o = pv(n)
        lane = ones_lane[heads[n][1]]
        outs.append(o / o[:, lane:lane + 1])
    for n in range(0, len(heads), HEADS_PER_VREG):
        o_ref[0, :, lanes[n]] = jnp.where(qmask[0], outs[n], outs[n + 1]).astype(BF16)


def band_attn(q, k, v, bias, *, tq, band):
    b, sq, w = q.shape
    sk = k.shape[1]
    n_off, n_heads = bias.shape[:2]
    assert band % KEY_CHUNK == 0 and tq % CHUNK == 0 or sq == tq
    kern = functools.partial(_band_attn_kernel, tq=tq, band=band)
    return pl.pallas_call(
        kern, grid=(b, sq // tq),
        in_specs=[pl.BlockSpec((1, tq, w), lambda i, c: (i, c, 0)),
                  pl.BlockSpec((1, sk, w), lambda i, c: (i, 0, 0)),
                  pl.BlockSpec((1, sk, w), lambda i, c: (i, 0, 0)),
                  pl.BlockSpec((1, n_heads, tq, band),
                               lambda i, c: (jnp.minimum(c, n_off - 1), 0, 0, 0))],
        out_specs=pl.BlockSpec((1, tq, w), lambda i, c: (i, c, 0)),
        out_shape=jax.ShapeDtypeStruct((b, sq, w), BF16),
        scratch_shapes=[pltpu.VMEM((HEADS_PER_VREG, sk, w), BF16),
                        pltpu.VMEM((2, tq, band), F32),
                        pltpu.VMEM((2, tq, band), BF16)],
        compiler_params=_cparams("parallel", "arbitrary"), name="band_attn")(q, k, v, bias)


STICK_UNDERFLOW_LOG2 = 160.0
STICK_NEAR_CHUNKS = 2


def _stick_kernel(q_ref, k_ref, v_ref, o_ref, v16, zl_scr, hl_scr, w_scr, *,
                  tq, q_offset, variants):
    qi = pl.program_id(2)
    ck = KEY_CHUNK
    heads = [(pp, j) for pp in range(q_ref.shape[2] // LANES) for j in range(HEADS_PER_VREG)]
    n_heads = len(heads)
    lanes = [slice(pp * LANES, (pp + 1) * LANES) for pp, _ in heads]

    @pl.when(qi == 0)
    def _():
        v = v_ref[0]
        vmask = _head_masks(v.shape)
        for j in range(HEADS_PER_VREG):
            v16[j] = jnp.where(vmask[j], v, jnp.zeros((), BF16))

    qmask = _head_masks((tq, LANES))
    qs = [jnp.where(qmask[j], q_ref[0, :, lanes[n]], jnp.zeros((), BF16))
          for n, (_, j) in enumerate(heads)]
    q_pos = q_offset + qi * tq + lax.broadcasted_iota(jnp.int32, (tq, ck), 0)
    k_iota = lax.broadcasted_iota(jnp.int32, (tq, ck), 1)
    rr = lax.broadcasted_iota(jnp.int32, (2 * ck, ck), 0) % ck
    cc = lax.broadcasted_iota(jnp.int32, (2 * ck, ck), 1)
    tri2 = jnp.where(rr > cc, 1.0, 0.0).astype(BF16)
    n_chunks = (q_offset + (qi + 1) * tq + ck - 1) // ck

    def split_cols(slot):
        return slice(2 * slot.start, 2 * slot.start + ck), slice(2 * slot.start + ck, 2 * slot.stop)

    def logits(j, slot, c, masked):
        k0 = pl.multiple_of(c * ck, ck)
        z = _dot_nt(qs[j], k_ref[0, pl.ds(k0, ck), lanes[j]])
        neg_abs = lax.bitcast_convert_type(
            lax.bitcast_convert_type(z, jnp.int32) | jnp.int32(-2 ** 31), F32)
        sp = jnp.maximum(z, 0.0) + jnp.log2(1.0 + jnp.exp2(neg_abs))
        if masked:
            sp = jnp.where(k0 + k_iota < q_pos, sp, 0.0)
        zl_scr[j, :, slot] = z - sp
        hi_cols, lo_cols = split_cols(slot)
        hl_scr[j, :, hi_cols], hl_scr[j, :, lo_cols] = _split2(sp)
        return jnp.sum(sp, axis=-1, keepdims=True)

    def weights(j, slot, c, later, masked):
        sums = _dot(hl_scr[j, :, 2 * slot.start:2 * slot.stop], tri2)
        w = jnp.exp2(zl_scr[j, :, slot] - sums - later)
        if masked:
            w = jnp.where(c * ck + k_iota < q_pos, w, 0.0)
        w_scr[j, :, slot] = w.astype(BF16)

    def more(laters):
        return (jnp.min(functools.reduce(jnp.minimum, laters))
                < STICK_UNDERFLOW_LOG2).astype(jnp.int32)

    def attend(near, n_masked):
        slots = [slice((near - 1 - i) * ck, (near - i) * ck) for i in range(near)]
        chunk = [n_chunks - 1 - i for i in range(near)]
        k0 = pl.multiple_of((n_chunks - near) * ck, ck)

        def pv(n, k0, width):
            return _dot(w_scr[n, :, :width], v16[heads[n][1], pl.ds(k0, width), lanes[n]])

        laters, accs = [], []
        row_sums = [logits(0, slots[i], chunk[i], i < n_masked) for i in range(near)]
        for n in range(n_heads):
            later = jnp.zeros((tq, 1), F32)
            next_sums = []
            for i in range(near):
                if n + 1 < n_heads:
                    next_sums.append(logits(n + 1, slots[i], chunk[i], i < n_masked))
                weights(n, slots[i], chunk[i], later, i < n_masked)
                later = later + row_sums[i]
            laters.append(later)
            row_sums = next_sums
            o = pv(n, k0, near * ck)
            if heads[n][1] == 0:
                accs.append(o)
            else:
                accs[-1] = accs[-1] + o

        def cond(carry):
            return (carry[0] >= 0) & (carry[1] > 0)

        def body(carry):
            c, _, accs, laters = carry
            accs = list(accs)
            k0 = pl.multiple_of(c * ck, ck)
            new = []
            for n in range(n_heads):
                rs = logits(n, slice(0, ck), c, False)
                weights(n, slice(0, ck), c, laters[n], False)
                accs[heads[n][0]] = accs[heads[n][0]] + pv(n, k0, ck)
                new.append(laters[n] + rs)
            return c - 1, more(new), tuple(accs), tuple(new)

        carry = lax.while_loop(cond, body,
                               (n_chunks - 1 - near, more(laters), tuple(accs), tuple(laters)))
        for pp, acc in enumerate(carry[2]):
            o_ref[0, :, pp * LANES:(pp + 1) * LANES] = acc.astype(BF16)

    for n_total, near, n_masked in variants:
        if n_total is None:
            pl.when(n_chunks >= near)(functools.partial(attend, near, n_masked))
        else:
            pl.when(n_chunks == n_total)(functools.partial(attend, near, n_masked))


def stick_attn(q, k, v, *, tq, q_offset):
    b, sq, w = q.shape
    sk = k.shape[1]
    ck = KEY_CHUNK
    assert q_offset % ck == 0 and (tq % ck == 0 or sq == tq <= ck) and sk % ck == 0
    n_masked = -(-tq // ck)
    totals = sorted({-(-(q_offset + (t + 1) * tq) // ck) for t in range(sq // tq)})
    variants = [(n, n, min(n_masked, n)) for n in totals if n < STICK_NEAR_CHUNKS]
    if totals[-1] >= STICK_NEAR_CHUNKS:
        variants.append((None, STICK_NEAR_CHUNKS, n_masked))
    near_cols = STICK_NEAR_CHUNKS * ck
    wg = ATTN_PAIRS_PER_STEP * LANES
    n_heads = ATTN_PAIRS_PER_STEP * HEADS_PER_VREG
    kern = functools.partial(_stick_kernel, tq=tq, q_offset=q_offset, variants=tuple(variants))
    return pl.pallas_call(
        kern, grid=(b, w // wg, sq // tq),
        in_specs=[pl.BlockSpec((1, tq, wg), lambda i, h, t: (i, t, h)),
                  pl.BlockSpec((1, sk, wg), lambda i, h, t: (i, 0, h)),
                  pl.BlockSpec((1, sk, wg), lambda i, h, t: (i, 0, h))],
        out_specs=pl.BlockSpec((1, tq, wg), lambda i, h, t: (i, t, h)),
        out_shape=jax.ShapeDtypeStruct((b, sq, w), BF16),
        scratch_shapes=[pltpu.VMEM((HEADS_PER_VREG, sk, wg), BF16),
                        pltpu.VMEM((n_heads, tq, near_cols), F32),
                        pltpu.VMEM((n_heads, tq, 2 * near_cols), BF16),
                        pltpu.VMEM((n_heads, tq, near_cols), BF16)],
        compiler_params=_cparams("parallel", "parallel", "arbitrary"), name="stick_attn")(q, k, v)


def _cumsum_kernel(lf_ref, lft_ref, col_ref, row_ref, *, blk):
    s_len, n_heads = lf_ref.shape[1], lf_ref.shape[2]
    rr = lax.broadcasted_iota(jnp.int32, (blk, blk), 0)
    cc = lax.broadcasted_iota(jnp.int32, (blk, blk), 1)
    lower = jnp.where(rr >= cc, 1.0, 0.0).astype(BF16)
    upper = jnp.where(rr <= cc, 1.0, 0.0).astype(BF16)
    blocks = [slice(n * blk, (n + 1) * blk) for n in range(s_len // blk)]
    local_col = [sum(_dot(lower, p) for p in _split3(lf_ref[0, rows, :])) for rows in blocks]
    local_row = [sum(_dot(p, upper) for p in _split3(lft_ref[0, :, rows])) for rows in blocks]
    carry_col = jnp.zeros((1, n_heads), F32)
    carry_row = jnp.zeros((n_heads, 1), F32)
    for rows, c, ct in zip(blocks, local_col, local_row):
        col_ref[0, rows, :] = c + carry_col
        row_ref[0, :, rows] = ct + carry_row
        carry_col = carry_col + c[blk - 1:blk, :]
        carry_row = carry_row + ct[:, blk - 1:blk]


def cumsum_logf(lf):
    b, s_len, n_heads = lf.shape
    kern = functools.partial(_cumsum_kernel, blk=LANES)
    return pl.pallas_call(
        kern, grid=(b,),
        in_specs=[pl.BlockSpec((1, s_len, n_heads), lambda i: (i, 0, 0)),
                  pl.BlockSpec((1, n_heads, s_len), lambda i: (i, 0, 0))],
        out_specs=[pl.BlockSpec((1, s_len, n_heads), lambda i: (i, 0, 0)),
                   pl.BlockSpec((1, n_heads, s_len), lambda i: (i, 0, 0))],
        out_shape=[jax.ShapeDtypeStruct((b, s_len, n_heads), F32),
                   jax.ShapeDtypeStruct((b, n_heads, s_len), F32)],
        compiler_params=_cparams("parallel"), name="cumsum_logf")(lf, jnp.swapaxes(lf, 1, 2))


FORGET_SKIP_LOG2 = 152.0
FORGET_BOUND_SLACK = 1.02
FORGET_BOUND_MARGIN = 2.0


def _forget_skip_kernel(qmax_ref, kmax_ref, cum_ref, o_ref, *, heads_per_group):
    n_blk, n_heads = kmax_ref.shape[1], kmax_ref.shape[2]
    blk = cum_ref.shape[1] // n_blk
    kmax = kmax_ref[0]
    k_term = kmax + jnp.max(kmax, axis=0, keepdims=True)
    cum_end = cum_ref[0, pl.ds(blk - 1, n_blk, stride=blk), :]
    cum_start = cum_ref[0, pl.ds(0, n_blk, stride=blk), :]
    chunk_id = lax.broadcasted_iota(jnp.int32, (n_blk, 1), 0)
    head_id = lax.broadcasted_iota(jnp.int32, (n_blk, n_heads), 1)
    row = lax.broadcasted_iota(jnp.int32, o_ref.shape[1:], 0)
    col = lax.broadcasted_iota(jnp.int32, o_ref.shape[1:], 1)
    table = jnp.zeros(o_ref.shape[1:], F32)
    for qi in range(n_blk):
        bound = (FORGET_BOUND_SLACK * qmax_ref[0, qi:qi + 1, :] * k_term
                 + LOG2E * (cum_start[qi:qi + 1, :] - cum_end) + FORGET_BOUND_MARGIN)
        for g in range(n_heads // heads_per_group):
            worst = jnp.max(jnp.where(head_id // heads_per_group == g, bound, NEG_INF),
                            axis=1, keepdims=True)
            needed = jnp.logical_not(worst < -FORGET_SKIP_LOG2) | (chunk_id >= qi)
            first = jnp.min(jnp.where(needed, chunk_id, n_blk).astype(F32), axis=0, keepdims=True)
            table = jnp.where((row == qi) & (col == g), first, table)
    o_ref[0] = table.astype(jnp.int32)


def forget_skip_table(qmax, kmax, cum, *, heads_per_group):
    b, n_blk, n_heads = kmax.shape
    s_len = cum.shape[1]
    blk3 = lambda shape: pl.BlockSpec((1,) + shape, lambda i: (i, 0, 0))
    return pl.pallas_call(
        functools.partial(_forget_skip_kernel, heads_per_group=heads_per_group), grid=(b,),
        in_specs=[blk3((n_blk, n_heads)), blk3((n_blk, n_heads)), blk3((s_len, n_heads))],
        out_specs=blk3((n_blk, LANES)),
        out_shape=jax.ShapeDtypeStruct((b, n_blk, LANES), jnp.int32),
        compiler_params=_cparams("parallel"), name="forget_skip_table")(qmax, kmax, cum)


def _forget_kernel(c0_ref, q_ref, k_ref, v_ref, cq_ref, ck_ref, o_ref, v16, s_scr, p_scr, *,
                   tq, q_offset, max_chunks):
    group = pl.program_id(1)
    qi = pl.program_id(2)
    wg = k_ref.shape[2]
    ck_w = KEY_CHUNK
    ones_lane = _ones_lanes()
    heads = [(pp, j) for pp in range(wg // LANES) for j in range(HEADS_PER_VREG)]
    lanes = [slice(pp * LANES, (pp + 1) * LANES) for pp, _ in heads]

    @pl.when(qi == 0)
    def _():
        v = v_ref[0]
        for j in range(HEADS_PER_VREG):
            v16[j] = _masked_values(v, j)

    qmask = _head_masks((tq, LANES))
    q_pos = q_offset + qi * tq + lax.broadcasted_iota(jnp.int32, (tq, ck_w), 0)
    k_iota = lax.broadcasted_iota(jnp.int32, (tq, ck_w), 1)
    cq_all = cq_ref[0]
    head_lane = lax.broadcasted_iota(jnp.int32, cq_all.shape, 1)
    n_c = (q_offset + (qi + 1) * tq + ck_w - 1) // ck_w
    step = (pl.program_id(0) * pl.num_programs(1) + group) * pl.num_programs(2) + qi
    c0 = jnp.minimum(c0_ref[step], n_c - 1)
    n_proc_here = n_c - c0

    def attend(n_proc):
        cols = [slice(c * ck_w, (c + 1) * ck_w) for c in range(n_proc)]
        k0 = [pl.multiple_of((c0 + c) * ck_w, ck_w) for c in range(n_proc)]
        allowed = k0[-1] + k_iota <= q_pos
        qs = [jnp.where(qmask[j], q_ref[0, :, lanes[n]], jnp.zeros((), BF16))
              for n, (_, j) in enumerate(heads)]
        cq = [LOG2E * jnp.sum(jnp.where(head_lane == group * len(heads) + n, cq_all, 0.0),
                              axis=-1, keepdims=True) for n in range(len(heads))]

        def logits(n, c, mrun):
            s = (_dot_nt(qs[n], k_ref[0, pl.ds(k0[c], ck_w), lanes[n]])
                 - LOG2E * ck_ref[0, 0, n:n + 1, pl.ds(k0[c], ck_w)])
            if c == n_proc - 1:
                s = jnp.where(allowed, s, NEG_INF)
            s_scr[n % 2, :, cols[c]] = s
            for part in range(ck_w // LANES):
                mrun = jnp.maximum(mrun, s[:, part * LANES:(part + 1) * LANES])
            return mrun

        def row_term(n, mrun):
            m = jnp.max(mrun, axis=-1, keepdims=True) + cq[n]
            return cq[n] - m

        def probs(n, c, row):
            p_scr[n % 2, :, cols[c]] = jnp.exp2(s_scr[n % 2, :, cols[c]] + row).astype(BF16)

        def pv(n):
            return _dot(p_scr[n % 2, :, :n_proc * ck_w],
                        v16[heads[n][1], pl.ds(k0[0], n_proc * ck_w), lanes[n]])

        neg = jnp.full((tq, LANES), NEG_INF, F32)
        outs = []
        mrun = functools.reduce(lambda m, c: logits(0, c, m), range(n_proc), neg)
        for n in range(len(heads)):
            row = row_term(n, mrun)
            mrun = neg
            for c in range(n_proc):
                if n + 1 < len(heads):
                    mrun = logits(n + 1, c, mrun)
                probs(n, c, row)
            o = pv(n)
            lane = ones_lane[heads[n][1]]
            outs.append(o / o[:, lane:lane + 1])
        for n in range(0, len(heads), HEADS_PER_VREG):
            o_ref[0, :, lanes[n]] = jnp.where(qmask[0], outs[n], outs[n + 1]).astype(BF16)

    for n_proc in range(1, max_chunks + 1):
        pl.when(n_proc_here == n_proc)(functools.partial(attend, n_proc))


def forget_attn(q, k, v, cum_q, cum_k_rows, skip, *, tq, q_offset):
    b, sq, w = q.shape
    sk = k.shape[1]
    n_heads = cum_q.shape[2]
    ck_w = KEY_CHUNK
    diag = [-(-(q_offset + (t + 1) * tq) // ck_w) - 1 for t in range(sq // tq)]
    assert sk % ck_w == 0 and all((q_offset + t * tq) // ck_w == c for t, c in enumerate(diag))
    max_chunks = max(diag) + 1
    wg = ATTN_PAIRS_PER_STEP * LANES
    heads_per_step = ATTN_PAIRS_PER_STEP * HEADS_PER_VREG
    assert skip.shape == (b, w // wg, sq // tq)
    ck = cum_k_rows.reshape(b, n_heads // heads_per_step, heads_per_step, sk)
    kern = functools.partial(_forget_kernel, tq=tq, q_offset=q_offset, max_chunks=max_chunks)
    grid_spec = pltpu.PrefetchScalarGridSpec(
        num_scalar_prefetch=1, grid=(b, w // wg, sq // tq),
        in_specs=[pl.BlockSpec((1, tq, wg), lambda i, h, t, c0: (i, t, h)),
                  pl.BlockSpec((1, sk, wg), lambda i, h, t, c0: (i, 0, h)),
                  pl.BlockSpec((1, sk, wg), lambda i, h, t, c0: (i, 0, h)),
                  pl.BlockSpec((1, tq, n_heads), lambda i, h, t, c0: (i, t, 0)),
                  pl.BlockSpec((1, 1, heads_per_step, sk), lambda i, h, t, c0: (i, h, 0, 0))],
        out_specs=pl.BlockSpec((1, tq, wg), lambda i, h, t, c0: (i, t, h)),
        scratch_shapes=[pltpu.VMEM((HEADS_PER_VREG, sk, wg), BF16),
                        pltpu.VMEM((2, tq, max_chunks * ck_w), F32),
                        pltpu.VMEM((2, tq, max_chunks * ck_w), BF16)])
    return pl.pallas_call(
        kern, grid_spec=grid_spec, out_shape=jax.ShapeDtypeStruct((b, sq, w), BF16),
        compiler_params=_cparams("parallel", "parallel", "arbitrary"),
        name="forget_attn")(skip.reshape(-1), q, k, v, cum_q, ck)


def _layer_tail_kernel(*refs, n_in, n_heads, rows_per_seq, tf, final_norm):
    a_refs, w_refs = refs[:n_in], refs[n_in:2 * n_in]
    (x_ref, gx_ref, wq_ref, wo_ref, mk_ref, mv_ref, gf_ref, wg_ref, wu_ref, wd_ref, gfin_ref,
     o_ref) = refs[2 * n_in:]
    x = x_ref[...]
    for a_ref, w_ref in zip(a_refs, w_refs):
        x = x + _dot(a_ref[...], w_ref[...])

    hd = x.shape[1] // n_heads
    q = (_dot(_rms_bf16(x, gx_ref[...]), wq_ref[...]) * (hd ** -0.5)).astype(BF16)
    per_seq = []
    for b in range(mk_ref.shape[0]):
        rows = slice(b * rows_per_seq, (b + 1) * rows_per_seq)
        outs = []
        for j in range(n_heads):
            cols = slice(j * hd, (j + 1) * hd)
            s = _dot_nt(q[rows, cols], mk_ref[b, :, cols])
            p = jnp.exp(s - jnp.max(s, axis=-1, keepdims=True))
            p = p / jnp.sum(p, axis=-1, keepdims=True)
            outs.append(_dot(p.astype(BF16), mv_ref[b, :, cols]).astype(BF16))
        per_seq.append(jnp.concatenate(outs, axis=-1))
    x = x + _dot(jnp.concatenate(per_seq, axis=0), wo_ref[...])

    h = _rms_bf16(x, gf_ref[...])
    for c in range(wg_ref.shape[1] // tf):
        cols = slice(c * tf, (c + 1) * tf)
        gate = _dot(h, wg_ref[:, cols])
        up = _dot(h, wu_ref[:, cols])
        a = (gate * jax.nn.sigmoid(gate) * up).astype(BF16)
        x = x + _dot(a, wd_ref[cols, :])
    if final_norm:
        x = x * lax.rsqrt(jnp.mean(x * x, axis=-1, keepdims=True) + RMS_EPS) * gfin_ref[...]
    o_ref[...] = x


def layer_tail(a_list, w_list, x, g_x, wq, wo, mk, mv, g_f, wg, wu, wd, g_final, *, seq, tm, tf,
               final_norm):
    m, d = x.shape
    n_mem = mk.shape[1]
    rows_per_seq = min(tm, seq)
    seqs = tm // rows_per_seq
    assert seq % rows_per_seq == 0 and m % tm == 0
    row = lambda i: (i, 0)
    mem = lambda i: (i * tm // (seq * seqs), 0, 0)
    gain = lambda g: g.reshape(1, d)
    kern = functools.partial(_layer_tail_kernel, n_in=len(a_list), n_heads=XA_HEADS,
                             rows_per_seq=rows_per_seq, tf=tf, final_norm=final_norm)
    in_specs = ([pl.BlockSpec((tm, a.shape[1]), row) for a in a_list]
                + [_resident_spec(w.shape) for w in w_list]
                + [pl.BlockSpec((tm, d), row), _resident_spec((1, d)),
                   _resident_spec(wq.shape), _resident_spec(wo.shape),
                   pl.BlockSpec((seqs, n_mem, d), mem), pl.BlockSpec((seqs, n_mem, d), mem),
                   _resident_spec((1, d)), _resident_spec(wg.shape), _resident_spec(wu.shape),
                   _resident_spec(wd.shape), _resident_spec((1, d))])
    return pl.pallas_call(
        kern, grid=(m // tm,), in_specs=in_specs,
        out_specs=pl.BlockSpec((tm, d), row), out_shape=jax.ShapeDtypeStruct((m, d), F32),
        compiler_params=_cparams("parallel"), name="layer_tail")(
            *a_list, *w_list, x, gain(g_x), wq, wo, mk, mv, gain(g_f), wg, wu, wd, gain(g_final))


def _pad_rows(a, rows):
    return jnp.pad(a, ((0, 0), (0, rows - a.shape[1]), (0, 0)))


def _row_tile(m, seq, cap):
    tm = min(cap, seq)
    assert seq % tm == 0 and m % tm == 0
    return tm


def _layer_ab(x, seq, g, w_in, w_out, rel_bias, cache, *, tm):
    m, d = x.shape
    b = m // seq
    wa = w_in.shape[1] // 6
    ws = [w_in[:, n * wa:(n + 1) * wa].astype(BF16) for n in range(6)]
    scale = HEAD_DIM ** -0.5 * LOG2E
    kept = 'tail' if min(A_PAST, seq) < seq else 'heads'
    outs = [(4, 'heads', HEAD_DIM), (4, 'bf16', 1.0), (5, 'heads', HEAD_DIM), (5, 'bf16', 1.0),
            (0, 'bf16', scale), (3, 'bf16', scale), (1, 'bf16', 1.0), (2, 'bf16', 1.0),
            (1, kept, HEAD_DIM), (2, kept, HEAD_DIM)]
    kb, kb16, vb, vb16, qa, qb, ka16, va16, ka_keep, va_keep = norm_proj(x, g, ws, outs, seq=seq,
                                                                         tm=tm)
    shp = lambda a: a.reshape(b, -1, a.shape[-1])
    if cache is None:
        tq = 4 * CHUNK
        band = -(-(A_PAST + tq) // KEY_CHUNK) * KEY_CHUNK
        offsets = sorted({t * tq - max(t * (tq // CHUNK) - A_PAST_CHUNKS, 0) * CHUNK
                          for t in range(seq // tq)})
        bias = band_bias(rel_bias, offsets, rows=tq, cols=band, n_real=band)
        oa = band_attn(shp(qa), shp(ka16), shp(va16), bias, tq=tq, band=band)
        ob = stick_attn(shp(qb), shp(kb16), shp(vb16), tq=256, q_offset=0)
    else:
        ca_k, ca_v, cb_k, cb_v = cache
        n_past = ca_k.shape[1]
        n_keys = n_past + seq
        band = -(-n_keys // KEY_CHUNK) * KEY_CHUNK
        flat = lambda a: a.reshape(a.shape[0], a.shape[1], -1)
        k_all = _pad_rows(jnp.concatenate([flat(ca_k).astype(BF16), shp(ka16)], axis=1), band)
        v_all = _pad_rows(jnp.concatenate([flat(ca_v).astype(BF16), shp(va16)], axis=1), band)
        bias = band_bias(rel_bias, [n_past], rows=seq, cols=band, n_real=n_keys)
        oa = band_attn(shp(qa), k_all, v_all, bias, tq=seq, band=band)
        n_pastb = cb_k.shape[1]
        sk = -(-(n_pastb + seq) // KEY_CHUNK) * KEY_CHUNK
        kb_all = _pad_rows(jnp.concatenate([flat(cb_k).astype(BF16), shp(kb16)], axis=1), sk)
        vb_all = _pad_rows(jnp.concatenate([flat(cb_v).astype(BF16), shp(vb16)], axis=1), sk)
        ob = stick_attn(shp(qb), kb_all, vb_all, tq=seq, q_offset=n_pastb)
    w_out = w_out.astype(BF16)
    wo_a, wo_b = w_out[:oa.shape[-1]], w_out[oa.shape[-1]:]
    return ([oa.reshape(m, -1), ob.reshape(m, -1)], [wo_a, wo_b]), (ka_keep, va_keep, kb, vb)


def _layer_c(x, seq, g, w_in, b_f, w_out, cache, *, tm):
    m, d = x.shape
    b = m // seq
    n_heads = b_f.shape[0]
    wq, wk, wv, wf = (w_in[:, :d], w_in[:, d:2 * d], w_in[:, 2 * d:3 * d], w_in[:, 3 * d:])
    ws = [w.astype(BF16) for w in (wq, wk, wv, wf)]
    scale = HEAD_DIM ** -0.5 * LOG2E
    outs = [(1, 'heads', HEAD_DIM), (1, 'bf16', 1.0), (2, 'heads', HEAD_DIM), (2, 'bf16', 1.0),
            (0, 'bf16', scale), (3, 'logf', None)]
    if cache is None:
        outs += [(1, 'rowmax', (HEAD_DIM, 1.0)), (0, 'rowmax', (HEAD_DIM, scale))]
    k, k16, v, v16, q, lf, *norms = norm_proj(x, g, ws, outs, seq=seq, tm=tm,
                                              bias=b_f.reshape(1, n_heads))
    norms = norms[::-1]
    shp = lambda a: a.reshape(b, -1, a.shape[-1])
    heads_per_group = ATTN_PAIRS_PER_STEP * HEADS_PER_VREG
    n_groups = n_heads // heads_per_group
    if cache is None:
        cum_col, cum_row = cumsum_logf(shp(lf))
        tq = KEY_CHUNK
        qmax, kmax = (a.reshape(b, seq // KEY_CHUNK, n_heads) for a in norms)
        skip = forget_skip_table(qmax, kmax, cum_col, heads_per_group=heads_per_group)
        skip = jnp.swapaxes(skip[:, :, :n_groups], 1, 2)
        o = forget_attn(shp(q), shp(k16), shp(v16), cum_col, cum_row, skip, tq=tq, q_offset=0)
    else:
        c_k, c_v, c_lf = cache
        n_past = c_k.shape[1]
        sk = -(-(n_past + seq) // KEY_CHUNK) * KEY_CHUNK
        flat = lambda a: a.reshape(a.shape[0], a.shape[1], -1)
        k_all = _pad_rows(jnp.concatenate([flat(c_k).astype(BF16), shp(k16)], axis=1), sk)
        v_all = _pad_rows(jnp.concatenate([flat(c_v).astype(BF16), shp(v16)], axis=1), sk)
        lf_all = _pad_rows(jnp.concatenate([c_lf, shp(lf)], axis=1), sk)
        cum_col, cum_row = cumsum_logf(lf_all)
        o = forget_attn(shp(q), k_all, v_all, cum_col[:, n_past:n_past + seq], cum_row,
                        jnp.zeros((b, n_groups, 1), jnp.int32), tq=seq, q_offset=n_past)
    return ([o.reshape(m, d)], [w_out.astype(BF16)]), (k, v, lf)


def kernel(x_prompt, x_sample, cache_a_k, cache_a_v, cache_b_k, cache_b_v, cache_c_k, cache_c_v, cache_c_logf, cache_mem_k, cache_mem_v, mem_prompt, w_in_ab, w_out_ab, rel_bias_a, w_in_c, b_f_c, w_out_c, g_mix, g_xattn, g_mem, w_xq, w_xk, w_xv, w_xo, g_ffn, w_gate, w_up, w_down, g_final):
    bp, sp, d = x_prompt.shape
    bs, ss, _ = x_sample.shape
    depth = g_mix.shape[0]
    n_mem = mem_prompt.shape[1]
    xp = x_prompt.reshape(bp * sp, d)
    xs = x_sample.reshape(bs * ss, d)
    tmp = _row_tile(bp * sp, sp, 512)
    tms = bs * ss
    assert tms <= 512
    mem = mem_prompt.reshape(bp * n_mem, d)
    tmm = min(512, bp * n_mem)
    assert (bp * n_mem) % tmm == 0
    dff = w_gate.shape[2]
    tf = 256 if dff % 256 == 0 else dff

    a_kp, a_vp, b_kp, b_vp, a_ks, a_vs, b_ks, b_vs = [], [], [], [], [], [], [], []
    c_kp, c_vp, c_lfp, c_ks, c_vs, c_lfs = [], [], [], [], [], []
    mem_kp, mem_vp = [], []
    for layer in range(depth):
        if layer % 2 == 0:
            e = layer // 2
            mix_p, (ka, va, kb, vb) = _layer_ab(xp, sp, g_mix[layer], w_in_ab[e], w_out_ab[e],
                                                rel_bias_a[e], None, tm=tmp)
            a_kp.append(ka); a_vp.append(va); b_kp.append(kb); b_vp.append(vb)
            mix_s, (ka, va, kb, vb) = _layer_ab(
                xs, ss, g_mix[layer], w_in_ab[e], w_out_ab[e], rel_bias_a[e],
                (cache_a_k[e], cache_a_v[e], cache_b_k[e], cache_b_v[e]), tm=tms)
            a_ks.append(ka); a_vs.append(va); b_ks.append(kb); b_vs.append(vb)
        else:
            c = layer // 2
            mix_p, (k, v, lf) = _layer_c(xp, sp, g_mix[layer], w_in_c[c], b_f_c[c], w_out_c[c],
                                         None, tm=tmp)
            c_kp.append(k); c_vp.append(v); c_lfp.append(lf)
            mix_s, (k, v, lf) = _layer_c(xs, ss, g_mix[layer], w_in_c[c], b_f_c[c], w_out_c[c],
                                         (cache_c_k[c], cache_c_v[c], cache_c_logf[c]), tm=tms)
            c_ks.append(k); c_vs.append(v); c_lfs.append(lf)
        mk, mk16, mv, mv16 = norm_proj(
            mem, g_mem[layer], [w_xk[layer].astype(BF16), w_xv[layer].astype(BF16)],
            [(0, 'heads', d // XA_HEADS), (0, 'bf16', 1.0), (1, 'heads', d // XA_HEADS),
             (1, 'bf16', 1.0)],
            seq=n_mem, tm=tmm)
        mem_kp.append(mk); mem_vp.append(mv)
        tail = functools.partial(
            layer_tail, g_x=g_xattn[layer], wq=w_xq[layer].astype(BF16),
            wo=w_xo[layer].astype(BF16), g_f=g_ffn[layer], wg=w_gate[layer].astype(BF16),
            wu=w_up[layer].astype(BF16), wd=w_down[layer].astype(BF16), g_final=g_final, tf=tf,
            final_norm=layer == depth - 1)
        xp = tail(*mix_p, xp, mk=mk16.reshape(bp, n_mem, d), mv=mv16.reshape(bp, n_mem, d),
                  seq=sp, tm=tmp)
        xs = tail(*mix_s, xs, mk=cache_mem_k[layer].reshape(bs, n_mem, d).astype(BF16),
                  mv=cache_mem_v[layer].reshape(bs, n_mem, d).astype(BF16), seq=ss, tm=tms)

    hd = HEAD_DIM
    xa_hd = d // XA_HEADS
    r5 = lambda lst, b, s, dd: jnp.stack([a.reshape(b, s, -1, dd) for a in lst])
    r4 = lambda lst, b, s: jnp.stack([a.reshape(b, s, -1) for a in lst])
    keep = min(A_PAST, sp)
    return (xp.reshape(bp, sp, d), xs.reshape(bs, ss, d),
            r5(a_kp, bp, keep, hd), r5(a_vp, bp, keep, hd), r5(b_kp, bp, sp, hd), r5(b_vp, bp, sp, hd),
            r5(c_kp, bp, sp, hd), r5(c_vp, bp, sp, hd), r4(c_lfp, bp, sp),
            r5(mem_kp, bp, n_mem, xa_hd), r5(mem_vp, bp, n_mem, xa_hd),
            r5(a_ks, bs, ss, hd), r5(a_vs, bs, ss, hd), r5(b_ks, bs, ss, hd), r5(b_vs, bs, ss, hd),
            r5(c_ks, bs, ss, hd), r5(c_vs, bs, ss, hd), r4(c_lfs, bs, ss))
```

```python
import functools

import jax
import jax.numpy as jnp
from jax import lax
from jax.experimental import pallas as pl
from jax.experimental.pallas import tpu as pltpu

F32 = jnp.float32
BF16 = jnp.bfloat16

RMS_EPS = 1e-6
NEG_INF = -1e30
LOG2E = 1.4426950408889634
HEAD_DIM = 64
CHUNK = 64
A_PAST_CHUNKS = 8
A_PAST = A_PAST_CHUNKS * CHUNK
REL_CLIP = 128
XA_HEADS = 4

LANES = 128
HEADS_PER_VREG = LANES // HEAD_DIM
KEY_CHUNK = 256
ATTN_PAIRS_PER_STEP = 4
BAND_ROWS = 4 * CHUNK
ROW_TILE = 512
VMEM_LIMIT = 56 * 1024 * 1024


def _cparams(*sem):
    return pltpu.CompilerParams(dimension_semantics=sem, vmem_limit_bytes=VMEM_LIMIT)


def _resident_spec(shape):
    return pl.BlockSpec(shape, lambda *_: (0,) * len(shape), pipeline_mode=pl.Buffered(1))


def _rms_bf16(x, g):
    y = x * lax.rsqrt(jnp.mean(x * x, axis=-1, keepdims=True) + RMS_EPS)
    return (y * g).astype(BF16)


def _log_sigmoid(z):
    return jnp.minimum(z, 0.0) - jnp.log1p(jnp.exp(-jnp.abs(z)))


def _split3(x):
    hi = x.astype(BF16)
    r = x - hi.astype(F32)
    mid = r.astype(BF16)
    lo = (r - mid.astype(F32)).astype(BF16)
    return hi, mid, lo


def _split2(x):
    hi = x.astype(BF16)
    lo = (x - hi.astype(F32)).astype(BF16)
    return hi, lo


def _dot(a, b):
    return jnp.dot(a, b, preferred_element_type=F32)


def _dot_nt(a, b):
    return lax.dot_general(a, b, (((1,), (1,)), ((), ())), preferred_element_type=F32)


def _norm_proj_kernel(x_ref, g_ref, b_ref, *refs, n_w, outs, tiles_per_seq):
    w_refs, o_refs = refs[:n_w], refs[n_w:]
    h = _rms_bf16(x_ref[...], g_ref[...])
    ys = {}
    for o_ref, (grp, kind, arg) in zip(o_refs, outs):
        if grp not in ys:
            ys[grp] = _dot(h, w_refs[grp][...])
        y = ys[grp]
        if kind == 'f32':
            o_ref[...] = y
        elif kind == 'bf16':
            o_ref[...] = (y * arg).astype(BF16)
        elif kind == 'logf':
            o_ref[...] = _log_sigmoid(y + b_ref[...])
        elif kind == 'heads':
            o_ref[...] = y.reshape(o_ref.shape)
        elif kind == 'rowmax':
            hd, scale = arg
            n_cols, n_heads = y.shape[1], y.shape[1] // hd
            sel_l = lax.broadcasted_iota(jnp.int32, (n_cols, n_heads), 0) // hd
            sel_h = lax.broadcasted_iota(jnp.int32, (n_cols, n_heads), 1)
            scaled = y * scale
            norms = jnp.sqrt(_dot((scaled * scaled).astype(BF16),
                                  jnp.where(sel_l == sel_h, 1.0, 0.0).astype(BF16)))
            for r in range(o_ref.shape[1]):
                o_ref[0, r:r + 1, :] = jnp.max(norms[r * KEY_CHUNK:(r + 1) * KEY_CHUNK],
                                               axis=0, keepdims=True)
        else:
            @pl.when(pl.program_id(0) % tiles_per_seq == tiles_per_seq - 1)
            def _(o_ref=o_ref, y=y):
                o_ref[...] = y.reshape(o_ref.shape)


def norm_proj(x, g, ws, outs, *, seq, tm, bias=None):
    m, d = x.shape
    tiles_per_seq = max(seq // tm, 1)
    if bias is None:
        bias = jnp.zeros((1, 16), F32)
    in_specs = [pl.BlockSpec((tm, d), lambda i: (i, 0)),
                pl.BlockSpec((1, d), lambda i: (0, 0)),
                pl.BlockSpec(bias.shape, lambda i: (0, 0))]
    in_specs += [_resident_spec(w.shape) for w in ws]
    out_shape, out_specs = [], []
    for grp, kind, arg in outs:
        n = ws[grp].shape[1]
        if kind == 'tail':
            assert tm == min(A_PAST, seq)
            out_shape.append(jax.ShapeDtypeStruct((m // tiles_per_seq, n // arg, arg), F32))
            out_specs.append(pl.BlockSpec((tm, n // arg, arg), lambda i: (i // tiles_per_seq, 0, 0)))
        elif kind == 'heads':
            out_shape.append(jax.ShapeDtypeStruct((m, n // arg, arg), F32))
            out_specs.append(pl.BlockSpec((tm, n // arg, arg), lambda i: (i, 0, 0)))
        elif kind == 'rowmax':
            assert tm % KEY_CHUNK == 0
            blocks = (tm // KEY_CHUNK, n // arg[0])
            out_shape.append(jax.ShapeDtypeStruct((m // tm,) + blocks, F32))
            out_specs.append(pl.BlockSpec((1,) + blocks, lambda i: (i, 0, 0)))
        else:
            out_shape.append(jax.ShapeDtypeStruct((m, n), BF16 if kind == 'bf16' else F32))
            out_specs.append(pl.BlockSpec((tm, n), lambda i: (i, 0)))
    kern = functools.partial(_norm_proj_kernel, n_w=len(ws), outs=tuple(outs),
                             tiles_per_seq=tiles_per_seq)
    return pl.pallas_call(
        kern, grid=(m // tm,), in_specs=in_specs, out_specs=out_specs, out_shape=out_shape,
        compiler_params=_cparams("arbitrary"), name="norm_proj")(x, g.reshape(1, d), bias, *ws)


BIAS_BLOCK_ROWS = 32


def _band_bias_kernel(rb_ref, off_ref, o_ref, *, n_real):
    n_heads, rows, cols = o_ref.shape[1:]
    off = off_ref[pl.program_id(0)]
    i0 = pl.program_id(1) * rows
    r0 = pl.program_id(2) * cols
    i = lax.broadcasted_iota(jnp.int32, (rows, cols), 0) + i0
    r = lax.broadcasted_iota(jnp.int32, (rows, cols), 1) + r0
    d = jnp.clip(off + i - r, -REL_CLIP, REL_CLIP) + REL_CLIP
    lo = jnp.clip(off + i0 - (r0 + cols - 1), -REL_CLIP, REL_CLIP) + REL_CLIP
    hi = jnp.clip(off + i0 + rows - 1 - r0, -REL_CLIP, REL_CLIP) + REL_CLIP
    first = (i // CHUNK - A_PAST_CHUNKS) * CHUNK + off
    last = jnp.minimum((i // CHUNK + 1) * CHUNK + off, n_real)
    visible = (r >= first) & (r < last)
    first0 = (i0 // CHUNK - A_PAST_CHUNKS) * CHUNK + off
    last0 = jnp.minimum(((i0 + rows - 1) // CHUNK + 1) * CHUNK + off, n_real)
    hi = jnp.where((r0 + cols <= first0) | (r0 >= last0), lo - 1, hi)

    def body(u, tbls):
        hit = d == u
        return tuple(jnp.where(hit, rb_ref[h, u], t) for h, t in enumerate(tbls))

    tbls = lax.fori_loop(lo, hi + 1, body,
                         tuple(jnp.zeros((rows, cols), F32) for _ in range(n_heads)))
    for h in range(n_heads):
        o_ref[0, h] = jnp.where(visible, LOG2E * tbls[h], NEG_INF)


def band_bias(rel_bias, offsets, *, rows, cols, n_real):
    n_rel, n_heads = rel_bias.shape
    blk = min(rows, BIAS_BLOCK_ROWS)
    return pl.pallas_call(
        functools.partial(_band_bias_kernel, n_real=n_real),
        grid=(len(offsets), rows // blk, cols // LANES),
        in_specs=[pl.BlockSpec(memory_space=pltpu.SMEM), pl.BlockSpec(memory_space=pltpu.SMEM)],
        out_specs=pl.BlockSpec((1, n_heads, blk, LANES), lambda o, t, c: (o, 0, t, c)),
        out_shape=jax.ShapeDtypeStruct((len(offsets), n_heads, rows, cols), F32),
        compiler_params=_cparams("arbitrary", "arbitrary", "arbitrary"),
        name="band_bias")(rel_bias.T, jnp.asarray(offsets, jnp.int32))


def _head_masks(shape):
    lane = lax.broadcasted_iota(jnp.int32, shape, len(shape) - 1)
    return [(lane % LANES) // HEAD_DIM == j for j in range(HEADS_PER_VREG)]


def _ones_lanes():
    return [((j + 1) % HEADS_PER_VREG) * HEAD_DIM for j in range(HEADS_PER_VREG)]


def _masked_values(v, j):
    lane = lax.broadcasted_iota(jnp.int32, v.shape, 1) % LANES
    vj = jnp.where(lane // HEAD_DIM == j, v, jnp.zeros((), BF16))
    return jnp.where(lane == _ones_lanes()[j], jnp.ones((), BF16), vj)


def _band_attn_kernel(q_ref, k_ref, v_ref, bias_ref, o_ref, v16, s_scr, p_scr, *, tq, band):
    step = pl.program_id(1)
    ck = KEY_CHUNK

    @pl.when(step == 0)
    def _():
        v = v_ref[0]
        for j in range(HEADS_PER_VREG):
            v16[j] = _masked_values(v, j)

    start = pl.multiple_of(jnp.maximum(step * (tq // CHUNK) - A_PAST_CHUNKS, 0) * CHUNK, CHUNK)
    qmask = _head_masks((tq, LANES))
    ones_lane = _ones_lanes()
    cols = [slice(c * ck, (c + 1) * ck) for c in range(band // ck)]
    heads = [(hp, j) for hp in range(q_ref.shape[2] // LANES) for j in range(HEADS_PER_VREG)]
    lanes = [slice(hp * LANES, (hp + 1) * LANES) for hp, _ in heads]
    qs = [jnp.where(qmask[j], q_ref[0, :, lanes[n]], jnp.zeros((), BF16))
          for n, (_, j) in enumerate(heads)]

    def logits(n, c, mrun):
        k = k_ref[0, pl.ds(pl.multiple_of(start + c * ck, CHUNK), ck), lanes[n]]
        s = _dot_nt(qs[n], k) + bias_ref[0, n, :, cols[c]]
        s_scr[n % 2, :, cols[c]] = s
        for part in range(ck // LANES):
            mrun = jnp.maximum(mrun, s[:, part * LANES:(part + 1) * LANES])
        return mrun

    def probs(n, c, row_max):
        p_scr[n % 2, :, cols[c]] = jnp.exp2(s_scr[n % 2, :, cols[c]] - row_max).astype(BF16)

    def pv(n):
        return _dot(p_scr[n % 2], v16[heads[n][1], pl.ds(start, band), lanes[n]])

    neg = jnp.full((tq, LANES), NEG_INF, F32)
    outs = []
    mrun = functools.reduce(lambda m, c: logits(0, c, m), range(len(cols)), neg)
    for n in range(len(heads)):
        row_max = jnp.max(mrun, axis=-1, keepdims=True)
        mrun = neg
        for c in range(len(cols)):
            if n + 1 < len(heads):
                mrun = logits(n + 1, c, mrun)
            probs(n, c, row_max)
        o = pv(n)
        lane = ones_lane[heads[n][1]]
        outs.append(o / o[:, lane:lane + 1])
    for n in range(0, len(heads), HEADS_PER_VREG):
        o_ref[0, :, lanes[n]] = jnp.where(qmask[0], outs[n], outs[n + 1]).astype(BF16)


def band_attn(q, k, v, bias, *, tq, band):
    b, sq, w = q.shape
    sk = k.shape[1]
    n_off, n_heads = bias.shape[:2]
    assert band % KEY_CHUNK == 0 and tq % CHUNK == 0 or sq == tq
    kern = functools.partial(_band_attn_kernel, tq=tq, band=band)
    return pl.pallas_call(
        kern, grid=(b, sq // tq),
        in_specs=[pl.BlockSpec((1, tq, w), lambda i, c: (i, c, 0)),
                  pl.BlockSpec((1, sk, w), lambda i, c: (i, 0, 0)),
                  pl.BlockSpec((1, sk, w), lambda i, c: (i, 0, 0)),
                  pl.BlockSpec((1, n_heads, tq, band),
                               lambda i, c: (jnp.minimum(c, n_off - 1), 0, 0, 0))],
        out_specs=pl.BlockSpec((1, tq, w), lambda i, c: (i, c, 0)),
        out_shape=jax.ShapeDtypeStruct((b, sq, w), BF16),
        scratch_shapes=[pltpu.VMEM((HEADS_PER_VREG, sk, w), BF16),
                        pltpu.VMEM((2, tq, band), F32),
                        pltpu.VMEM((2, tq, band), BF16)],
        compiler_params=_cparams("parallel", "arbitrary"), name="band_attn")(q, k, v, bias)


STICK_UNDERFLOW_LOG2 = 160.0
STICK_NEAR_CHUNKS = 2


def _stick_kernel(q_ref, k_ref, v_ref, o_ref, v16, zl_scr, hl_scr, w_scr, *,
                  tq, q_offset, variants):
    qi = pl.program_id(2)
    ck = KEY_CHUNK
    heads = [(pp, j) for pp in range(q_ref.shape[2] // LANES) for j in range(HEADS_PER_VREG)]
    n_heads = len(heads)
    lanes = [slice(pp * LANES, (pp + 1) * LANES) for pp, _ in heads]

    @pl.when(qi == 0)
    def _():
        v = v_ref[0]
        vmask = _head_masks(v.shape)
        for j in range(HEADS_PER_VREG):
            v16[j] = jnp.where(vmask[j], v, jnp.zeros((), BF16))

    qmask = _head_masks((tq, LANES))
    qs = [jnp.where(qmask[j], q_ref[0, :, lanes[n]], jnp.zeros((), BF16))
          for n, (_, j) in enumerate(heads)]
    q_pos = q_offset + qi * tq + lax.broadcasted_iota(jnp.int32, (tq, ck), 0)
    k_iota = lax.broadcasted_iota(jnp.int32, (tq, ck), 1)
    rr = lax.broadcasted_iota(jnp.int32, (2 * ck, ck), 0) % ck
    cc = lax.broadcasted_iota(jnp.int32, (2 * ck, ck), 1)
    tri2 = jnp.where(rr > cc, 1.0, 0.0).astype(BF16)
    n_chunks = (q_offset + (qi + 1) * tq + ck - 1) // ck

    def split_cols(slot):
        return slice(2 * slot.start, 2 * slot.start + ck), slice(2 * slot.start + ck, 2 * slot.stop)

    def logits(j, slot, c, masked):
        k0 = pl.multiple_of(c * ck, ck)
        z = _dot_nt(qs[j], k_ref[0, pl.ds(k0, ck), lanes[j]])
        neg_abs = lax.bitcast_convert_type(
            lax.bitcast_convert_type(z, jnp.int32) | jnp.int32(-2 ** 31), F32)
        sp = jnp.maximum(z, 0.0) + jnp.log2(1.0 + jnp.exp2(neg_abs))
        if masked:
            sp = jnp.where(k0 + k_iota < q_pos, sp, 0.0)
        zl_scr[j, :, slot] = z - sp
        hi_cols, lo_cols = split_cols(slot)
        hl_scr[j, :, hi_cols], hl_scr[j, :, lo_cols] = _split2(sp)
        return jnp.sum(sp, axis=-1, keepdims=True)

    def weights(j, slot, c, later, masked):
        sums = _dot(hl_scr[j, :, 2 * slot.start:2 * slot.stop], tri2)
        w = jnp.exp2(zl_scr[j, :, slot] - sums - later)
        if masked:
            w = jnp.where(c * ck + k_iota < q_pos, w, 0.0)
        w_scr[j, :, slot] = w.astype(BF16)

    def more(laters):
        return (jnp.min(functools.reduce(jnp.minimum, laters))
                < STICK_UNDERFLOW_LOG2).astype(jnp.int32)

    def attend(near, n_masked):
        slots = [slice((near - 1 - i) * ck, (near - i) * ck) for i in range(near)]
        chunk = [n_chunks - 1 - i for i in range(near)]
        k0 = pl.multiple_of((n_chunks - near) * ck, ck)

        def pv(n, k0, width):
            return _dot(w_scr[n, :, :width], v16[heads[n][1], pl.ds(k0, width), lanes[n]])

        laters, accs = [], []
        row_sums = [logits(0, slots[i], chunk[i], i < n_masked) for i in range(near)]
        for n in range(n_heads):
            later = jnp.zeros((tq, 1), F32)
            next_sums = []
            for i in range(near):
                if n + 1 < n_heads:
                    next_sums.append(logits(n + 1, slots[i], chunk[i], i < n_masked))
                weights(n, slots[i], chunk[i], later, i < n_masked)
                later = later + row_sums[i]
            laters.append(later)
            row_sums = next_sums
            o = pv(n, k0, near * ck)
            if heads[n][1] == 0:
                accs.append(o)
            else:
                accs[-1] = accs[-1] + o

        def cond(carry):
            return (carry[0] >= 0) & (carry[1] > 0)

        def body(carry):
            c, _, accs, laters = carry
            accs = list(accs)
            k0 = pl.multiple_of(c * ck, ck)
            new = []
            for n in range(n_heads):
                rs = logits(n, slice(0, ck), c, False)
                weights(n, slice(0, ck), c, laters[n], False)
                accs[heads[n][0]] = accs[heads[n][0]] + pv(n, k0, ck)
                new.append(laters[n] + rs)
            return c - 1, more(new), tuple(accs), tuple(new)

        carry = lax.while_loop(cond, body,
                               (n_chunks - 1 - near, more(laters), tuple(accs), tuple(laters)))
        for pp, acc in enumerate(carry[2]):
            o_ref[0, :, pp * LANES:(pp + 1) * LANES] = acc.astype(BF16)

    for n_total, near, n_masked in variants:
        if n_total is None:
            pl.when(n_chunks >= near)(functools.partial(attend, near, n_masked))
        else:
            pl.when(n_chunks == n_total)(functools.partial(attend, near, n_masked))


def stick_attn(q, k, v, *, tq, q_offset):
    b, sq, w = q.shape
    sk = k.shape[1]
    ck = KEY_CHUNK
    assert q_offset % ck == 0 and (tq % ck == 0 or sq == tq <= ck) and sk % ck == 0
    n_masked = -(-tq // ck)
    totals = sorted({-(-(q_offset + (t + 1) * tq) // ck) for t in range(sq // tq)})
    variants = [(n, n, min(n_masked, n)) for n in totals if n < STICK_NEAR_CHUNKS]
    if totals[-1] >= STICK_NEAR_CHUNKS:
        variants.append((None, STICK_NEAR_CHUNKS, n_masked))
    near_cols = STICK_NEAR_CHUNKS * ck
    wg = ATTN_PAIRS_PER_STEP * LANES
    n_heads = ATTN_PAIRS_PER_STEP * HEADS_PER_VREG
    kern = functools.partial(_stick_kernel, tq=tq, q_offset=q_offset, variants=tuple(variants))
    return pl.pallas_call(
        kern, grid=(b, w // wg, sq // tq),
        in_specs=[pl.BlockSpec((1, tq, wg), lambda i, h, t: (i, t, h)),
                  pl.BlockSpec((1, sk, wg), lambda i, h, t: (i, 0, h)),
                  pl.BlockSpec((1, sk, wg), lambda i, h, t: (i, 0, h))],
        out_specs=pl.BlockSpec((1, tq, wg), lambda i, h, t: (i, t, h)),
        out_shape=jax.ShapeDtypeStruct((b, sq, w), BF16),
        scratch_shapes=[pltpu.VMEM((HEADS_PER_VREG, sk, wg), BF16),
                        pltpu.VMEM((n_heads, tq, near_cols), F32),
                        pltpu.VMEM((n_heads, tq, 2 * near_cols), BF16),
                        pltpu.VMEM((n_heads, tq, near_cols), BF16)],
        compiler_params=_cparams("parallel", "parallel", "arbitrary"), name="stick_attn")(q, k, v)


def _cumsum_kernel(lf_ref, lft_ref, col_ref, row_ref, *, blk):
    s_len, n_heads = lf_ref.shape[1], lf_ref.shape[2]
    rr = lax.broadcasted_iota(jnp.int32, (blk, blk), 0)
    cc = lax.broadcasted_iota(jnp.int32, (blk, blk), 1)
    lower = jnp.where(rr >= cc, 1.0, 0.0).astype(BF16)
    upper = jnp.where(rr <= cc, 1.0, 0.0).astype(BF16)
    blocks = [slice(n * blk, (n + 1) * blk) for n in range(s_len // blk)]
    local_col = [sum(_dot(lower, p) for p in _split3(lf_ref[0, rows, :])) for rows in blocks]
    local_row = [sum(_dot(p, upper) for p in _split3(lft_ref[0, :, rows])) for rows in blocks]
    carry_col = jnp.zeros((1, n_heads), F32)
    carry_row = jnp.zeros((n_heads, 1), F32)
    for rows, c, ct in zip(blocks, local_col, local_row):
        col_ref[0, rows, :] = c + carry_col
        row_ref[0, :, rows] = ct + carry_row
        carry_col = carry_col + c[blk - 1:blk, :]
        carry_row = carry_row + ct[:, blk - 1:blk]


def cumsum_logf(lf):
    b, s_len, n_heads = lf.shape
    kern = functools.partial(_cumsum_kernel, blk=LANES)
    return pl.pallas_call(
        kern, grid=(b,),
        in_specs=[pl.BlockSpec((1, s_len, n_heads), lambda i: (i, 0, 0)),
                  pl.BlockSpec((1, n_heads, s_len), lambda i: (i, 0, 0))],
        out_specs=[pl.BlockSpec((1, s_len, n_heads), lambda i: (i, 0, 0)),
                   pl.BlockSpec((1, n_heads, s_len), lambda i: (i, 0, 0))],
        out_shape=[jax.ShapeDtypeStruct((b, s_len, n_heads), F32),
                   jax.ShapeDtypeStruct((b, n_heads, s_len), F32)],
        compiler_params=_cparams("parallel"), name="cumsum_logf")(lf, jnp.swapaxes(lf, 1, 2))


FORGET_SKIP_LOG2 = 152.0
FORGET_BOUND_SLACK = 1.02
FORGET_BOUND_MARGIN = 2.0


def _forget_skip_kernel(qmax_ref, kmax_ref, cum_ref, o_ref, *, heads_per_group):
    n_blk, n_heads = kmax_ref.shape[1], kmax_ref.shape[2]
    blk = cum_ref.shape[1] // n_blk
    kmax = kmax_ref[0]
    k_term = kmax + jnp.max(kmax, axis=0, keepdims=True)
    cum_end = cum_ref[0, pl.ds(blk - 1, n_blk, stride=blk), :]
    cum_start = cum_ref[0, pl.ds(0, n_blk, stride=blk), :]
    chunk_id = lax.broadcasted_iota(jnp.int32, (n_blk, 1), 0)
    head_id = lax.broadcasted_iota(jnp.int32, (n_blk, n_heads), 1)
    row = lax.broadcasted_iota(jnp.int32, o_ref.shape[1:], 0)
    col = lax.broadcasted_iota(jnp.int32, o_ref.shape[1:], 1)
    table = jnp.zeros(o_ref.shape[1:], F32)
    for qi in range(n_blk):
        bound = (FORGET_BOUND_SLACK * qmax_ref[0, qi:qi + 1, :] * k_term
                 + LOG2E * (cum_start[qi:qi + 1, :] - cum_end) + FORGET_BOUND_MARGIN)
        for g in range(n_heads // heads_per_group):
            worst = jnp.max(jnp.where(head_id // heads_per_group == g, bound, NEG_INF),
                            axis=1, keepdims=True)
            needed = jnp.logical_not(worst < -FORGET_SKIP_LOG2) | (chunk_id >= qi)
            first = jnp.min(jnp.where(needed, chunk_id, n_blk).astype(F32), axis=0, keepdims=True)
            table = jnp.where((row == qi) & (col == g), first, table)
    o_ref[0] = table.astype(jnp.int32)


def forget_skip_table(qmax, kmax, cum, *, heads_per_group):
    b, n_blk, n_heads = kmax.shape
    s_len = cum.shape[1]
    blk3 = lambda shape: pl.BlockSpec((1,) + shape, lambda i: (i, 0, 0))
    return pl.pallas_call(
        functools.partial(_forget_skip_kernel, heads_per_group=heads_per_group), grid=(b,),
        in_specs=[blk3((n_blk, n_heads)), blk3((n_blk, n_heads)), blk3((s_len, n_heads))],
        out_specs=blk3((n_blk, LANES)),
        out_shape=jax.ShapeDtypeStruct((b, n_blk, LANES), jnp.int32),
        compiler_params=_cparams("parallel"), name="forget_skip_table")(qmax, kmax, cum)


def _forget_kernel(c0_ref, q_ref, k_ref, v_ref, cq_ref, ck_ref, o_ref, v16, s_scr, p_scr, *,
                   tq, q_offset, max_chunks):
    group = pl.program_id(1)
    qi = pl.program_id(2)
    wg = k_ref.shape[2]
    ck_w = KEY_CHUNK
    ones_lane = _ones_lanes()
    heads = [(pp, j) for pp in range(wg // LANES) for j in range(HEADS_PER_VREG)]
    lanes = [slice(pp * LANES, (pp + 1) * LANES) for pp, _ in heads]

    @pl.when(qi == 0)
    def _():
        v = v_ref[0]
        for j in range(HEADS_PER_VREG):
            v16[j] = _masked_values(v, j)

    qmask = _head_masks((tq, LANES))
    q_pos = q_offset + qi * tq + lax.broadcasted_iota(jnp.int32, (tq, ck_w), 0)
    k_iota = lax.broadcasted_iota(jnp.int32, (tq, ck_w), 1)
    cq_all = cq_ref[0]
    head_lane = lax.broadcasted_iota(jnp.int32, cq_all.shape, 1)
    n_c = (q_offset + (qi + 1) * tq + ck_w - 1) // ck_w
    step = (pl.program_id(0) * pl.num_programs(1) + group) * pl.num_programs(2) + qi
    c0 = jnp.minimum(c0_ref[step], n_c - 1)
    n_proc_here = n_c - c0

    def attend(n_proc):
        cols = [slice(c * ck_w, (c + 1) * ck_w) for c in range(n_proc)]
        k0 = [pl.multiple_of((c0 + c) * ck_w, ck_w) for c in range(n_proc)]
        allowed = k0[-1] + k_iota <= q_pos
        qs = [jnp.where(qmask[j], q_ref[0, :, lanes[n]], jnp.zeros((), BF16))
              for n, (_, j) in enumerate(heads)]
        cq = [LOG2E * jnp.sum(jnp.where(head_lane == group * len(heads) + n, cq_all, 0.0),
                              axis=-1, keepdims=True) for n in range(len(heads))]

        def logits(n, c, mrun):
            s = (_dot_nt(qs[n], k_ref[0, pl.ds(k0[c], ck_w), lanes[n]])
                 - LOG2E * ck_ref[0, 0, n:n + 1, pl.ds(k0[c], ck_w)])
            if c == n_proc - 1:
                s = jnp.where(allowed, s, NEG_INF)
            s_scr[n % 2, :, cols[c]] = s
            for part in range(ck_w // LANES):
                mrun = jnp.maximum(mrun, s[:, part * LANES:(part + 1) * LANES])
            return mrun

        def row_term(n, mrun):
            m = jnp.max(mrun, axis=-1, keepdims=True) + cq[n]
            return cq[n] - m

        def probs(n, c, row):
            p_scr[n % 2, :, cols[c]] = jnp.exp2(s_scr[n % 2, :, cols[c]] + row).astype(BF16)

        def pv(n):
            return _dot(p_scr[n % 2, :, :n_proc * ck_w],
                        v16[heads[n][1], pl.ds(k0[0], n_proc * ck_w), lanes[n]])

        neg = jnp.full((tq, LANES), NEG_INF, F32)
        outs = []
        mrun = functools.reduce(lambda m, c: logits(0, c, m), range(n_proc), neg)
        for n in range(len(heads)):
            row = row_term(n, mrun)
            mrun = neg
            for c in range(n_proc):
                if n + 1 < len(heads):
                    mrun = logits(n + 1, c, mrun)
                probs(n, c, row)
            o = pv(n)
            lane = ones_lane[heads[n][1]]
            outs.append(o / o[:, lane:lane + 1])
        for n in range(0, len(heads), HEADS_PER_VREG):
            o_ref[0, :, lanes[n]] = jnp.where(qmask[0], outs[n], outs[n + 1]).astype(BF16)

    for n_proc in range(1, max_chunks + 1):
        pl.when(n_proc_here == n_proc)(functools.partial(attend, n_proc))


def forget_attn(q, k, v, cum_q, cum_k_rows, skip, *, tq, q_offset):
    b, sq, w = q.shape
    sk = k.shape[1]
    n_heads = cum_q.shape[2]
    ck_w = KEY_CHUNK
    diag = [-(-(q_offset + (t + 1) * tq) // ck_w) - 1 for t in range(sq // tq)]
    assert sk % ck_w == 0 and all((q_offset + t * tq) // ck_w == c for t, c in enumerate(diag))
    max_chunks = max(diag) + 1
    wg = ATTN_PAIRS_PER_STEP * LANES
    heads_per_step = ATTN_PAIRS_PER_STEP * HEADS_PER_VREG
    assert skip.shape == (b, w // wg, sq // tq)
    ck = cum_k_rows.reshape(b, n_heads // heads_per_step, heads_per_step, sk)
    kern = functools.partial(_forget_kernel, tq=tq, q_offset=q_offset, max_chunks=max_chunks)
    grid_spec = pltpu.PrefetchScalarGridSpec(
        num_scalar_prefetch=1, grid=(b, w // wg, sq // tq),
        in_specs=[pl.BlockSpec((1, tq, wg), lambda i, h, t, c0: (i, t, h)),
                  pl.BlockSpec((1, sk, wg), lambda i, h, t, c0: (i, 0, h)),
                  pl.BlockSpec((1, sk, wg), lambda i, h, t, c0: (i, 0, h)),
                  pl.BlockSpec((1, tq, n_heads), lambda i, h, t, c0: (i, t, 0)),
                  pl.BlockSpec((1, 1, heads_per_step, sk), lambda i, h, t, c0: (i, h, 0, 0))],
        out_specs=pl.BlockSpec((1, tq, wg), lambda i, h, t, c0: (i, t, h)),
        scratch_shapes=[pltpu.VMEM((HEADS_PER_VREG, sk, wg), BF16),
                        pltpu.VMEM((2, tq, max_chunks * ck_w), F32),
                        pltpu.VMEM((2, tq, max_chunks * ck_w), BF16)])
    return pl.pallas_call(
        kern, grid_spec=grid_spec, out_shape=jax.ShapeDtypeStruct((b, sq, w), BF16),
        compiler_params=_cparams("parallel", "parallel", "arbitrary"),
        name="forget_attn")(skip.reshape(-1), q, k, v, cum_q, ck)


def _layer_tail_kernel(*refs, n_in, n_heads, rows_per_seq, tf, final_norm):
    a_refs, w_refs = refs[:n_in], refs[n_in:2 * n_in]
    (x_ref, gx_ref, wq_ref, wo_ref, mk_ref, mv_ref, gf_ref, wg_ref, wu_ref, wd_ref, gfin_ref,
     o_ref) = refs[2 * n_in:]
    x = x_ref[...]
    for a_ref, w_ref in zip(a_refs, w_refs):
        x = x + _dot(a_ref[...], w_ref[...])

    hd = x.shape[1] // n_heads
    q = (_dot(_rms_bf16(x, gx_ref[...]), wq_ref[...]) * (hd ** -0.5)).astype(BF16)
    per_seq = []
    for b in range(mk_ref.shape[0]):
        rows = slice(b * rows_per_seq, (b + 1) * rows_per_seq)
        outs = []
        for j in range(n_heads):
            cols = slice(j * hd, (j + 1) * hd)
            s = _dot_nt(q[rows, cols], mk_ref[b, :, cols])
            p = jnp.exp(s - jnp.max(s, axis=-1, keepdims=True))
            p = p / jnp.sum(p, axis=-1, keepdims=True)
            outs.append(_dot(p.astype(BF16), mv_ref[b, :, cols]).astype(BF16))
        per_seq.append(jnp.concatenate(outs, axis=-1))
    x = x + _dot(jnp.concatenate(per_seq, axis=0), wo_ref[...])

    h = _rms_bf16(x, gf_ref[...])
    for c in range(wg_ref.shape[1] // tf):
        cols = slice(c * tf, (c + 1) * tf)
        gate = _dot(h, wg_ref[:, cols])
        up = _dot(h, wu_ref[:, cols])
        a = (gate * jax.nn.sigmoid(gate) * up).astype(BF16)
        x = x + _dot(a, wd_ref[cols, :])
    if final_norm:
        x = x * lax.rsqrt(jnp.mean(x * x, axis=-1, keepdims=True) + RMS_EPS) * gfin_ref[...]
    o_ref[...] = x


def layer_tail(a_list, w_list, x, g_x, wq, wo, mk, mv, g_f, wg, wu, wd, g_final, *, seq, tm, tf,
               final_norm):
    m, d = x.shape
    n_mem = mk.shape[1]
    rows_per_seq = min(tm, seq)
    seqs = tm // rows_per_seq
    assert seq % rows_per_seq == 0 and m % tm == 0
    row = lambda i: (i, 0)
    mem = lambda i: (i * tm // (seq * seqs), 0, 0)
    gain = lambda g: g.reshape(1, d)
    kern = functools.partial(_layer_tail_kernel, n_in=len(a_list), n_heads=XA_HEADS,
                             rows_per_seq=rows_per_seq, tf=tf, final_norm=final_norm)
    in_specs = ([pl.BlockSpec((tm, a.shape[1]), row) for a in a_list]
                + [_resident_spec(w.shape) for w in w_list]
                + [pl.BlockSpec((tm, d), row), _resident_spec((1, d)),
                   _resident_spec(wq.shape), _resident_spec(wo.shape),
                   pl.BlockSpec((seqs, n_mem, d), mem), pl.BlockSpec((seqs, n_mem, d), mem),
                   _resident_spec((1, d)), _resident_spec(wg.shape), _resident_spec(wu.shape),
                   _resident_spec(wd.shape), _resident_spec((1, d))])
    return pl.pallas_call(
        kern, grid=(m // tm,), in_specs=in_specs,
        out_specs=pl.BlockSpec((tm, d), row), out_shape=jax.ShapeDtypeStruct((m, d), F32),
        compiler_params=_cparams("parallel"), name="layer_tail")(
            *a_list, *w_list, x, gain(g_x), wq, wo, mk, mv, gain(g_f), wg, wu, wd, gain(g_final))


def _pad_rows(a, rows):
    return jnp.pad(a, ((0, 0), (0, rows - a.shape[1]), (0, 0)))


def _row_tile(m, seq, cap):
    tm = min(cap, seq)
    assert seq % tm == 0 and m % tm == 0
    return tm


def _layer_ab(x, seq, g, w_in, w_out, rel_bias, cache, *, tm):
    m, d = x.shape
    b = m // seq
    wa = w_in.shape[1] // 6
    ws = [w_in[:, n * wa:(n + 1) * wa].astype(BF16) for n in range(6)]
    scale = HEAD_DIM ** -0.5 * LOG2E
    kept = 'tail' if min(A_PAST, seq) < seq else 'heads'
    outs = [(4, 'heads', HEAD_DIM), (4, 'bf16', 1.0), (5, 'heads', HEAD_DIM), (5, 'bf16', 1.0),
            (0, 'bf16', scale), (3, 'bf16', scale), (1, 'bf16', 1.0), (2, 'bf16', 1.0),
            (1, kept, HEAD_DIM), (2, kept, HEAD_DIM)]
    kb, kb16, vb, vb16, qa, qb, ka16, va16, ka_keep, va_keep = norm_proj(x, g, ws, outs, seq=seq,
                                                                         tm=tm)
    shp = lambda a: a.reshape(b, -1, a.shape[-1])
    if cache is None:
        tq = BAND_ROWS
        band = -(-(A_PAST + tq) // KEY_CHUNK) * KEY_CHUNK
        offsets = sorted({t * tq - max(t * (tq // CHUNK) - A_PAST_CHUNKS, 0) * CHUNK
                          for t in range(seq // tq)})
        bias = band_bias(rel_bias, offsets, rows=tq, cols=band, n_real=band)
        oa = band_attn(shp(qa), shp(ka16), shp(va16), bias, tq=tq, band=band)
        ob = stick_attn(shp(qb), shp(kb16), shp(vb16), tq=KEY_CHUNK, q_offset=0)
    else:
        ca_k, ca_v, cb_k, cb_v = cache
        n_past = ca_k.shape[1]
        n_keys = n_past + seq
        band = -(-n_keys // KEY_CHUNK) * KEY_CHUNK
        flat = lambda a: a.reshape(a.shape[0], a.shape[1], -1)
        k_all = _pad_rows(jnp.concatenate([flat(ca_k).astype(BF16), shp(ka16)], axis=1), band)
        v_all = _pad_rows(jnp.concatenate([flat(ca_v).astype(BF16), shp(va16)], axis=1), band)
        bias = band_bias(rel_bias, [n_past], rows=seq, cols=band, n_real=n_keys)
        oa = band_attn(shp(qa), k_all, v_all, bias, tq=seq, band=band)
        n_pastb = cb_k.shape[1]
        sk = -(-(n_pastb + seq) // KEY_CHUNK) * KEY_CHUNK
        kb_all = _pad_rows(jnp.concatenate([flat(cb_k).astype(BF16), shp(kb16)], axis=1), sk)
        vb_all = _pad_rows(jnp.concatenate([flat(cb_v).astype(BF16), shp(vb16)], axis=1), sk)
        ob = stick_attn(shp(qb), kb_all, vb_all, tq=seq, q_offset=n_pastb)
    w_out = w_out.astype(BF16)
    wo_a, wo_b = w_out[:oa.shape[-1]], w_out[oa.shape[-1]:]
    return ([oa.reshape(m, -1), ob.reshape(m, -1)], [wo_a, wo_b]), (ka_keep, va_keep, kb, vb)


def _layer_c(x, seq, g, w_in, b_f, w_out, cache, *, tm):
    m, d = x.shape
    b = m // seq
    n_heads = b_f.shape[0]
    wq, wk, wv, wf = (w_in[:, :d], w_in[:, d:2 * d], w_in[:, 2 * d:3 * d], w_in[:, 3 * d:])
    ws = [w.astype(BF16) for w in (wq, wk, wv, wf)]
    scale = HEAD_DIM ** -0.5 * LOG2E
    outs = [(1, 'heads', HEAD_DIM), (1, 'bf16', 1.0), (2, 'heads', HEAD_DIM), (2, 'bf16', 1.0),
            (0, 'bf16', scale), (3, 'logf', None)]
    if cache is None:
        outs += [(1, 'rowmax', (HEAD_DIM, 1.0)), (0, 'rowmax', (HEAD_DIM, scale))]
    k, k16, v, v16, q, lf, *norms = norm_proj(x, g, ws, outs, seq=seq, tm=tm,
                                              bias=b_f.reshape(1, n_heads))
    norms = norms[::-1]
    shp = lambda a: a.reshape(b, -1, a.shape[-1])
    heads_per_group = ATTN_PAIRS_PER_STEP * HEADS_PER_VREG
    n_groups = n_heads // heads_per_group
    if cache is None:
        cum_col, cum_row = cumsum_logf(shp(lf))
        tq = KEY_CHUNK
        qmax, kmax = (a.reshape(b, seq // KEY_CHUNK, n_heads) for a in norms)
        skip = forget_skip_table(qmax, kmax, cum_col, heads_per_group=heads_per_group)
        skip = jnp.swapaxes(skip[:, :, :n_groups], 1, 2)
        o = forget_attn(shp(q), shp(k16), shp(v16), cum_col, cum_row, skip, tq=tq, q_offset=0)
    else:
        c_k, c_v, c_lf = cache
        n_past = c_k.shape[1]
        sk = -(-(n_past + seq) // KEY_CHUNK) * KEY_CHUNK
        flat = lambda a: a.reshape(a.shape[0], a.shape[1], -1)
        k_all = _pad_rows(jnp.concatenate([flat(c_k).astype(BF16), shp(k16)], axis=1), sk)
        v_all = _pad_rows(jnp.concatenate([flat(c_v).astype(BF16), shp(v16)], axis=1), sk)
        lf_all = _pad_rows(jnp.concatenate([c_lf, shp(lf)], axis=1), sk)
        cum_col, cum_row = cumsum_logf(lf_all)
        o = forget_attn(shp(q), k_all, v_all, cum_col[:, n_past:n_past + seq], cum_row,
                        jnp.zeros((b, n_groups, 1), jnp.int32), tq=seq, q_offset=n_past)
    return ([o.reshape(m, d)], [w_out.astype(BF16)]), (k, v, lf)


def kernel(x_prompt, x_sample, cache_a_k, cache_a_v, cache_b_k, cache_b_v, cache_c_k, cache_c_v, cache_c_logf, cache_mem_k, cache_mem_v, mem_prompt, w_in_ab, w_out_ab, rel_bias_a, w_in_c, b_f_c, w_out_c, g_mix, g_xattn, g_mem, w_xq, w_xk, w_xv, w_xo, g_ffn, w_gate, w_up, w_down, g_final):
    bp, sp, d = x_prompt.shape
    bs, ss, _ = x_sample.shape
    depth = g_mix.shape[0]
    n_mem = mem_prompt.shape[1]
    xp = x_prompt.reshape(bp * sp, d)
    xs = x_sample.reshape(bs * ss, d)
    tmp = _row_tile(bp * sp, sp, ROW_TILE)
    tms = bs * ss
    assert tms <= ROW_TILE
    mem = mem_prompt.reshape(bp * n_mem, d)
    tmm = min(ROW_TILE, bp * n_mem)
    assert (bp * n_mem) % tmm == 0
    dff = w_gate.shape[2]
    tf = 256 if dff % 256 == 0 else dff

    a_kp, a_vp, b_kp, b_vp, a_ks, a_vs, b_ks, b_vs = [], [], [], [], [], [], [], []
    c_kp, c_vp, c_lfp, c_ks, c_vs, c_lfs = [], [], [], [], [], []
    mem_kp, mem_vp = [], []
    for layer in range(depth):
        if layer % 2 == 0:
            e = layer // 2
            mix_p, (ka, va, kb, vb) = _layer_ab(xp, sp, g_mix[layer], w_in_ab[e], w_out_ab[e],
                                                rel_bias_a[e], None, tm=tmp)
            a_kp.append(ka); a_vp.append(va); b_kp.append(kb); b_vp.append(vb)
            mix_s, (ka, va, kb, vb) = _layer_ab(
                xs, ss, g_mix[layer], w_in_ab[e], w_out_ab[e], rel_bias_a[e],
                (cache_a_k[e], cache_a_v[e], cache_b_k[e], cache_b_v[e]), tm=tms)
            a_ks.append(ka); a_vs.append(va); b_ks.append(kb); b_vs.append(vb)
        else:
            c = layer // 2
            mix_p, (k, v, lf) = _layer_c(xp, sp, g_mix[layer], w_in_c[c], b_f_c[c], w_out_c[c],
                                         None, tm=tmp)
            c_kp.append(k); c_vp.append(v); c_lfp.append(lf)
            mix_s, (k, v, lf) = _layer_c(xs, ss, g_mix[layer], w_in_c[c], b_f_c[c], w_out_c[c],
                                         (cache_c_k[c], cache_c_v[c], cache_c_logf[c]), tm=tms)
            c_ks.append(k); c_vs.append(v); c_lfs.append(lf)
        mk, mk16, mv, mv16 = norm_proj(
            mem, g_mem[layer], [w_xk[layer].astype(BF16), w_xv[layer].astype(BF16)],
            [(0, 'heads', d // XA_HEADS), (0, 'bf16', 1.0), (1, 'heads', d // XA_HEADS),
             (1, 'bf16', 1.0)],
            seq=n_mem, tm=tmm)
        mem_kp.append(mk); mem_vp.append(mv)
        tail = functools.partial(
            layer_tail, g_x=g_xattn[layer], wq=w_xq[layer].astype(BF16),
            wo=w_xo[layer].astype(BF16), g_f=g_ffn[layer], wg=w_gate[layer].astype(BF16),
            wu=w_up[layer].astype(BF16), wd=w_down[layer].astype(BF16), g_final=g_final, tf=tf,
            final_norm=layer == depth - 1)
        xp = tail(*mix_p, xp, mk=mk16.reshape(bp, n_mem, d), mv=mv16.reshape(bp, n_mem, d),
                  seq=sp, tm=tmp)
        xs = tail(*mix_s, xs, mk=cache_mem_k[layer].reshape(bs, n_mem, d).astype(BF16),
                  mv=cache_mem_v[layer].reshape(bs, n_mem, d).astype(BF16), seq=ss, tm=tms)

    hd = HEAD_DIM
    xa_hd = d // XA_HEADS
    r5 = lambda lst, b, s, dd: jnp.stack([a.reshape(b, s, -1, dd) for a in lst])
    r4 = lambda lst, b, s: jnp.stack([a.reshape(b, s, -1) for a in lst])
    keep = min(A_PAST, sp)
    return (xp.reshape(bp, sp, d), xs.reshape(bs, ss, d),
            r5(a_kp, bp, keep, hd), r5(a_vp, bp, keep, hd), r5(b_kp, bp, sp, hd), r5(b_vp, bp, sp, hd),
            r5(c_kp, bp, sp, hd), r5(c_vp, bp, sp, hd), r4(c_lfp, bp, sp),
            r5(mem_kp, bp, n_mem, xa_hd), r5(mem_vp, bp, n_mem, xa_hd),
            r5(a_ks, bs, ss, hd), r5(a_vs, bs, ss, hd), r5(b_ks, bs, ss, hd), r5(b_vs, bs, ss, hd),
            r5(c_ks, bs, ss, hd), r5(c_vs, bs, ss, hd), r4(c_lfs, bs, ss))
```

```python
import functools

import jax
import jax.numpy as jnp
from jax import lax
from jax.experimental import pallas as pl
from jax.experimental.pallas import tpu as pltpu

F32 = jnp.float32
BF16 = jnp.bfloat16

RMS_EPS = 1e-6
NEG_INF = -1e30
LOG2E = 1.4426950408889634
HEAD_DIM = 64
CHUNK = 64
A_PAST_CHUNKS = 8
A_PAST = A_PAST_CHUNKS * CHUNK
REL_CLIP = 128
XA_HEADS = 4

LANES = 128
HEADS_PER_VREG = LANES // HEAD_DIM
KEY_CHUNK = 256
ATTN_PAIRS_PER_STEP = 4
BAND_ROWS = 4 * CHUNK
ROW_TILE = 512
VMEM_LIMIT = 56 * 1024 * 1024


def _cparams(*sem):
    return pltpu.CompilerParams(dimension_semantics=sem, vmem_limit_bytes=VMEM_LIMIT)


def _resident_spec(shape):
    return pl.BlockSpec(shape, lambda *_: (0,) * len(shape), pipeline_mode=pl.Buffered(1))


def _rms_bf16(x, g):
    y = x * lax.rsqrt(jnp.mean(x * x, axis=-1, keepdims=True) + RMS_EPS)
    return (y * g).astype(BF16)


def _log_sigmoid(z):
    return jnp.minimum(z, 0.0) - jnp.log1p(jnp.exp(-jnp.abs(z)))


def _split3(x):
    hi = x.astype(BF16)
    r = x - hi.astype(F32)
    mid = r.astype(BF16)
    lo = (r - mid.astype(F32)).astype(BF16)
    return hi, mid, lo


def _split2(x):
    hi = x.astype(BF16)
    lo = (x - hi.astype(F32)).astype(BF16)
    return hi, lo


def _dot(a, b):
    return jnp.dot(a, b, preferred_element_type=F32)


def _dot_nt(a, b):
    return lax.dot_general(a, b, (((1,), (1,)), ((), ())), preferred_element_type=F32)


def _norm_proj_kernel(x_ref, g_ref, b_ref, *refs, n_w, outs, tiles_per_seq):
    w_refs, o_refs = refs[:n_w], refs[n_w:]
    h = _rms_bf16(x_ref[...], g_ref[...])
    ys = {}
    for o_ref, (grp, kind, arg) in zip(o_refs, outs):
        if grp not in ys:
            ys[grp] = _dot(h, w_refs[grp][...])
        y = ys[grp]
        if kind == 'f32':
            o_ref[...] = y
        elif kind == 'bf16':
            o_ref[...] = (y * arg).astype(BF16)
        elif kind == 'logf':
            o_ref[...] = _log_sigmoid(y + b_ref[...])
        elif kind == 'heads':
            o_ref[...] = y.reshape(o_ref.shape)
        elif kind == 'rowmax':
            hd, scale = arg
            n_cols, n_heads = y.shape[1], y.shape[1] // hd
            sel_l = lax.broadcasted_iota(jnp.int32, (n_cols, n_heads), 0) // hd
            sel_h = lax.broadcasted_iota(jnp.int32, (n_cols, n_heads), 1)
            scaled = y * scale
            norms = jnp.sqrt(_dot((scaled * scaled).astype(BF16),
                                  jnp.where(sel_l == sel_h, 1.0, 0.0).astype(BF16)))
            for r in range(o_ref.shape[1]):
                o_ref[0, r:r + 1, :] = jnp.max(norms[r * KEY_CHUNK:(r + 1) * KEY_CHUNK],
                                               axis=0, keepdims=True)
        else:
            @pl.when(pl.program_id(0) % tiles_per_seq == tiles_per_seq - 1)
            def _(o_ref=o_ref, y=y):
                o_ref[...] = y.reshape(o_ref.shape)


def norm_proj(x, g, ws, outs, *, seq, tm, bias=None):
    m, d = x.shape
    tiles_per_seq = max(seq // tm, 1)
    if bias is None:
        bias = jnp.zeros((1, 16), F32)
    in_specs = [pl.BlockSpec((tm, d), lambda i: (i, 0)),
                pl.BlockSpec((1, d), lambda i: (0, 0)),
                pl.BlockSpec(bias.shape, lambda i: (0, 0))]
    in_specs += [_resident_spec(w.shape) for w in ws]
    out_shape, out_specs = [], []
    for grp, kind, arg in outs:
        n = ws[grp].shape[1]
        if kind == 'tail':
            assert tm == min(A_PAST, seq)
            out_shape.append(jax.ShapeDtypeStruct((m // tiles_per_seq, n // arg, arg), F32))
            out_specs.append(pl.BlockSpec((tm, n // arg, arg), lambda i: (i // tiles_per_seq, 0, 0)))
        elif kind == 'heads':
            out_shape.append(jax.ShapeDtypeStruct((m, n // arg, arg), F32))
            out_specs.append(pl.BlockSpec((tm, n // arg, arg), lambda i: (i, 0, 0)))
        elif kind == 'rowmax':
            assert tm % KEY_CHUNK == 0
            blocks = (tm // KEY_CHUNK, n // arg[0])
            out_shape.append(jax.ShapeDtypeStruct((m // tm,) + blocks, F32))
            out_specs.append(pl.BlockSpec((1,) + blocks, lambda i: (i, 0, 0)))
        else:
            out_shape.append(jax.ShapeDtypeStruct((m, n), BF16 if kind == 'bf16' else F32))
            out_specs.append(pl.BlockSpec((tm, n), lambda i: (i, 0)))
    kern = functools.partial(_norm_proj_kernel, n_w=len(ws), outs=tuple(outs),
                             tiles_per_seq=tiles_per_seq)
    return pl.pallas_call(
        kern, grid=(m // tm,), in_specs=in_specs, out_specs=out_specs, out_shape=out_shape,
        compiler_params=_cparams("arbitrary"), name="norm_proj")(x, g.reshape(1, d), bias, *ws)


BIAS_BLOCK_ROWS = 32


def _band_bias_kernel(rb_ref, off_ref, o_ref, *, n_real):
    n_heads, rows, cols = o_ref.shape[1:]
    off = off_ref[pl.program_id(0)]
    i0 = pl.program_id(1) * rows
    r0 = pl.program_id(2) * cols
    i = lax.broadcasted_iota(jnp.int32, (rows, cols), 0) + i0
    r = lax.broadcasted_iota(jnp.int32, (rows, cols), 1) + r0
    d = jnp.clip(off + i - r, -REL_CLIP, REL_CLIP) + REL_CLIP
    lo = jnp.clip(off + i0 - (r0 + cols - 1), -REL_CLIP, REL_CLIP) + REL_CLIP
    hi = jnp.clip(off + i0 + rows - 1 - r0, -REL_CLIP, REL_CLIP) + REL_CLIP
    first = (i // CHUNK - A_PAST_CHUNKS) * CHUNK + off
    last = jnp.minimum((i // CHUNK + 1) * CHUNK + off, n_real)
    visible = (r >= first) & (r < last)
    first0 = (i0 // CHUNK - A_PAST_CHUNKS) * CHUNK + off
    last0 = jnp.minimum(((i0 + rows - 1) // CHUNK + 1) * CHUNK + off, n_real)
    hi = jnp.where((r0 + cols <= first0) | (r0 >= last0), lo - 1, hi)

    def body(u, tbls):
        hit = d == u
        return tuple(jnp.where(hit, rb_ref[h, u], t) for h, t in enumerate(tbls))

    tbls = lax.fori_loop(lo, hi + 1, body,
                         tuple(jnp.zeros((rows, cols), F32) for _ in range(n_heads)))
    for h in range(n_heads):
        o_ref[0, h] = jnp.where(visible, LOG2E * tbls[h], NEG_INF)


def band_bias(rel_bias, offsets, *, rows, cols, n_real):
    n_rel, n_heads = rel_bias.shape
    blk = min(rows, BIAS_BLOCK_ROWS)
    return pl.pallas_call(
        functools.partial(_band_bias_kernel, n_real=n_real),
        grid=(len(offsets), rows // blk, cols // LANES),
        in_specs=[pl.BlockSpec(memory_space=pltpu.SMEM), pl.BlockSpec(memory_space=pltpu.SMEM)],
        out_specs=pl.BlockSpec((1, n_heads, blk, LANES), lambda o, t, c: (o, 0, t, c)),
        out_shape=jax.ShapeDtypeStruct((len(offsets), n_heads, rows, cols), F32),
        compiler_params=_cparams("arbitrary", "arbitrary", "arbitrary"),
        name="band_bias")(rel_bias.T, jnp.asarray(offsets, jnp.int32))


def _head_masks(shape):
    lane = lax.broadcasted_iota(jnp.int32, shape, len(shape) - 1)
    return [(lane % LANES) // HEAD_DIM == j for j in range(HEADS_PER_VREG)]


def _ones_lanes():
    return [((j + 1) % HEADS_PER_VREG) * HEAD_DIM for j in range(HEADS_PER_VREG)]


def _masked_values(v, j):
    lane = lax.broadcasted_iota(jnp.int32, v.shape, 1) % LANES
    vj = jnp.where(lane // HEAD_DIM == j, v, jnp.zeros((), BF16))
    return jnp.where(lane == _ones_lanes()[j], jnp.ones((), BF16), vj)


def _band_attn_kernel(q_ref, k_ref, v_ref, bias_ref, o_ref, v16, s_scr, p_scr, *, tq, band):
    step = pl.program_id(1)
    ck = KEY_CHUNK
    rows = pl.ds(pl.multiple_of(step * tq, tq), tq)

    @pl.when(step == 0)
    def _():
        v = v_ref[0]
        for j in range(HEADS_PER_VREG):
            v16[j] = _masked_values(v, j)

    start = pl.multiple_of(jnp.maximum(step * (tq // CHUNK) - A_PAST_CHUNKS, 0) * CHUNK, CHUNK)
    qmask = _head_masks((tq, LANES))
    ones_lane = _ones_lanes()
    cols = [slice(c * ck, (c + 1) * ck) for c in range(band // ck)]
    heads = [(hp, j) for hp in range(q_ref.shape[2] // LANES) for j in range(HEADS_PER_VREG)]
    lanes = [slice(hp * LANES, (hp + 1) * LANES) for hp, _ in heads]
    qs = [jnp.where(qmask[j], q_ref[0, rows, lanes[n]], jnp.zeros((), BF16))
          for n, (_, j) in enumerate(heads)]

    def logits(n, c, mrun):
        k = k_ref[0, pl.ds(pl.multiple_of(start + c * ck, CHUNK), ck), lanes[n]]
        s = _dot_nt(qs[n], k) + bias_ref[0, n, :, cols[c]]
        s_scr[n % 2, :, cols[c]] = s
        for part in range(ck // LANES):
            mrun = jnp.maximum(mrun, s[:, part * LANES:(part + 1) * LANES])
        return mrun

    def probs(n, c, row_max):
        p_scr[n % 2, :, cols[c]] = jnp.exp2(s_scr[n % 2, :, cols[c]] - row_max).astype(BF16)

    def pv(n):
        return _dot(p_scr[n % 2], v16[heads[n][1], pl.ds(start, band), lanes[n]])

    neg = jnp.full((tq, LANES), NEG_INF, F32)
    outs = []
    mrun = functools.reduce(lambda m, c: logits(0, c, m), range(len(cols)), neg)
    for n in range(len(heads)):
        row_max = jnp.max(mrun, axis=-1, keepdims=True)
        mrun = neg
        for c in range(len(cols)):
            if n + 1 < len(heads):
                mrun = logits(n + 1, c, mrun)
            probs(n, c, row_max)
        o = pv(n)
        lane = ones_lane[heads[n][1]]
        outs.append(o / o[:, lane:lane + 1])
    for n in range(0, len(heads), HEADS_PER_VREG):
        o_ref[0, rows, lanes[n]] = jnp.where(qmask[0], outs[n], outs[n + 1]).astype(BF16)


def band_attn(q, k, v, bias, *, tq, band):
    b, sq, w = q.shape
    sk = k.shape[1]
    n_off, n_heads = bias.shape[:2]
    assert band % KEY_CHUNK == 0 and tq % CHUNK == 0 or sq == tq
    kern = functools.partial(_band_attn_kernel, tq=tq, band=band)
    return pl.pallas_call(
        kern, grid=(b, sq // tq),
        in_specs=[pl.BlockSpec((1, sq, w), lambda i, c: (i, 0, 0)),
                  pl.BlockSpec((1, sk, w), lambda i, c: (i, 0, 0)),
                  pl.BlockSpec((1, sk, w), lambda i, c: (i, 0, 0)),
                  pl.BlockSpec((1, n_heads, tq, band),
                               lambda i, c: (jnp.minimum(c, n_off - 1), 0, 0, 0))],
        out_specs=pl.BlockSpec((1, sq, w), lambda i, c: (i, 0, 0)),
        out_shape=jax.ShapeDtypeStruct((b, sq, w), BF16),
        scratch_shapes=[pltpu.VMEM((HEADS_PER_VREG, sk, w), BF16),
                        pltpu.VMEM((2, tq, band), F32),
                        pltpu.VMEM((2, tq, band), BF16)],
        compiler_params=_cparams("parallel", "arbitrary"), name="band_attn")(q, k, v, bias)


STICK_UNDERFLOW_LOG2 = 160.0
STICK_NEAR_CHUNKS = 2


def _stick_kernel(q_ref, k_ref, v_ref, o_ref, v16, zl_scr, hl_scr, w_scr, *,
                  tq, q_offset, variants):
    qi = pl.program_id(2)
    ck = KEY_CHUNK
    heads = [(pp, j) for pp in range(q_ref.shape[2] // LANES) for j in range(HEADS_PER_VREG)]
    n_heads = len(heads)
    lanes = [slice(pp * LANES, (pp + 1) * LANES) for pp, _ in heads]

    @pl.when(qi == 0)
    def _():
        v = v_ref[0]
        vmask = _head_masks(v.shape)
        for j in range(HEADS_PER_VREG):
            v16[j] = jnp.where(vmask[j], v, jnp.zeros((), BF16))

    rows = pl.ds(pl.multiple_of(qi * tq, tq), tq)
    qmask = _head_masks((tq, LANES))
    qs = [jnp.where(qmask[j], q_ref[0, rows, lanes[n]], jnp.zeros((), BF16))
          for n, (_, j) in enumerate(heads)]
    q_pos = q_offset + qi * tq + lax.broadcasted_iota(jnp.int32, (tq, ck), 0)
    k_iota = lax.broadcasted_iota(jnp.int32, (tq, ck), 1)
    rr = lax.broadcasted_iota(jnp.int32, (2 * ck, ck), 0) % ck
    cc = lax.broadcasted_iota(jnp.int32, (2 * ck, ck), 1)
    tri2 = jnp.where(rr > cc, 1.0, 0.0).astype(BF16)
    n_chunks = (q_offset + (qi + 1) * tq + ck - 1) // ck

    def split_cols(slot):
        return slice(2 * slot.start, 2 * slot.start + ck), slice(2 * slot.start + ck, 2 * slot.stop)

    def logits(j, slot, c, masked):
        k0 = pl.multiple_of(c * ck, ck)
        z = _dot_nt(qs[j], k_ref[0, pl.ds(k0, ck), lanes[j]])
        neg_abs = lax.bitcast_convert_type(
            lax.bitcast_convert_type(z, jnp.int32) | jnp.int32(-2 ** 31), F32)
        sp = jnp.maximum(z, 0.0) + jnp.log2(1.0 + jnp.exp2(neg_abs))
        if masked:
            sp = jnp.where(k0 + k_iota < q_pos, sp, 0.0)
        zl_scr[j, :, slot] = z - sp
        hi_cols, lo_cols = split_cols(slot)
        hl_scr[j, :, hi_cols], hl_scr[j, :, lo_cols] = _split2(sp)
        return jnp.sum(sp, axis=-1, keepdims=True)

    def weights(j, slot, c, later, masked):
        sums = _dot(hl_scr[j, :, 2 * slot.start:2 * slot.stop], tri2)
        w = jnp.exp2(zl_scr[j, :, slot] - sums - later)
        if masked:
            w = jnp.where(c * ck + k_iota < q_pos, w, 0.0)
        w_scr[j, :, slot] = w.astype(BF16)

    def more(laters):
        return (jnp.min(functools.reduce(jnp.minimum, laters))
                < STICK_UNDERFLOW_LOG2).astype(jnp.int32)

    def attend(near, n_masked):
        slots = [slice((near - 1 - i) * ck, (near - i) * ck) for i in range(near)]
        chunk = [n_chunks - 1 - i for i in range(near)]
        k0 = pl.multiple_of((n_chunks - near) * ck, ck)

        def pv(n, k0, width):
            return _dot(w_scr[n, :, :width], v16[heads[n][1], pl.ds(k0, width), lanes[n]])

        laters, accs = [], []
        row_sums = [logits(0, slots[i], chunk[i], i < n_masked) for i in range(near)]
        for n in range(n_heads):
            later = jnp.zeros((tq, 1), F32)
            next_sums = []
            for i in range(near):
                if n + 1 < n_heads:
                    next_sums.append(logits(n + 1, slots[i], chunk[i], i < n_masked))
                weights(n, slots[i], chunk[i], later, i < n_masked)
                later = later + row_sums[i]
            laters.append(later)
            row_sums = next_sums
            o = pv(n, k0, near * ck)
            if heads[n][1] == 0:
                accs.append(o)
            else:
                accs[-1] = accs[-1] + o

        def cond(carry):
            return (carry[0] >= 0) & (carry[1] > 0)

        def body(carry):
            c, _, accs, laters = carry
            accs = list(accs)
            k0 = pl.multiple_of(c * ck, ck)
            new = []
            for n in range(n_heads):
                rs = logits(n, slice(0, ck), c, False)
                weights(n, slice(0, ck), c, laters[n], False)
                accs[heads[n][0]] = accs[heads[n][0]] + pv(n, k0, ck)
                new.append(laters[n] + rs)
            return c - 1, more(new), tuple(accs), tuple(new)

        carry = lax.while_loop(cond, body,
                               (n_chunks - 1 - near, more(laters), tuple(accs), tuple(laters)))
        for pp, acc in enumerate(carry[2]):
            o_ref[0, rows, pp * LANES:(pp + 1) * LANES] = acc.astype(BF16)

    for n_total, near, n_masked in variants:
        if n_total is None:
            pl.when(n_chunks >= near)(functools.partial(attend, near, n_masked))
        else:
            pl.when(n_chunks == n_total)(functools.partial(attend, near, n_masked))


def stick_attn(q, k, v, *, tq, q_offset):
    b, sq, w = q.shape
    sk = k.shape[1]
    ck = KEY_CHUNK
    assert q_offset % ck == 0 and (tq % ck == 0 or sq == tq <= ck) and sk % ck == 0
    n_masked = -(-tq // ck)
    totals = sorted({-(-(q_offset + (t + 1) * tq) // ck) for t in range(sq // tq)})
    variants = [(n, n, min(n_masked, n)) for n in totals if n < STICK_NEAR_CHUNKS]
    if totals[-1] >= STICK_NEAR_CHUNKS:
        variants.append((None, STICK_NEAR_CHUNKS, n_masked))
    near_cols = STICK_NEAR_CHUNKS * ck
    wg = ATTN_PAIRS_PER_STEP * LANES
    n_heads = ATTN_PAIRS_PER_STEP * HEADS_PER_VREG
    kern = functools.partial(_stick_kernel, tq=tq, q_offset=q_offset, variants=tuple(variants))
    return pl.pallas_call(
        kern, grid=(b, w // wg, sq // tq),
        in_specs=[pl.BlockSpec((1, sq, wg), lambda i, h, t: (i, 0, h)),
                  pl.BlockSpec((1, sk, wg), lambda i, h, t: (i, 0, h)),
                  pl.BlockSpec((1, sk, wg), lambda i, h, t: (i, 0, h))],
        out_specs=pl.BlockSpec((1, sq, wg), lambda i, h, t: (i, 0, h)),
        out_shape=jax.ShapeDtypeStruct((b, sq, w), BF16),
        scratch_shapes=[pltpu.VMEM((HEADS_PER_VREG, sk, wg), BF16),
                        pltpu.VMEM((n_heads, tq, near_cols), F32),
                        pltpu.VMEM((n_heads, tq, 2 * near_cols), BF16),
                        pltpu.VMEM((n_heads, tq, near_cols), BF16)],
        compiler_params=_cparams("parallel", "parallel", "arbitrary"), name="stick_attn")(q, k, v)


def _cumsum_kernel(lf_ref, lft_ref, col_ref, row_ref, *, blk):
    s_len, n_heads = lf_ref.shape[1], lf_ref.shape[2]
    rr = lax.broadcasted_iota(jnp.int32, (blk, blk), 0)
    cc = lax.broadcasted_iota(jnp.int32, (blk, blk), 1)
    lower = jnp.where(rr >= cc, 1.0, 0.0).astype(BF16)
    upper = jnp.where(rr <= cc, 1.0, 0.0).astype(BF16)
    blocks = [slice(n * blk, (n + 1) * blk) for n in range(s_len // blk)]
    local_col = [sum(_dot(lower, p) for p in _split3(lf_ref[0, rows, :])) for rows in blocks]
    local_row = [sum(_dot(p, upper) for p in _split3(lft_ref[0, :, rows])) for rows in blocks]
    carry_col = jnp.zeros((1, n_heads), F32)
    carry_row = jnp.zeros((n_heads, 1), F32)
    for rows, c, ct in zip(blocks, local_col, local_row):
        col_ref[0, rows, :] = c + carry_col
        row_ref[0, :, rows] = ct + carry_row
        carry_col = carry_col + c[blk - 1:blk, :]
        carry_row = carry_row + ct[:, blk - 1:blk]


def cumsum_logf(lf):
    b, s_len, n_heads = lf.shape
    kern = functools.partial(_cumsum_kernel, blk=LANES)
    return pl.pallas_call(
        kern, grid=(b,),
        in_specs=[pl.BlockSpec((1, s_len, n_heads), lambda i: (i, 0, 0)),
                  pl.BlockSpec((1, n_heads, s_len), lambda i: (i, 0, 0))],
        out_specs=[pl.BlockSpec((1, s_len, n_heads), lambda i: (i, 0, 0)),
                   pl.BlockSpec((1, n_heads, s_len), lambda i: (i, 0, 0))],
        out_shape=[jax.ShapeDtypeStruct((b, s_len, n_heads), F32),
                   jax.ShapeDtypeStruct((b, n_heads, s_len), F32)],
        compiler_params=_cparams("parallel"), name="cumsum_logf")(lf, jnp.swapaxes(lf, 1, 2))


FORGET_SKIP_LOG2 = 152.0
FORGET_BOUND_SLACK = 1.02
FORGET_BOUND_MARGIN = 2.0


def _forget_skip_kernel(qmax_ref, kmax_ref, cum_ref, o_ref, *, heads_per_group):
    n_blk, n_heads = kmax_ref.shape[1], kmax_ref.shape[2]
    blk = cum_ref.shape[1] // n_blk
    kmax = kmax_ref[0]
    k_term = kmax + jnp.max(kmax, axis=0, keepdims=True)
    cum_end = cum_ref[0, pl.ds(blk - 1, n_blk, stride=blk), :]
    cum_start = cum_ref[0, pl.ds(0, n_blk, stride=blk), :]
    chunk_id = lax.broadcasted_iota(jnp.int32, (n_blk, 1), 0)
    head_id = lax.broadcasted_iota(jnp.int32, (n_blk, n_heads), 1)
    row = lax.broadcasted_iota(jnp.int32, o_ref.shape[1:], 0)
    col = lax.broadcasted_iota(jnp.int32, o_ref.shape[1:], 1)
    table = jnp.zeros(o_ref.shape[1:], F32)
    for qi in range(n_blk):
        bound = (FORGET_BOUND_SLACK * qmax_ref[0, qi:qi + 1, :] * k_term
                 + LOG2E * (cum_start[qi:qi + 1, :] - cum_end) + FORGET_BOUND_MARGIN)
        for g in range(n_heads // heads_per_group):
            worst = jnp.max(jnp.where(head_id // heads_per_group == g, bound, NEG_INF),
                            axis=1, keepdims=True)
            needed = jnp.logical_not(worst < -FORGET_SKIP_LOG2) | (chunk_id >= qi)
            first = jnp.min(jnp.where(needed, chunk_id, n_blk).astype(F32), axis=0, keepdims=True)
            table = jnp.where((row == qi) & (col == g), first, table)
    o_ref[0] = table.astype(jnp.int32)


def forget_skip_table(qmax, kmax, cum, *, heads_per_group):
    b, n_blk, n_heads = kmax.shape
    s_len = cum.shape[1]
    blk3 = lambda shape: pl.BlockSpec((1,) + shape, lambda i: (i, 0, 0))
    return pl.pallas_call(
        functools.partial(_forget_skip_kernel, heads_per_group=heads_per_group), grid=(b,),
        in_specs=[blk3((n_blk, n_heads)), blk3((n_blk, n_heads)), blk3((s_len, n_heads))],
        out_specs=blk3((n_blk, LANES)),
        out_shape=jax.ShapeDtypeStruct((b, n_blk, LANES), jnp.int32),
        compiler_params=_cparams("parallel"), name="forget_skip_table")(qmax, kmax, cum)


def _forget_kernel(c0_ref, q_ref, k_ref, v_ref, cq_ref, ck_ref, o_ref, v16, s_scr, p_scr, *,
                   tq, q_offset, max_chunks):
    group = pl.program_id(1)
    qi = pl.program_id(2)
    wg = k_ref.shape[2]
    ck_w = KEY_CHUNK
    ones_lane = _ones_lanes()
    heads = [(pp, j) for pp in range(wg // LANES) for j in range(HEADS_PER_VREG)]
    lanes = [slice(pp * LANES, (pp + 1) * LANES) for pp, _ in heads]

    @pl.when(qi == 0)
    def _():
        v = v_ref[0]
        for j in range(HEADS_PER_VREG):
            v16[j] = _masked_values(v, j)

    qmask = _head_masks((tq, LANES))
    q_pos = q_offset + qi * tq + lax.broadcasted_iota(jnp.int32, (tq, ck_w), 0)
    k_iota = lax.broadcasted_iota(jnp.int32, (tq, ck_w), 1)
    rows = pl.ds(pl.multiple_of(qi * tq, tq), tq)
    cq_all = cq_ref[0, rows, :]
    head_lane = lax.broadcasted_iota(jnp.int32, cq_all.shape, 1)
    n_c = (q_offset + (qi + 1) * tq + ck_w - 1) // ck_w
    step = (pl.program_id(0) * pl.num_programs(1) + group) * pl.num_programs(2) + qi
    c0 = jnp.minimum(c0_ref[step], n_c - 1)
    n_proc_here = n_c - c0

    def attend(n_proc):
        cols = [slice(c * ck_w, (c + 1) * ck_w) for c in range(n_proc)]
        k0 = [pl.multiple_of((c0 + c) * ck_w, ck_w) for c in range(n_proc)]
        allowed = k0[-1] + k_iota <= q_pos
        qs = [jnp.where(qmask[j], q_ref[0, rows, lanes[n]], jnp.zeros((), BF16))
              for n, (_, j) in enumerate(heads)]
        cq = [LOG2E * jnp.sum(jnp.where(head_lane == group * len(heads) + n, cq_all, 0.0),
                              axis=-1, keepdims=True) for n in range(len(heads))]

        def logits(n, c, mrun):
            s = (_dot_nt(qs[n], k_ref[0, pl.ds(k0[c], ck_w), lanes[n]])
                 - LOG2E * ck_ref[0, 0, n:n + 1, pl.ds(k0[c], ck_w)])
            if c == n_proc - 1:
                s = jnp.where(allowed, s, NEG_INF)
            s_scr[n % 2, :, cols[c]] = s
            for part in range(ck_w // LANES):
                mrun = jnp.maximum(mrun, s[:, part * LANES:(part + 1) * LANES])
            return mrun

        def row_term(n, mrun):
            m = jnp.max(mrun, axis=-1, keepdims=True) + cq[n]
            return cq[n] - m

        def probs(n, c, row):
            p_scr[n % 2, :, cols[c]] = jnp.exp2(s_scr[n % 2, :, cols[c]] + row).astype(BF16)

        def pv(n):
            return _dot(p_scr[n % 2, :, :n_proc * ck_w],
                        v16[heads[n][1], pl.ds(k0[0], n_proc * ck_w), lanes[n]])

        neg = jnp.full((tq, LANES), NEG_INF, F32)
        outs = []
        mrun = functools.reduce(lambda m, c: logits(0, c, m), range(n_proc), neg)
        for n in range(len(heads)):
            row = row_term(n, mrun)
            mrun = neg
            for c in range(n_proc):
                if n + 1 < len(heads):
                    mrun = logits(n + 1, c, mrun)
                probs(n, c, row)
            o = pv(n)
            lane = ones_lane[heads[n][1]]
            outs.append(o / o[:, lane:lane + 1])
        for n in range(0, len(heads), HEADS_PER_VREG):
            o_ref[0, rows, lanes[n]] = jnp.where(qmask[0], outs[n], outs[n + 1]).astype(BF16)

    for n_proc in range(1, max_chunks + 1):
        pl.when(n_proc_here == n_proc)(functools.partial(attend, n_proc))


def forget_attn(q, k, v, cum_q, cum_k_rows, skip, *, tq, q_offset):
    b, sq, w = q.shape
    sk = k.shape[1]
    n_heads = cum_q.shape[2]
    ck_w = KEY_CHUNK
    diag = [-(-(q_offset + (t + 1) * tq) // ck_w) - 1 for t in range(sq // tq)]
    assert sk % ck_w == 0 and all((q_offset + t * tq) // ck_w == c for t, c in enumerate(diag))
    max_chunks = max(diag) + 1
    wg = ATTN_PAIRS_PER_STEP * LANES
    heads_per_step = ATTN_PAIRS_PER_STEP * HEADS_PER_VREG
    assert skip.shape == (b, w // wg, sq // tq)
    ck = cum_k_rows.reshape(b, n_heads // heads_per_step, heads_per_step, sk)
    kern = functools.partial(_forget_kernel, tq=tq, q_offset=q_offset, max_chunks=max_chunks)
    grid_spec = pltpu.PrefetchScalarGridSpec(
        num_scalar_prefetch=1, grid=(b, w // wg, sq // tq),
        in_specs=[pl.BlockSpec((1, sq, wg), lambda i, h, t, c0: (i, 0, h)),
                  pl.BlockSpec((1, sk, wg), lambda i, h, t, c0: (i, 0, h)),
                  pl.BlockSpec((1, sk, wg), lambda i, h, t, c0: (i, 0, h)),
                  pl.BlockSpec((1, sq, n_heads), lambda i, h, t, c0: (i, 0, 0)),
                  pl.BlockSpec((1, 1, heads_per_step, sk), lambda i, h, t, c0: (i, h, 0, 0))],
        out_specs=pl.BlockSpec((1, sq, wg), lambda i, h, t, c0: (i, 0, h)),
        scratch_shapes=[pltpu.VMEM((HEADS_PER_VREG, sk, wg), BF16),
                        pltpu.VMEM((2, tq, max_chunks * ck_w), F32),
                        pltpu.VMEM((2, tq, max_chunks * ck_w), BF16)])
    return pl.pallas_call(
        kern, grid_spec=grid_spec, out_shape=jax.ShapeDtypeStruct((b, sq, w), BF16),
        compiler_params=_cparams("parallel", "parallel", "arbitrary"),
        name="forget_attn")(skip.reshape(-1), q, k, v, cum_q, ck)


def _layer_tail_kernel(*refs, n_in, n_heads, rows_per_seq, tf, final_norm):
    a_refs, w_refs = refs[:n_in], refs[n_in:2 * n_in]
    (x_ref, gx_ref, wq_ref, wo_ref, mk_ref, mv_ref, gf_ref, wg_ref, wu_ref, wd_ref, gfin_ref,
     o_ref) = refs[2 * n_in:]
    x = x_ref[...]
    for a_ref, w_ref in zip(a_refs, w_refs):
        x = x + _dot(a_ref[...], w_ref[...])

    hd = x.shape[1] // n_heads
    q = (_dot(_rms_bf16(x, gx_ref[...]), wq_ref[...]) * (hd ** -0.5)).astype(BF16)
    per_seq = []
    for b in range(mk_ref.shape[0]):
        rows = slice(b * rows_per_seq, (b + 1) * rows_per_seq)
        outs = []
        for j in range(n_heads):
            cols = slice(j * hd, (j + 1) * hd)
            s = _dot_nt(q[rows, cols], mk_ref[b, :, cols])
            p = jnp.exp(s - jnp.max(s, axis=-1, keepdims=True))
            p = p / jnp.sum(p, axis=-1, keepdims=True)
            outs.append(_dot(p.astype(BF16), mv_ref[b, :, cols]).astype(BF16))
        per_seq.append(jnp.concatenate(outs, axis=-1))
    x = x + _dot(jnp.concatenate(per_seq, axis=0), wo_ref[...])

    h = _rms_bf16(x, gf_ref[...])
    for c in range(wg_ref.shape[1] // tf):
        cols = slice(c * tf, (c + 1) * tf)
        gate = _dot(h, wg_ref[:, cols])
        up = _dot(h, wu_ref[:, cols])
        a = (gate * jax.nn.sigmoid(gate) * up).astype(BF16)
        x = x + _dot(a, wd_ref[cols, :])
    if final_norm:
        x = x * lax.rsqrt(jnp.mean(x * x, axis=-1, keepdims=True) + RMS_EPS) * gfin_ref[...]
    o_ref[...] = x


def layer_tail(a_list, w_list, x, g_x, wq, wo, mk, mv, g_f, wg, wu, wd, g_final, *, seq, tm, tf,
               final_norm):
    m, d = x.shape
    n_mem = mk.shape[1]
    rows_per_seq = min(tm, seq)
    seqs = tm // rows_per_seq
    assert seq % rows_per_seq == 0 and m % tm == 0
    row = lambda i: (i, 0)
    mem = lambda i: (i * tm // (seq * seqs), 0, 0)
    gain = lambda g: g.reshape(1, d)
    kern = functools.partial(_layer_tail_kernel, n_in=len(a_list), n_heads=XA_HEADS,
                             rows_per_seq=rows_per_seq, tf=tf, final_norm=final_norm)
    in_specs = ([pl.BlockSpec((tm, a.shape[1]), row) for a in a_list]
                + [_resident_spec(w.shape) for w in w_list]
                + [pl.BlockSpec((tm, d), row), _resident_spec((1, d)),
                   _resident_spec(wq.shape), _resident_spec(wo.shape),
                   pl.BlockSpec((seqs, n_mem, d), mem), pl.BlockSpec((seqs, n_mem, d), mem),
                   _resident_spec((1, d)), _resident_spec(wg.shape), _resident_spec(wu.shape),
                   _resident_spec(wd.shape), _resident_spec((1, d))])
    return pl.pallas_call(
        kern, grid=(m // tm,), in_specs=in_specs,
        out_specs=pl.BlockSpec((tm, d), row), out_shape=jax.ShapeDtypeStruct((m, d), F32),
        compiler_params=_cparams("parallel"), name="layer_tail")(
            *a_list, *w_list, x, gain(g_x), wq, wo, mk, mv, gain(g_f), wg, wu, wd, gain(g_final))


def _pad_rows(a, rows):
    return jnp.pad(a, ((0, 0), (0, rows - a.shape[1]), (0, 0)))


def _row_tile(m, seq, cap):
    tm = min(cap, seq)
    assert seq % tm == 0 and m % tm == 0
    return tm


def _layer_ab(x, seq, g, w_in, w_out, rel_bias, cache, *, tm):
    m, d = x.shape
    b = m // seq
    wa = w_in.shape[1] // 6
    ws = [w_in[:, n * wa:(n + 1) * wa].astype(BF16) for n in range(6)]
    scale = HEAD_DIM ** -0.5 * LOG2E
    kept = 'tail' if min(A_PAST, seq) < seq else 'heads'
    outs = [(4, 'heads', HEAD_DIM), (4, 'bf16', 1.0), (5, 'heads', HEAD_DIM), (5, 'bf16', 1.0),
            (0, 'bf16', scale), (3, 'bf16', scale), (1, 'bf16', 1.0), (2, 'bf16', 1.0),
            (1, kept, HEAD_DIM), (2, kept, HEAD_DIM)]
    kb, kb16, vb, vb16, qa, qb, ka16, va16, ka_keep, va_keep = norm_proj(x, g, ws, outs, seq=seq,
                                                                         tm=tm)
    shp = lambda a: a.reshape(b, -1, a.shape[-1])
    if cache is None:
        tq = BAND_ROWS
        band = -(-(A_PAST + tq) // KEY_CHUNK) * KEY_CHUNK
        offsets = sorted({t * tq - max(t * (tq // CHUNK) - A_PAST_CHUNKS, 0) * CHUNK
                          for t in range(seq // tq)})
        bias = band_bias(rel_bias, offsets, rows=tq, cols=band, n_real=band)
        oa = band_attn(shp(qa), shp(ka16), shp(va16), bias, tq=tq, band=band)
        ob = stick_attn(shp(qb), shp(kb16), shp(vb16), tq=KEY_CHUNK, q_offset=0)
    else:
        ca_k, ca_v, cb_k, cb_v = cache
        n_past = ca_k.shape[1]
        n_keys = n_past + seq
        band = -(-n_keys // KEY_CHUNK) * KEY_CHUNK
        flat = lambda a: a.reshape(a.shape[0], a.shape[1], -1)
        k_all = _pad_rows(jnp.concatenate([flat(ca_k).astype(BF16), shp(ka16)], axis=1), band)
        v_all = _pad_rows(jnp.concatenate([flat(ca_v).astype(BF16), shp(va16)], axis=1), band)
        bias = band_bias(rel_bias, [n_past], rows=seq, cols=band, n_real=n_keys)
        oa = band_attn(shp(qa), k_all, v_all, bias, tq=seq, band=band)
        n_pastb = cb_k.shape[1]
        sk = -(-(n_pastb + seq) // KEY_CHUNK) * KEY_CHUNK
        kb_all = _pad_rows(jnp.concatenate([flat(cb_k).astype(BF16), shp(kb16)], axis=1), sk)
        vb_all = _pad_rows(jnp.concatenate([flat(cb_v).astype(BF16), shp(vb16)], axis=1), sk)
        ob = stick_attn(shp(qb), kb_all, vb_all, tq=seq, q_offset=n_pastb)
    w_out = w_out.astype(BF16)
    wo_a, wo_b = w_out[:oa.shape[-1]], w_out[oa.shape[-1]:]
    return ([oa.reshape(m, -1), ob.reshape(m, -1)], [wo_a, wo_b]), (ka_keep, va_keep, kb, vb)


def _layer_c(x, seq, g, w_in, b_f, w_out, cache, *, tm):
    m, d = x.shape
    b = m // seq
    n_heads = b_f.shape[0]
    wq, wk, wv, wf = (w_in[:, :d], w_in[:, d:2 * d], w_in[:, 2 * d:3 * d], w_in[:, 3 * d:])
    ws = [w.astype(BF16) for w in (wq, wk, wv, wf)]
    scale = HEAD_DIM ** -0.5 * LOG2E
    outs = [(1, 'heads', HEAD_DIM), (1, 'bf16', 1.0), (2, 'heads', HEAD_DIM), (2, 'bf16', 1.0),
            (0, 'bf16', scale), (3, 'logf', None)]
    if cache is None:
        outs += [(1, 'rowmax', (HEAD_DIM, 1.0)), (0, 'rowmax', (HEAD_DIM, scale))]
    k, k16, v, v16, q, lf, *norms = norm_proj(x, g, ws, outs, seq=seq, tm=tm,
                                              bias=b_f.reshape(1, n_heads))
    norms = norms[::-1]
    shp = lambda a: a.reshape(b, -1, a.shape[-1])
    heads_per_group = ATTN_PAIRS_PER_STEP * HEADS_PER_VREG
    n_groups = n_heads // heads_per_group
    if cache is None:
        cum_col, cum_row = cumsum_logf(shp(lf))
        tq = KEY_CHUNK
        qmax, kmax = (a.reshape(b, seq // KEY_CHUNK, n_heads) for a in norms)
        skip = forget_skip_table(qmax, kmax, cum_col, heads_per_group=heads_per_group)
        skip = jnp.swapaxes(skip[:, :, :n_groups], 1, 2)
        o = forget_attn(shp(q), shp(k16), shp(v16), cum_col, cum_row, skip, tq=tq, q_offset=0)
    else:
        c_k, c_v, c_lf = cache
        n_past = c_k.shape[1]
        sk = -(-(n_past + seq) // KEY_CHUNK) * KEY_CHUNK
        flat = lambda a: a.reshape(a.shape[0], a.shape[1], -1)
        k_all = _pad_rows(jnp.concatenate([flat(c_k).astype(BF16), shp(k16)], axis=1), sk)
        v_all = _pad_rows(jnp.concatenate([flat(c_v).astype(BF16), shp(v16)], axis=1), sk)
        lf_all = _pad_rows(jnp.concatenate([c_lf, shp(lf)], axis=1), sk)
        cum_col, cum_row = cumsum_logf(lf_all)
        o = forget_attn(shp(q), k_all, v_all, cum_col[:, n_past:n_past + seq], cum_row,
                        jnp.zeros((b, n_groups, 1), jnp.int32), tq=seq, q_offset=n_past)
    return ([o.reshape(m, d)], [w_out.astype(BF16)]), (k, v, lf)


def kernel(x_prompt, x_sample, cache_a_k, cache_a_v, cache_b_k, cache_b_v, cache_c_k, cache_c_v, cache_c_logf, cache_mem_k, cache_mem_v, mem_prompt, w_in_ab, w_out_ab, rel_bias_a, w_in_c, b_f_c, w_out_c, g_mix, g_xattn, g_mem, w_xq, w_xk, w_xv, w_xo, g_ffn, w_gate, w_up, w_down, g_final):
    bp, sp, d = x_prompt.shape
    bs, ss, _ = x_sample.shape
    depth = g_mix.shape[0]
    n_mem = mem_prompt.shape[1]
    xp = x_prompt.reshape(bp * sp, d)
    xs = x_sample.reshape(bs * ss, d)
    tmp = _row_tile(bp * sp, sp, ROW_TILE)
    tms = bs * ss
    assert tms <= ROW_TILE
    mem = mem_prompt.reshape(bp * n_mem, d)
    tmm = min(ROW_TILE, bp * n_mem)
    assert (bp * n_mem) % tmm == 0
    dff = w_gate.shape[2]
    tf = 256 if dff % 256 == 0 else dff

    a_kp, a_vp, b_kp, b_vp, a_ks, a_vs, b_ks, b_vs = [], [], [], [], [], [], [], []
    c_kp, c_vp, c_lfp, c_ks, c_vs, c_lfs = [], [], [], [], [], []
    mem_kp, mem_vp = [], []
    for layer in range(depth):
        if layer % 2 == 0:
            e = layer // 2
            mix_p, (ka, va, kb, vb) = _layer_ab(xp, sp, g_mix[layer], w_in_ab[e], w_out_ab[e],
                                                rel_bias_a[e], None, tm=tmp)
            a_kp.append(ka); a_vp.append(va); b_kp.append(kb); b_vp.append(vb)
            mix_s, (ka, va, kb, vb) = _layer_ab(
                xs, ss, g_mix[layer], w_in_ab[e], w_out_ab[e], rel_bias_a[e],
                (cache_a_k[e], cache_a_v[e], cache_b_k[e], cache_b_v[e]), tm=tms)
            a_ks.append(ka); a_vs.append(va); b_ks.append(kb); b_vs.append(vb)
        else:
            c = layer // 2
            mix_p, (k, v, lf) = _layer_c(xp, sp, g_mix[layer], w_in_c[c], b_f_c[c], w_out_c[c],
                                         None, tm=tmp)
            c_kp.append(k); c_vp.append(v); c_lfp.append(lf)
            mix_s, (k, v, lf) = _layer_c(xs, ss, g_mix[layer], w_in_c[c], b_f_c[c], w_out_c[c],
                                         (cache_c_k[c], cache_c_v[c], cache_c_logf[c]), tm=tms)
            c_ks.append(k); c_vs.append(v); c_lfs.append(lf)
        mk, mk16, mv, mv16 = norm_proj(
            mem, g_mem[layer], [w_xk[layer].astype(BF16), w_xv[layer].astype(BF16)],
            [(0, 'heads', d // XA_HEADS), (0, 'bf16', 1.0), (1, 'heads', d // XA_HEADS),
             (1, 'bf16', 1.0)],
            seq=n_mem, tm=tmm)
        mem_kp.append(mk); mem_vp.append(mv)
        tail = functools.partial(
            layer_tail, g_x=g_xattn[layer], wq=w_xq[layer].astype(BF16),
            wo=w_xo[layer].astype(BF16), g_f=g_ffn[layer], wg=w_gate[layer].astype(BF16),
            wu=w_up[layer].astype(BF16), wd=w_down[layer].astype(BF16), g_final=g_final, tf=tf,
            final_norm=layer == depth - 1)
        xp = tail(*mix_p, xp, mk=mk16.reshape(bp, n_mem, d), mv=mv16.reshape(bp, n_mem, d),
                  seq=sp, tm=tmp)
        xs = tail(*mix_s, xs, mk=cache_mem_k[layer].reshape(bs, n_mem, d).astype(BF16),
                  mv=cache_mem_v[layer].reshape(bs, n_mem, d).astype(BF16), seq=ss, tm=tms)

    hd = HEAD_DIM
    xa_hd = d // XA_HEADS
    r5 = lambda lst, b, s, dd: jnp.stack([a.reshape(b, s, -1, dd) for a in lst])
    r4 = lambda lst, b, s: jnp.stack([a.reshape(b, s, -1) for a in lst])
    keep = min(A_PAST, sp)
    return (xp.reshape(bp, sp, d), xs.reshape(bs, ss, d),
            r5(a_kp, bp, keep, hd), r5(a_vp, bp, keep, hd), r5(b_kp, bp, sp, hd), r5(b_vp, bp, sp, hd),
            r5(c_kp, bp, sp, hd), r5(c_vp, bp, sp, hd), r4(c_lfp, bp, sp),
            r5(mem_kp, bp, n_mem, xa_hd), r5(mem_vp, bp, n_mem, xa_hd),
            r5(a_ks, bs, ss, hd), r5(a_vs, bs, ss, hd), r5(b_ks, bs, ss, hd), r5(b_vs, bs, ss, hd),
            r5(c_ks, bs, ss, hd), r5(c_vs, bs, ss, hd), r4(c_lfs, bs, ss))
```

```python
import functools

import jax
import jax.numpy as jnp
from jax import lax
from jax.experimental import pallas as pl
from jax.experimental.pallas import tpu as pltpu

F32 = jnp.float32
BF16 = jnp.bfloat16

RMS_EPS = 1e-6
NEG_INF = -1e30
LOG2E = 1.4426950408889634
HEAD_DIM = 64
CHUNK = 64
A_PAST_CHUNKS = 8
A_PAST = A_PAST_CHUNKS * CHUNK
REL_CLIP = 128
XA_HEADS = 4

LANES = 128
HEADS_PER_VREG = LANES // HEAD_DIM
KEY_CHUNK = 256
ATTN_PAIRS_PER_STEP = 4
BAND_ROWS = 4 * CHUNK
ROW_TILE = 512
VMEM_LIMIT = 56 * 1024 * 1024


def _cparams(*sem):
    return pltpu.CompilerParams(dimension_semantics=sem, vmem_limit_bytes=VMEM_LIMIT)


def _resident_spec(shape):
    return pl.BlockSpec(shape, lambda *_: (0,) * len(shape), pipeline_mode=pl.Buffered(1))


def _rms_bf16(x, g):
    y = x * lax.rsqrt(jnp.mean(x * x, axis=-1, keepdims=True) + RMS_EPS)
    return (y * g).astype(BF16)


def _log_sigmoid(z):
    return jnp.minimum(z, 0.0) - jnp.log1p(jnp.exp(-jnp.abs(z)))


def _split3(x):
    hi = x.astype(BF16)
    r = x - hi.astype(F32)
    mid = r.astype(BF16)
    lo = (r - mid.astype(F32)).astype(BF16)
    return hi, mid, lo


def _split2(x):
    hi = x.astype(BF16)
    lo = (x - hi.astype(F32)).astype(BF16)
    return hi, lo


def _dot(a, b):
    return jnp.dot(a, b, preferred_element_type=F32)


def _dot_nt(a, b):
    return lax.dot_general(a, b, (((1,), (1,)), ((), ())), preferred_element_type=F32)


def _norm_proj_kernel(x_ref, g_ref, b_ref, *refs, n_w, outs, tiles_per_seq):
    w_refs, o_refs = refs[:n_w], refs[n_w:]
    h = _rms_bf16(x_ref[...], g_ref[...])
    ys = {}
    for o_ref, (grp, kind, arg) in zip(o_refs, outs):
        if grp not in ys:
            ys[grp] = _dot(h, w_refs[grp][...])
        y = ys[grp]
        if kind == 'f32':
            o_ref[...] = y
        elif kind == 'bf16':
            o_ref[...] = (y * arg).astype(BF16)
        elif kind == 'logf':
            o_ref[...] = _log_sigmoid(y + b_ref[...])
        elif kind == 'heads':
            o_ref[...] = y.reshape(o_ref.shape)
        elif kind == 'rowmax':
            hd, scale = arg
            n_cols, n_heads = y.shape[1], y.shape[1] // hd
            sel_l = lax.broadcasted_iota(jnp.int32, (n_cols, n_heads), 0) // hd
            sel_h = lax.broadcasted_iota(jnp.int32, (n_cols, n_heads), 1)
            scaled = y * scale
            norms = jnp.sqrt(_dot((scaled * scaled).astype(BF16),
                                  jnp.where(sel_l == sel_h, 1.0, 0.0).astype(BF16)))
            for r in range(o_ref.shape[1]):
                o_ref[0, r:r + 1, :] = jnp.max(norms[r * KEY_CHUNK:(r + 1) * KEY_CHUNK],
                                               axis=0, keepdims=True)
        else:
            @pl.when(pl.program_id(0) % tiles_per_seq == tiles_per_seq - 1)
            def _(o_ref=o_ref, y=y):
                o_ref[...] = y.reshape(o_ref.shape)


def norm_proj(x, g, ws, outs, *, seq, tm, bias=None):
    m, d = x.shape
    tiles_per_seq = max(seq // tm, 1)
    if bias is None:
        bias = jnp.zeros((1, 16), F32)
    in_specs = [pl.BlockSpec((tm, d), lambda i: (i, 0)),
                pl.BlockSpec((1, d), lambda i: (0, 0)),
                pl.BlockSpec(bias.shape, lambda i: (0, 0))]
    in_specs += [_resident_spec(w.shape) for w in ws]
    out_shape, out_specs = [], []
    for grp, kind, arg in outs:
        n = ws[grp].shape[1]
        if kind == 'tail':
            assert tm == min(A_PAST, seq)
            out_shape.append(jax.ShapeDtypeStruct((m // tiles_per_seq, n // arg, arg), F32))
            out_specs.append(pl.BlockSpec((tm, n // arg, arg), lambda i: (i // tiles_per_seq, 0, 0)))
        elif kind == 'heads':
            out_shape.append(jax.ShapeDtypeStruct((m, n // arg, arg), F32))
            out_specs.append(pl.BlockSpec((tm, n // arg, arg), lambda i: (i, 0, 0)))
        elif kind == 'rowmax':
            assert tm % KEY_CHUNK == 0
            blocks = (tm // KEY_CHUNK, n // arg[0])
            out_shape.append(jax.ShapeDtypeStruct((m // tm,) + blocks, F32))
            out_specs.append(pl.BlockSpec((1,) + blocks, lambda i: (i, 0, 0)))
        else:
            out_shape.append(jax.ShapeDtypeStruct((m, n), BF16 if kind == 'bf16' else F32))
            out_specs.append(pl.BlockSpec((tm, n), lambda i: (i, 0)))
    kern = functools.partial(_norm_proj_kernel, n_w=len(ws), outs=tuple(outs),
                             tiles_per_seq=tiles_per_seq)
    return pl.pallas_call(
        kern, grid=(m // tm,), in_specs=in_specs, out_specs=out_specs, out_shape=out_shape,
        compiler_params=_cparams("arbitrary"), name="norm_proj")(x, g.reshape(1, d), bias, *ws)


BIAS_BLOCK_ROWS = 32


def _band_bias_kernel(rb_ref, off_ref, o_ref, *, n_real):
    n_heads, rows, cols = o_ref.shape[1:]
    off = off_ref[pl.program_id(0)]
    i0 = pl.program_id(1) * rows
    r0 = pl.program_id(2) * cols
    i = lax.broadcasted_iota(jnp.int32, (rows, cols), 0) + i0
    r = lax.broadcasted_iota(jnp.int32, (rows, cols), 1) + r0
    d = jnp.clip(off + i - r, -REL_CLIP, REL_CLIP) + REL_CLIP
    lo = jnp.clip(off + i0 - (r0 + cols - 1), -REL_CLIP, REL_CLIP) + REL_CLIP
    hi = jnp.clip(off + i0 + rows - 1 - r0, -REL_CLIP, REL_CLIP) + REL_CLIP
    first = (i // CHUNK - A_PAST_CHUNKS) * CHUNK + off
    last = jnp.minimum((i // CHUNK + 1) * CHUNK + off, n_real)
    visible = (r >= first) & (r < last)
    first0 = (i0 // CHUNK - A_PAST_CHUNKS) * CHUNK + off
    last0 = jnp.minimum(((i0 + rows - 1) // CHUNK + 1) * CHUNK + off, n_real)
    hi = jnp.where((r0 + cols <= first0) | (r0 >= last0), lo - 1, hi)

    def body(u, tbls):
        hit = d == u
        return tuple(jnp.where(hit, rb_ref[h, u], t) for h, t in enumerate(tbls))

    tbls = lax.fori_loop(lo, hi + 1, body,
                         tuple(jnp.zeros((rows, cols), F32) for _ in range(n_heads)))
    for h in range(n_heads):
        o_ref[0, h] = jnp.where(visible, LOG2E * tbls[h], NEG_INF)


def band_bias(rel_bias, offsets, *, rows, cols, n_real):
    n_rel, n_heads = rel_bias.shape
    blk = min(rows, BIAS_BLOCK_ROWS)
    return pl.pallas_call(
        functools.partial(_band_bias_kernel, n_real=n_real),
        grid=(len(offsets), rows // blk, cols // LANES),
        in_specs=[pl.BlockSpec(memory_space=pltpu.SMEM), pl.BlockSpec(memory_space=pltpu.SMEM)],
        out_specs=pl.BlockSpec((1, n_heads, blk, LANES), lambda o, t, c: (o, 0, t, c)),
        out_shape=jax.ShapeDtypeStruct((len(offsets), n_heads, rows, cols), F32),
        compiler_params=_cparams("arbitrary", "arbitrary", "arbitrary"),
        name="band_bias")(rel_bias.T, jnp.asarray(offsets, jnp.int32))


def _head_masks(shape):
    lane = lax.broadcasted_iota(jnp.int32, shape, len(shape) - 1)
    return [(lane % LANES) // HEAD_DIM == j for j in range(HEADS_PER_VREG)]


def _ones_lanes():
    return [((j + 1) % HEADS_PER_VREG) * HEAD_DIM for j in range(HEADS_PER_VREG)]


def _masked_values(v, j):
    lane = lax.broadcasted_iota(jnp.int32, v.shape, 1) % LANES
    vj = jnp.where(lane // HEAD_DIM == j, v, jnp.zeros((), BF16))
    return jnp.where(lane == _ones_lanes()[j], jnp.ones((), BF16), vj)


def _band_attn_kernel(q_ref, k_ref, v_ref, bias_ref, o_ref, v16, s_scr, p_scr, *, tq, band):
    step = pl.program_id(1)
    ck = KEY_CHUNK

    @pl.when(step == 0)
    def _():
        v = v_ref[0]
        for j in range(HEADS_PER_VREG):
            v16[j] = _masked_values(v, j)

    start = pl.multiple_of(jnp.maximum(step * (tq // CHUNK) - A_PAST_CHUNKS, 0) * CHUNK, CHUNK)
    qmask = _head_masks((tq, LANES))
    ones_lane = _ones_lanes()
    cols = [slice(c * ck, (c + 1) * ck) for c in range(band // ck)]
    heads = [(hp, j) for hp in range(q_ref.shape[2] // LANES) for j in range(HEADS_PER_VREG)]
    lanes = [slice(hp * LANES, (hp + 1) * LANES) for hp, _ in heads]
    qs = [jnp.where(qmask[j], q_ref[0, :, lanes[n]], jnp.zeros((), BF16))
          for n, (_, j) in enumerate(heads)]

    def logits(n, c, mrun):
        k = k_ref[0, pl.ds(pl.multiple_of(start + c * ck, CHUNK), ck), lanes[n]]
        s = _dot_nt(qs[n], k) + bias_ref[0, n, :, cols[c]]
        s_scr[n % 2, :, cols[c]] = s
        for part in range(ck // LANES):
            mrun = jnp.maximum(mrun, s[:, part * LANES:(part + 1) * LANES])
        return mrun

    def probs(n, c, row_max):
        p_scr[n % 2, :, cols[c]] = jnp.exp2(s_scr[n % 2, :, cols[c]] - row_max).astype(BF16)

    def pv(n):
        return _dot(p_scr[n % 2], v16[heads[n][1], pl.ds(start, band), lanes[n]])

    neg = jnp.full((tq, LANES), NEG_INF, F32)
    outs = []
    mrun = functools.reduce(lambda m, c: logits(0, c, m), range(len(cols)), neg)
    for n in range(len(heads)):
        row_max = jnp.max(mrun, axis=-1, keepdims=True)
        mrun = neg
        for c in range(len(cols)):
            if n + 1 < len(heads):
                mrun = logits(n + 1, c, mrun)
            probs(n, c, row_max)
        o = pv(n)
        lane = ones_lane[heads[n][1]]
        outs.append(o / o[:, lane:lane + 1])
    for n in range(0, len(heads), HEADS_PER_VREG):
        o_ref[0, :, lanes[n]] = jnp.where(qmask[0], outs[n], outs[n + 1]).astype(BF16)


def band_attn(q, k, v, bias, *, tq, band):
    b, sq, w = q.shape
    sk = k.shape[1]
    n_off, n_heads = bias.shape[:2]
    assert band % KEY_CHUNK == 0 and tq % CHUNK == 0 or sq == tq
    kern = functools.partial(_band_attn_kernel, tq=tq, band=band)
    return pl.pallas_call(
        kern, grid=(b, sq // tq),
        in_specs=[pl.BlockSpec((1, tq, w), lambda i, c: (i, c, 0)),
                  pl.BlockSpec((1, sk, w), lambda i, c: (i, 0, 0)),
                  pl.BlockSpec((1, sk, w), lambda i, c: (i, 0, 0)),
                  pl.BlockSpec((1, n_heads, tq, band),
                               lambda i, c: (jnp.minimum(c, n_off - 1), 0, 0, 0))],
        out_specs=pl.BlockSpec((1, tq, w), lambda i, c: (i, c, 0)),
        out_shape=jax.ShapeDtypeStruct((b, sq, w), BF16),
        scratch_shapes=[pltpu.VMEM((HEADS_PER_VREG, sk, w), BF16),
                        pltpu.VMEM((2, tq, band), F32),
                        pltpu.VMEM((2, tq, band), BF16)],
        compiler_params=_cparams("parallel", "arbitrary"), name="band_attn")(q, k, v, bias)


STICK_UNDERFLOW_LOG2 = 160.0
STICK_NEAR_CHUNKS = 2


def _stick_kernel(q_ref, k_ref, v_ref, o_ref, v16, zl_scr, hl_scr, w_scr, *,
                  tq, q_offset, variants):
    qi = pl.program_id(2)
    ck = KEY_CHUNK
    heads = [(pp, j) for pp in range(q_ref.shape[2] // LANES) for j in range(HEADS_PER_VREG)]
    n_heads = len(heads)
    lanes = [slice(pp * LANES, (pp + 1) * LANES) for pp, _ in heads]

    @pl.when(qi == 0)
    def _():
        v = v_ref[0]
        vmask = _head_masks(v.shape)
        for j in range(HEADS_PER_VREG):
            v16[j] = jnp.where(vmask[j], v, jnp.zeros((), BF16))

    qmask = _head_masks((tq, LANES))
    qs = [jnp.where(qmask[j], q_ref[0, :, lanes[n]], jnp.zeros((), BF16))
          for n, (_, j) in enumerate(heads)]
    q_pos = q_offset + qi * tq + lax.broadcasted_iota(jnp.int32, (tq, ck), 0)
    k_iota = lax.broadcasted_iota(jnp.int32, (tq, ck), 1)
    rr = lax.broadcasted_iota(jnp.int32, (2 * ck, ck), 0) % ck
    cc = lax.broadcasted_iota(jnp.int32, (2 * ck, ck), 1)
    tri2 = jnp.where(rr > cc, 1.0, 0.0).astype(BF16)
    n_chunks = (q_offset + (qi + 1) * tq + ck - 1) // ck

    def split_cols(slot):
        return slice(2 * slot.start, 2 * slot.start + ck), slice(2 * slot.start + ck, 2 * slot.stop)

    def logits(j, slot, c, masked):
        k0 = pl.multiple_of(c * ck, ck)
        z = _dot_nt(qs[j], k_ref[0, pl.ds(k0, ck), lanes[j]])
        neg_abs = lax.bitcast_convert_type(
            lax.bitcast_convert_type(z, jnp.int32) | jnp.int32(-2 ** 31), F32)
        sp = jnp.maximum(z, 0.0) + jnp.log2(1.0 + jnp.exp2(neg_abs))
        if masked:
            sp = jnp.where(k0 + k_iota < q_pos, sp, 0.0)
        zl_scr[j, :, slot] = z - sp
        hi_cols, lo_cols = split_cols(slot)
        hl_scr[j, :, hi_cols], hl_scr[j, :, lo_cols] = _split2(sp)
        return jnp.sum(sp, axis=-1, keepdims=True)

    def weights(j, slot, c, later, masked):
        sums = _dot(hl_scr[j, :, 2 * slot.start:2 * slot.stop], tri2)
        w = jnp.exp2(zl_scr[j, :, slot] - sums - later)
        if masked:
            w = jnp.where(c * ck + k_iota < q_pos, w, 0.0)
        w_scr[j, :, slot] = w.astype(BF16)

    def more(laters):
        return (jnp.min(functools.reduce(jnp.minimum, laters))
                < STICK_UNDERFLOW_LOG2).astype(jnp.int32)

    def attend(near, n_masked):
        slots = [slice((near - 1 - i) * ck, (near - i) * ck) for i in range(near)]
        chunk = [n_chunks - 1 - i for i in range(near)]
        k0 = pl.multiple_of((n_chunks - near) * ck, ck)

        def pv(n, k0, width):
            return _dot(w_scr[n, :, :width], v16[heads[n][1], pl.ds(k0, width), lanes[n]])

        laters, accs = [], []
        row_sums = [logits(0, slots[i], chunk[i], i < n_masked) for i in range(near)]
        for n in range(n_heads):
            later = jnp.zeros((tq, 1), F32)
            next_sums = []
            for i in range(near):
                if n + 1 < n_heads:
                    next_sums.append(logits(n + 1, slots[i], chunk[i], i < n_masked))
                weights(n, slots[i], chunk[i], later, i < n_masked)
                later = later + row_sums[i]
            laters.append(later)
            row_sums = next_sums
            o = pv(n, k0, near * ck)
            if heads[n][1] == 0:
                accs.append(o)
            else:
                accs[-1] = accs[-1] + o

        def cond(carry):
            return (carry[0] >= 0) & (carry[1] > 0)

        def body(carry):
            c, _, accs, laters = carry
            accs = list(accs)
            k0 = pl.multiple_of(c * ck, ck)
            new = []
            for n in range(n_heads):
                rs = logits(n, slice(0, ck), c, False)
                weights(n, slice(0, ck), c, laters[n], False)
                accs[heads[n][0]] = accs[heads[n][0]] + pv(n, k0, ck)
                new.append(laters[n] + rs)
            return c - 1, more(new), tuple(accs), tuple(new)

        carry = lax.while_loop(cond, body,
                               (n_chunks - 1 - near, more(laters), tuple(accs), tuple(laters)))
        for pp, acc in enumerate(carry[2]):
            o_ref[0, :, pp * LANES:(pp + 1) * LANES] = acc.astype(BF16)

    for n_total, near, n_masked in variants:
        if n_total is None:
            pl.when(n_chunks >= near)(functools.partial(attend, near, n_masked))
        else:
            pl.when(n_chunks == n_total)(functools.partial(attend, near, n_masked))


def stick_attn(q, k, v, *, tq, q_offset):
    b, sq, w = q.shape
    sk = k.shape[1]
    ck = KEY_CHUNK
    assert q_offset % ck == 0 and (tq % ck == 0 or sq == tq <= ck) and sk % ck == 0
    n_masked = -(-tq // ck)
    totals = sorted({-(-(q_offset + (t + 1) * tq) // ck) for t in range(sq // tq)})
    variants = [(n, n, min(n_masked, n)) for n in totals if n < STICK_NEAR_CHUNKS]
    if totals[-1] >= STICK_NEAR_CHUNKS:
        variants.append((None, STICK_NEAR_CHUNKS, n_masked))
    near_cols = STICK_NEAR_CHUNKS * ck
    wg = ATTN_PAIRS_PER_STEP * LANES
    n_heads = ATTN_PAIRS_PER_STEP * HEADS_PER_VREG
    kern = functools.partial(_stick_kernel, tq=tq, q_offset=q_offset, variants=tuple(variants))
    return pl.pallas_call(
        kern, grid=(b, w // wg, sq // tq),
        in_specs=[pl.BlockSpec((1, tq, wg), lambda i, h, t: (i, t, h)),
                  pl.BlockSpec((1, sk, wg), lambda i, h, t: (i, 0, h)),
                  pl.BlockSpec((1, sk, wg), lambda i, h, t: (i, 0, h))],
        out_specs=pl.BlockSpec((1, tq, wg), lambda i, h, t: (i, t, h)),
        out_shape=jax.ShapeDtypeStruct((b, sq, w), BF16),
        scratch_shapes=[pltpu.VMEM((HEADS_PER_VREG, sk, wg), BF16),
                        pltpu.VMEM((n_heads, tq, near_cols), F32),
                        pltpu.VMEM((n_heads, tq, 2 * near_cols), BF16),
                        pltpu.VMEM((n_heads, tq, near_cols), BF16)],
        compiler_params=_cparams("parallel", "parallel", "arbitrary"), name="stick_attn")(q, k, v)


def _cumsum_kernel(lf_ref, col_ref, row_ref, *, blk):
    s_len, n_heads = lf_ref.shape[1], lf_ref.shape[2]
    rr = lax.broadcasted_iota(jnp.int32, (blk, blk), 0)
    cc = lax.broadcasted_iota(jnp.int32, (blk, blk), 1)
    lower = jnp.where(rr >= cc, 1.0, 0.0).astype(BF16)
    upper = jnp.where(rr <= cc, 1.0, 0.0).astype(BF16)
    dot_tn = lambda a, b: lax.dot_general(a, b, (((0,), (0,)), ((), ())),
                                          preferred_element_type=F32)
    blocks = [slice(n * blk, (n + 1) * blk) for n in range(s_len // blk)]
    parts = [_split3(lf_ref[0, rows, :]) for rows in blocks]
    local_col = [sum(_dot(lower, p) for p in ps) for ps in parts]
    local_row = [sum(dot_tn(p, upper) for p in ps) for ps in parts]
    carry_col = jnp.zeros((1, n_heads), F32)
    carry_row = jnp.zeros((n_heads, 1), F32)
    for rows, c, ct in zip(blocks, local_col, local_row):
        col_ref[0, rows, :] = c + carry_col
        row_ref[0, :, rows] = ct + carry_row
        carry_col = carry_col + c[blk - 1:blk, :]
        carry_row = carry_row + ct[:, blk - 1:blk]


def cumsum_logf(lf):
    b, s_len, n_heads = lf.shape
    kern = functools.partial(_cumsum_kernel, blk=LANES)
    return pl.pallas_call(
        kern, grid=(b,),
        in_specs=[pl.BlockSpec((1, s_len, n_heads), lambda i: (i, 0, 0))],
        out_specs=[pl.BlockSpec((1, s_len, n_heads), lambda i: (i, 0, 0)),
                   pl.BlockSpec((1, n_heads, s_len), lambda i: (i, 0, 0))],
        out_shape=[jax.ShapeDtypeStruct((b, s_len, n_heads), F32),
                   jax.ShapeDtypeStruct((b, n_heads, s_len), F32)],
        compiler_params=_cparams("parallel"), name="cumsum_logf")(lf)


FORGET_SKIP_LOG2 = 152.0
FORGET_BOUND_SLACK = 1.02
FORGET_BOUND_MARGIN = 2.0


def _forget_skip_kernel(qmax_ref, kmax_ref, cum_ref, o_ref, *, heads_per_group):
    n_blk, n_heads = kmax_ref.shape[1], kmax_ref.shape[2]
    blk = cum_ref.shape[1] // n_blk
    kmax = kmax_ref[0]
    k_term = kmax + jnp.max(kmax, axis=0, keepdims=True)
    cum_end = cum_ref[0, pl.ds(blk - 1, n_blk, stride=blk), :]
    cum_start = cum_ref[0, pl.ds(0, n_blk, stride=blk), :]
    chunk_id = lax.broadcasted_iota(jnp.int32, (n_blk, 1), 0)
    head_id = lax.broadcasted_iota(jnp.int32, (n_blk, n_heads), 1)
    row = lax.broadcasted_iota(jnp.int32, o_ref.shape[1:], 0)
    col = lax.broadcasted_iota(jnp.int32, o_ref.shape[1:], 1)
    table = jnp.zeros(o_ref.shape[1:], F32)
    for qi in range(n_blk):
        bound = (FORGET_BOUND_SLACK * qmax_ref[0, qi:qi + 1, :] * k_term
                 + LOG2E * (cum_start[qi:qi + 1, :] - cum_end) + FORGET_BOUND_MARGIN)
        for g in range(n_heads // heads_per_group):
            worst = jnp.max(jnp.where(head_id // heads_per_group == g, bound, NEG_INF),
                            axis=1, keepdims=True)
            needed = jnp.logical_not(worst < -FORGET_SKIP_LOG2) | (chunk_id >= qi)
            first = jnp.min(jnp.where(needed, chunk_id, n_blk).astype(F32), axis=0, keepdims=True)
            table = jnp.where((row == qi) & (col == g), first, table)
    o_ref[0] = table.astype(jnp.int32)


def forget_skip_table(qmax, kmax, cum, *, heads_per_group):
    b, n_blk, n_heads = kmax.shape
    s_len = cum.shape[1]
    blk3 = lambda shape: pl.BlockSpec((1,) + shape, lambda i: (i, 0, 0))
    return pl.pallas_call(
        functools.partial(_forget_skip_kernel, heads_per_group=heads_per_group), grid=(b,),
        in_specs=[blk3((n_blk, n_heads)), blk3((n_blk, n_heads)), blk3((s_len, n_heads))],
        out_specs=blk3((n_blk, LANES)),
        out_shape=jax.ShapeDtypeStruct((b, n_blk, LANES), jnp.int32),
        compiler_params=_cparams("parallel"), name="forget_skip_table")(qmax, kmax, cum)


def _forget_kernel(c0_ref, q_ref, k_ref, v_ref, cq_ref, ck_ref, o_ref, v16, s_scr, p_scr, *,
                   tq, q_offset, max_chunks):
    group = pl.program_id(1)
    qi = pl.program_id(2)
    wg = k_ref.shape[2]
    ck_w = KEY_CHUNK
    ones_lane = _ones_lanes()
    heads = [(pp, j) for pp in range(wg // LANES) for j in range(HEADS_PER_VREG)]
    lanes = [slice(pp * LANES, (pp + 1) * LANES) for pp, _ in heads]

    @pl.when(qi == 0)
    def _():
        v = v_ref[0]
        for j in range(HEADS_PER_VREG):
            v16[j] = _masked_values(v, j)

    qmask = _head_masks((tq, LANES))
    q_pos = q_offset + qi * tq + lax.broadcasted_iota(jnp.int32, (tq, ck_w), 0)
    k_iota = lax.broadcasted_iota(jnp.int32, (tq, ck_w), 1)
    cq_all = cq_ref[0]
    head_lane = lax.broadcasted_iota(jnp.int32, cq_all.shape, 1)
    n_c = (q_offset + (qi + 1) * tq + ck_w - 1) // ck_w
    step = (pl.program_id(0) * pl.num_programs(1) + group) * pl.num_programs(2) + qi
    c0 = jnp.minimum(c0_ref[step], n_c - 1)
    n_proc_here = n_c - c0

    def attend(n_proc):
        cols = [slice(c * ck_w, (c + 1) * ck_w) for c in range(n_proc)]
        k0 = [pl.multiple_of((c0 + c) * ck_w, ck_w) for c in range(n_proc)]
        allowed = k0[-1] + k_iota <= q_pos
        qs = [jnp.where(qmask[j], q_ref[0, :, lanes[n]], jnp.zeros((), BF16))
              for n, (_, j) in enumerate(heads)]
        cq = [LOG2E * jnp.sum(jnp.where(head_lane == group * len(heads) + n, cq_all, 0.0),
                              axis=-1, keepdims=True) for n in range(len(heads))]

        def logits(n, c, mrun):
            s = (_dot_nt(qs[n], k_ref[0, pl.ds(k0[c], ck_w), lanes[n]])
                 - LOG2E * ck_ref[0, 0, n:n + 1, pl.ds(k0[c], ck_w)])
            if c == n_proc - 1:
                s = jnp.where(allowed, s, NEG_INF)
            s_scr[n % 2, :, cols[c]] = s
            for part in range(ck_w // LANES):
                mrun = jnp.maximum(mrun, s[:, part * LANES:(part + 1) * LANES])
            return mrun

        def row_term(n, mrun):
            m = jnp.max(mrun, axis=-1, keepdims=True) + cq[n]
            return cq[n] - m

        def probs(n, c, row):
            p_scr[n % 2, :, cols[c]] = jnp.exp2(s_scr[n % 2, :, cols[c]] + row).astype(BF16)

        def pv(n):
            return _dot(p_scr[n % 2, :, :n_proc * ck_w],
                        v16[heads[n][1], pl.ds(k0[0], n_proc * ck_w), lanes[n]])

        neg = jnp.full((tq, LANES), NEG_INF, F32)
        outs = []
        mrun = functools.reduce(lambda m, c: logits(0, c, m), range(n_proc), neg)
        for n in range(len(heads)):
            row = row_term(n, mrun)
            mrun = neg
            for c in range(n_proc):
                if n + 1 < len(heads):
                    mrun = logits(n + 1, c, mrun)
                probs(n, c, row)
            o = pv(n)
            lane = ones_lane[heads[n][1]]
            outs.append(o / o[:, lane:lane + 1])
        for n in range(0, len(heads), HEADS_PER_VREG):
            o_ref[0, :, lanes[n]] = jnp.where(qmask[0], outs[n], outs[n + 1]).astype(BF16)

    for n_proc in range(1, max_chunks + 1):
        pl.when(n_proc_here == n_proc)(functools.partial(attend, n_proc))


def forget_attn(q, k, v, cum_q, cum_k_rows, skip, *, tq, q_offset):
    b, sq, w = q.shape
    sk = k.shape[1]
    n_heads = cum_q.shape[2]
    ck_w = KEY_CHUNK
    diag = [-(-(q_offset + (t + 1) * tq) // ck_w) - 1 for t in range(sq // tq)]
    assert sk % ck_w == 0 and all((q_offset + t * tq) // ck_w == c for t, c in enumerate(diag))
    max_chunks = max(diag) + 1
    wg = ATTN_PAIRS_PER_STEP * LANES
    heads_per_step = ATTN_PAIRS_PER_STEP * HEADS_PER_VREG
    assert skip.shape == (b, w // wg, sq // tq)
    ck = cum_k_rows.reshape(b, n_heads // heads_per_step, heads_per_step, sk)
    kern = functools.partial(_forget_kernel, tq=tq, q_offset=q_offset, max_chunks=max_chunks)
    grid_spec = pltpu.PrefetchScalarGridSpec(
        num_scalar_prefetch=1, grid=(b, w // wg, sq // tq),
        in_specs=[pl.BlockSpec((1, tq, wg), lambda i, h, t, c0: (i, t, h)),
                  pl.BlockSpec((1, sk, wg), lambda i, h, t, c0: (i, 0, h)),
                  pl.BlockSpec((1, sk, wg), lambda i, h, t, c0: (i, 0, h)),
                  pl.BlockSpec((1, tq, n_heads), lambda i, h, t, c0: (i, t, 0)),
                  pl.BlockSpec((1, 1, heads_per_step, sk), lambda i, h, t, c0: (i, h, 0, 0))],
        out_specs=pl.BlockSpec((1, tq, wg), lambda i, h, t, c0: (i, t, h)),
        scratch_shapes=[pltpu.VMEM((HEADS_PER_VREG, sk, wg), BF16),
                        pltpu.VMEM((2, tq, max_chunks * ck_w), F32),
                        pltpu.VMEM((2, tq, max_chunks * ck_w), BF16)])
    return pl.pallas_call(
        kern, grid_spec=grid_spec, out_shape=jax.ShapeDtypeStruct((b, sq, w), BF16),
        compiler_params=_cparams("parallel", "parallel", "arbitrary"),
        name="forget_attn")(skip.reshape(-1), q, k, v, cum_q, ck)


def _layer_tail_kernel(*refs, n_in, n_heads, rows_per_seq, tf, final_norm):
    a_refs, w_refs = refs[:n_in], refs[n_in:2 * n_in]
    (x_ref, gx_ref, wq_ref, wo_ref, mk_ref, mv_ref, gf_ref, wg_ref, wu_ref, wd_ref, gfin_ref,
     o_ref) = refs[2 * n_in:]
    x = x_ref[...]
    for a_ref, w_ref in zip(a_refs, w_refs):
        x = x + _dot(a_ref[...], w_ref[...])

    hd = x.shape[1] // n_heads
    q = (_dot(_rms_bf16(x, gx_ref[...]), wq_ref[...]) * (hd ** -0.5)).astype(BF16)
    per_seq = []
    for b in range(mk_ref.shape[0]):
        rows = slice(b * rows_per_seq, (b + 1) * rows_per_seq)
        outs = []
        for j in range(n_heads):
            cols = slice(j * hd, (j + 1) * hd)
            s = _dot_nt(q[rows, cols], mk_ref[b, :, cols])
            p = jnp.exp(s - jnp.max(s, axis=-1, keepdims=True))
            p = p / jnp.sum(p, axis=-1, keepdims=True)
            outs.append(_dot(p.astype(BF16), mv_ref[b, :, cols]).astype(BF16))
        per_seq.append(jnp.concatenate(outs, axis=-1))
    x = x + _dot(jnp.concatenate(per_seq, axis=0), wo_ref[...])

    h = _rms_bf16(x, gf_ref[...])
    for c in range(wg_ref.shape[1] // tf):
        cols = slice(c * tf, (c + 1) * tf)
        gate = _dot(h, wg_ref[:, cols])
        up = _dot(h, wu_ref[:, cols])
        a = (gate * jax.nn.sigmoid(gate) * up).astype(BF16)
        x = x + _dot(a, wd_ref[cols, :])
    if final_norm:
        x = x * lax.rsqrt(jnp.mean(x * x, axis=-1, keepdims=True) + RMS_EPS) * gfin_ref[...]
    o_ref[...] = x


def layer_tail(a_list, w_list, x, g_x, wq, wo, mk, mv, g_f, wg, wu, wd, g_final, *, seq, tm, tf,
               final_norm):
    m, d = x.shape
    n_mem = mk.shape[1]
    rows_per_seq = min(tm, seq)
    seqs = tm // rows_per_seq
    assert seq % rows_per_seq == 0 and m % tm == 0
    row = lambda i: (i, 0)
    mem = lambda i: (i * tm // (seq * seqs), 0, 0)
    gain = lambda g: g.reshape(1, d)
    kern = functools.partial(_layer_tail_kernel, n_in=len(a_list), n_heads=XA_HEADS,
                             rows_per_seq=rows_per_seq, tf=tf, final_norm=final_norm)
    in_specs = ([pl.BlockSpec((tm, a.shape[1]), row) for a in a_list]
                + [_resident_spec(w.shape) for w in w_list]
                + [pl.BlockSpec((tm, d), row), _resident_spec((1, d)),
                   _resident_spec(wq.shape), _resident_spec(wo.shape),
                   pl.BlockSpec((seqs, n_mem, d), mem), pl.BlockSpec((seqs, n_mem, d), mem),
                   _resident_spec((1, d)), _resident_spec(wg.shape), _resident_spec(wu.shape),
                   _resident_spec(wd.shape), _resident_spec((1, d))])
    return pl.pallas_call(
        kern, grid=(m // tm,), in_specs=in_specs,
        out_specs=pl.BlockSpec((tm, d), row), out_shape=jax.ShapeDtypeStruct((m, d), F32),
        compiler_params=_cparams("parallel"), name="layer_tail")(
            *a_list, *w_list, x, gain(g_x), wq, wo, mk, mv, gain(g_f), wg, wu, wd, gain(g_final))


def _pad_rows(a, rows):
    return jnp.pad(a, ((0, 0), (0, rows - a.shape[1]), (0, 0)))


def _row_tile(m, seq, cap):
    tm = min(cap, seq)
    assert seq % tm == 0 and m % tm == 0
    return tm


def _layer_ab(x, seq, g, w_in, w_out, rel_bias, cache, *, tm):
    m, d = x.shape
    b = m // seq
    wa = w_in.shape[1] // 6
    ws = [w_in[:, n * wa:(n + 1) * wa].astype(BF16) for n in range(6)]
    scale = HEAD_DIM ** -0.5 * LOG2E
    kept = 'tail' if min(A_PAST, seq) < seq else 'heads'
    outs = [(4, 'heads', HEAD_DIM), (4, 'bf16', 1.0), (5, 'heads', HEAD_DIM), (5, 'bf16', 1.0),
            (0, 'bf16', scale), (3, 'bf16', scale), (1, 'bf16', 1.0), (2, 'bf16', 1.0),
            (1, kept, HEAD_DIM), (2, kept, HEAD_DIM)]
    kb, kb16, vb, vb16, qa, qb, ka16, va16, ka_keep, va_keep = norm_proj(x, g, ws, outs, seq=seq,
                                                                         tm=tm)
    shp = lambda a: a.reshape(b, -1, a.shape[-1])
    if cache is None:
        tq = BAND_ROWS
        band = -(-(A_PAST + tq) // KEY_CHUNK) * KEY_CHUNK
        offsets = sorted({t * tq - max(t * (tq // CHUNK) - A_PAST_CHUNKS, 0) * CHUNK
                          for t in range(seq // tq)})
        bias = band_bias(rel_bias, offsets, rows=tq, cols=band, n_real=band)
        oa = band_attn(shp(qa), shp(ka16), shp(va16), bias, tq=tq, band=band)
        ob = stick_attn(shp(qb), shp(kb16), shp(vb16), tq=KEY_CHUNK, q_offset=0)
    else:
        ca_k, ca_v, cb_k, cb_v = cache
        n_past = ca_k.shape[1]
        n_keys = n_past + seq
        band = -(-n_keys // KEY_CHUNK) * KEY_CHUNK
        flat = lambda a: a.reshape(a.shape[0], a.shape[1], -1)
        k_all = _pad_rows(jnp.concatenate([flat(ca_k).astype(BF16), shp(ka16)], axis=1), band)
        v_all = _pad_rows(jnp.concatenate([flat(ca_v).astype(BF16), shp(va16)], axis=1), band)
        bias = band_bias(rel_bias, [n_past], rows=seq, cols=band, n_real=n_keys)
        oa = band_attn(shp(qa), k_all, v_all, bias, tq=seq, band=band)
        n_pastb = cb_k.shape[1]
        sk = -(-(n_pastb + seq) // KEY_CHUNK) * KEY_CHUNK
        kb_all = _pad_rows(jnp.concatenate([flat(cb_k).astype(BF16), shp(kb16)], axis=1), sk)
        vb_all = _pad_rows(jnp.concatenate([flat(cb_v).astype(BF16), shp(vb16)], axis=1), sk)
        ob = stick_attn(shp(qb), kb_all, vb_all, tq=seq, q_offset=n_pastb)
    w_out = w_out.astype(BF16)
    wo_a, wo_b = w_out[:oa.shape[-1]], w_out[oa.shape[-1]:]
    return ([oa.reshape(m, -1), ob.reshape(m, -1)], [wo_a, wo_b]), (ka_keep, va_keep, kb, vb)


def _layer_c(x, seq, g, w_in, b_f, w_out, cache, *, tm):
    m, d = x.shape
    b = m // seq
    n_heads = b_f.shape[0]
    wq, wk, wv, wf = (w_in[:, :d], w_in[:, d:2 * d], w_in[:, 2 * d:3 * d], w_in[:, 3 * d:])
    ws = [w.astype(BF16) for w in (wq, wk, wv, wf)]
    scale = HEAD_DIM ** -0.5 * LOG2E
    outs = [(1, 'heads', HEAD_DIM), (1, 'bf16', 1.0), (2, 'heads', HEAD_DIM), (2, 'bf16', 1.0),
            (0, 'bf16', scale), (3, 'logf', None)]
    if cache is None:
        outs += [(1, 'rowmax', (HEAD_DIM, 1.0)), (0, 'rowmax', (HEAD_DIM, scale))]
    k, k16, v, v16, q, lf, *norms = norm_proj(x, g, ws, outs, seq=seq, tm=tm,
                                              bias=b_f.reshape(1, n_heads))
    norms = norms[::-1]
    shp = lambda a: a.reshape(b, -1, a.shape[-1])
    heads_per_group = ATTN_PAIRS_PER_STEP * HEADS_PER_VREG
    n_groups = n_heads // heads_per_group
    if cache is None:
        cum_col, cum_row = cumsum_logf(shp(lf))
        tq = KEY_CHUNK
        qmax, kmax = (a.reshape(b, seq // KEY_CHUNK, n_heads) for a in norms)
        skip = forget_skip_table(qmax, kmax, cum_col, heads_per_group=heads_per_group)
        skip = jnp.swapaxes(skip[:, :, :n_groups], 1, 2)
        o = forget_attn(shp(q), shp(k16), shp(v16), cum_col, cum_row, skip, tq=tq, q_offset=0)
    else:
        c_k, c_v, c_lf = cache
        n_past = c_k.shape[1]
        sk = -(-(n_past + seq) // KEY_CHUNK) * KEY_CHUNK
        flat = lambda a: a.reshape(a.shape[0], a.shape[1], -1)
        k_all = _pad_rows(jnp.concatenate([flat(c_k).astype(BF16), shp(k16)], axis=1), sk)
        v_all = _pad_rows(jnp.concatenate([flat(c_v).astype(BF16), shp(v16)], axis=1), sk)
        lf_all = _pad_rows(jnp.concatenate([c_lf, shp(lf)], axis=1), sk)
        cum_col, cum_row = cumsum_logf(lf_all)
        o = forget_attn(shp(q), k_all, v_all, cum_col[:, n_past:n_past + seq], cum_row,
                        jnp.zeros((b, n_groups, 1), jnp.int32), tq=seq, q_offset=n_past)
    return ([o.reshape(m, d)], [w_out.astype(BF16)]), (k, v, lf)


def kernel(x_prompt, x_sample, cache_a_k, cache_a_v, cache_b_k, cache_b_v, cache_c_k, cache_c_v, cache_c_logf, cache_mem_k, cache_mem_v, mem_prompt, w_in_ab, w_out_ab, rel_bias_a, w_in_c, b_f_c, w_out_c, g_mix, g_xattn, g_mem, w_xq, w_xk, w_xv, w_xo, g_ffn, w_gate, w_up, w_down, g_final):
    bp, sp, d = x_prompt.shape
    bs, ss, _ = x_sample.shape
    depth = g_mix.shape[0]
    n_mem = mem_prompt.shape[1]
    xp = x_prompt.reshape(bp * sp, d)
    xs = x_sample.reshape(bs * ss, d)
    tmp = _row_tile(bp * sp, sp, ROW_TILE)
    tms = bs * ss
    assert tms <= ROW_TILE
    mem = mem_prompt.reshape(bp * n_mem, d)
    tmm = min(ROW_TILE, bp * n_mem)
    assert (bp * n_mem) % tmm == 0
    dff = w_gate.shape[2]
    tf = 256 if dff % 256 == 0 else dff

    a_kp, a_vp, b_kp, b_vp, a_ks, a_vs, b_ks, b_vs = [], [], [], [], [], [], [], []
    c_kp, c_vp, c_lfp, c_ks, c_vs, c_lfs = [], [], [], [], [], []
    mem_kp, mem_vp = [], []
    for layer in range(depth):
        if layer % 2 == 0:
            e = layer // 2
            mix_p, (ka, va, kb, vb) = _layer_ab(xp, sp, g_mix[layer], w_in_ab[e], w_out_ab[e],
                                                rel_bias_a[e], None, tm=tmp)
            a_kp.append(ka); a_vp.append(va); b_kp.append(kb); b_vp.append(vb)
            mix_s, (ka, va, kb, vb) = _layer_ab(
                xs, ss, g_mix[layer], w_in_ab[e], w_out_ab[e], rel_bias_a[e],
                (cache_a_k[e], cache_a_v[e], cache_b_k[e], cache_b_v[e]), tm=tms)
            a_ks.append(ka); a_vs.append(va); b_ks.append(kb); b_vs.append(vb)
        else:
            c = layer // 2
            mix_p, (k, v, lf) = _layer_c(xp, sp, g_mix[layer], w_in_c[c], b_f_c[c], w_out_c[c],
                                         None, tm=tmp)
            c_kp.append(k); c_vp.append(v); c_lfp.append(lf)
            mix_s, (k, v, lf) = _layer_c(xs, ss, g_mix[layer], w_in_c[c], b_f_c[c], w_out_c[c],
                                         (cache_c_k[c], cache_c_v[c], cache_c_logf[c]), tm=tms)
            c_ks.append(k); c_vs.append(v); c_lfs.append(lf)
        mk, mk16, mv, mv16 = norm_proj(
            mem, g_mem[layer], [w_xk[layer].astype(BF16), w_xv[layer].astype(BF16)],
            [(0, 'heads', d // XA_HEADS), (0, 'bf16', 1.0), (1, 'heads', d // XA_HEADS),
             (1, 'bf16', 1.0)],
            seq=n_mem, tm=tmm)
        mem_kp.append(mk); mem_vp.append(mv)
        tail = functools.partial(
            layer_tail, g_x=g_xattn[layer], wq=w_xq[layer].astype(BF16),
            wo=w_xo[layer].astype(BF16), g_f=g_ffn[layer], wg=w_gate[layer].astype(BF16),
            wu=w_up[layer].astype(BF16), wd=w_down[layer].astype(BF16), g_final=g_final, tf=tf,
            final_norm=layer == depth - 1)
        xp = tail(*mix_p, xp, mk=mk16.reshape(bp, n_mem, d), mv=mv16.reshape(bp, n_mem, d),
                  seq=sp, tm=tmp)
        xs = tail(*mix_s, xs, mk=cache_mem_k[layer].reshape(bs, n_mem, d).astype(BF16),
                  mv=cache_mem_v[layer].reshape(bs, n_mem, d).astype(BF16), seq=ss, tm=tms)

    hd = HEAD_DIM
    xa_hd = d // XA_HEADS
    r5 = lambda lst, b, s, dd: jnp.stack([a.reshape(b, s, -1, dd) for a in lst])
    r4 = lambda lst, b, s: jnp.stack([a.reshape(b, s, -1) for a in lst])
    keep = min(A_PAST, sp)
    return (xp.reshape(bp, sp, d), xs.reshape(bs, ss, d),
            r5(a_kp, bp, keep, hd), r5(a_vp, bp, keep, hd), r5(b_kp, bp, sp, hd), r5(b_vp, bp, sp, hd),
            r5(c_kp, bp, sp, hd), r5(c_vp, bp, sp, hd), r4(c_lfp, bp, sp),
            r5(mem_kp, bp, n_mem, xa_hd), r5(mem_vp, bp, n_mem, xa_hd),
            r5(a_ks, bs, ss, hd), r5(a_vs, bs, ss, hd), r5(b_ks, bs, ss, hd), r5(b_vs, bs, ss, hd),
            r5(c_ks, bs, ss, hd), r5(c_vs, bs, ss, hd), r4(c_lfs, bs, ss))
```

```python
import functools

import jax
import jax.numpy as jnp
from jax import lax
from jax.experimental import pallas as pl
from jax.experimental.pallas import tpu as pltpu

F32 = jnp.float32
BF16 = jnp.bfloat16

RMS_EPS = 1e-6
NEG_INF = -1e30
LOG2E = 1.4426950408889634
HEAD_DIM = 64
CHUNK = 64
A_PAST_CHUNKS = 8
A_PAST = A_PAST_CHUNKS * CHUNK
REL_CLIP = 128
XA_HEADS = 4

LANES = 128
HEADS_PER_VREG = LANES // HEAD_DIM
KEY_CHUNK = 256
ATTN_PAIRS_PER_STEP = 4
BAND_ROWS = 4 * CHUNK
ROW_TILE = 512
VMEM_LIMIT = 56 * 1024 * 1024


def _cparams(*sem):
    return pltpu.CompilerParams(dimension_semantics=sem, vmem_limit_bytes=VMEM_LIMIT)


def _resident_spec(shape):
    return pl.BlockSpec(shape, lambda *_: (0,) * len(shape), pipeline_mode=pl.Buffered(1))


def _rms_bf16(x, g):
    y = x * lax.rsqrt(jnp.mean(x * x, axis=-1, keepdims=True) + RMS_EPS)
    return (y * g).astype(BF16)


def _log_sigmoid(z):
    return jnp.minimum(z, 0.0) - jnp.log1p(jnp.exp(-jnp.abs(z)))


def _split3(x):
    hi = x.astype(BF16)
    r = x - hi.astype(F32)
    mid = r.astype(BF16)
    lo = (r - mid.astype(F32)).astype(BF16)
    return hi, mid, lo


def _split2(x):
    hi = x.astype(BF16)
    lo = (x - hi.astype(F32)).astype(BF16)
    return hi, lo


def _dot(a, b):
    return jnp.dot(a, b, preferred_element_type=F32)


def _dot_nt(a, b):
    return lax.dot_general(a, b, (((1,), (1,)), ((), ())), preferred_element_type=F32)


def _norm_proj_kernel(x_ref, g_ref, b_ref, *refs, n_w, outs, tiles_per_seq):
    w_refs, o_refs = refs[:n_w], refs[n_w:]
    h = _rms_bf16(x_ref[...], g_ref[...])
    ys = {}
    for o_ref, (grp, kind, arg) in zip(o_refs, outs):
        if grp not in ys:
            ys[grp] = _dot(h, w_refs[grp][...])
        y = ys[grp]
        if kind == 'f32':
            o_ref[...] = y
        elif kind == 'bf16':
            o_ref[...] = (y * arg).astype(BF16)
        elif kind == 'logf':
            o_ref[...] = _log_sigmoid(y + b_ref[...])
        elif kind == 'heads':
            o_ref[...] = y.reshape(o_ref.shape)
        elif kind == 'rowmax':
            hd, scale = arg
            n_cols, n_heads = y.shape[1], y.shape[1] // hd
            sel_l = lax.broadcasted_iota(jnp.int32, (n_cols, n_heads), 0) // hd
            sel_h = lax.broadcasted_iota(jnp.int32, (n_cols, n_heads), 1)
            scaled = y * scale
            norms = jnp.sqrt(_dot((scaled * scaled).astype(BF16),
                                  jnp.where(sel_l == sel_h, 1.0, 0.0).astype(BF16)))
            for r in range(o_ref.shape[1]):
                o_ref[0, r:r + 1, :] = jnp.max(norms[r * KEY_CHUNK:(r + 1) * KEY_CHUNK],
                                               axis=0, keepdims=True)
        else:
            @pl.when(pl.program_id(0) % tiles_per_seq == tiles_per_seq - 1)
            def _(o_ref=o_ref, y=y):
                o_ref[...] = y.reshape(o_ref.shape)


def norm_proj(x, g, ws, outs, *, seq, tm, bias=None):
    m, d = x.shape
    tiles_per_seq = max(seq // tm, 1)
    if bias is None:
        bias = jnp.zeros((1, 16), F32)
    in_specs = [pl.BlockSpec((tm, d), lambda i: (i, 0)),
                pl.BlockSpec((1, d), lambda i: (0, 0)),
                pl.BlockSpec(bias.shape, lambda i: (0, 0))]
    in_specs += [_resident_spec(w.shape) for w in ws]
    out_shape, out_specs = [], []
    for grp, kind, arg in outs:
        n = ws[grp].shape[1]
        if kind == 'tail':
            assert tm == min(A_PAST, seq)
            out_shape.append(jax.ShapeDtypeStruct((m // tiles_per_seq, n // arg, arg), F32))
            out_specs.append(pl.BlockSpec((tm, n // arg, arg), lambda i: (i // tiles_per_seq, 0, 0)))
        elif kind == 'heads':
            out_shape.append(jax.ShapeDtypeStruct((m, n // arg, arg), F32))
            out_specs.append(pl.BlockSpec((tm, n // arg, arg), lambda i: (i, 0, 0)))
        elif kind == 'rowmax':
            assert tm % KEY_CHUNK == 0
            blocks = (tm // KEY_CHUNK, n // arg[0])
            out_shape.append(jax.ShapeDtypeStruct((m // tm,) + blocks, F32))
            out_specs.append(pl.BlockSpec((1,) + blocks, lambda i: (i, 0, 0)))
        else:
            out_shape.append(jax.ShapeDtypeStruct((m, n), BF16 if kind == 'bf16' else F32))
            out_specs.append(pl.BlockSpec((tm, n), lambda i: (i, 0)))
    kern = functools.partial(_norm_proj_kernel, n_w=len(ws), outs=tuple(outs),
                             tiles_per_seq=tiles_per_seq)
    return pl.pallas_call(
        kern, grid=(m // tm,), in_specs=in_specs, out_specs=out_specs, out_shape=out_shape,
        compiler_params=_cparams("arbitrary"), name="norm_proj")(x, g.reshape(1, d), bias, *ws)


BIAS_BLOCK_ROWS = 32


def _band_bias_kernel(rb_ref, off_ref, o_ref, *, n_real):
    n_heads, rows, cols = o_ref.shape[1:]
    off = off_ref[pl.program_id(0)]
    i0 = pl.program_id(1) * rows
    r0 = pl.program_id(2) * cols
    i = lax.broadcasted_iota(jnp.int32, (rows, cols), 0) + i0
    r = lax.broadcasted_iota(jnp.int32, (rows, cols), 1) + r0
    d = jnp.clip(off + i - r, -REL_CLIP, REL_CLIP) + REL_CLIP
    lo = jnp.clip(off + i0 - (r0 + cols - 1), -REL_CLIP, REL_CLIP) + REL_CLIP
    hi = jnp.clip(off + i0 + rows - 1 - r0, -REL_CLIP, REL_CLIP) + REL_CLIP
    first = (i // CHUNK - A_PAST_CHUNKS) * CHUNK + off
    last = jnp.minimum((i // CHUNK + 1) * CHUNK + off, n_real)
    visible = (r >= first) & (r < last)
    first0 = (i0 // CHUNK - A_PAST_CHUNKS) * CHUNK + off
    last0 = jnp.minimum(((i0 + rows - 1) // CHUNK + 1) * CHUNK + off, n_real)
    hi = jnp.where((r0 + cols <= first0) | (r0 >= last0), lo - 1, hi)

    def body(u, tbls):
        hit = d == u
        return tuple(jnp.where(hit, rb_ref[h, u], t) for h, t in enumerate(tbls))

    tbls = lax.fori_loop(lo, hi + 1, body,
                         tuple(jnp.zeros((rows, cols), F32) for _ in range(n_heads)))
    for h in range(n_heads):
        o_ref[0, h] = jnp.where(visible, LOG2E * tbls[h], NEG_INF)


def band_bias(rel_bias, offsets, *, rows, cols, n_real):
    n_rel, n_heads = rel_bias.shape
    blk = min(rows, BIAS_BLOCK_ROWS)
    return pl.pallas_call(
        functools.partial(_band_bias_kernel, n_real=n_real),
        grid=(len(offsets), rows // blk, cols // LANES),
        in_specs=[pl.BlockSpec(memory_space=pltpu.SMEM), pl.BlockSpec(memory_space=pltpu.SMEM)],
        out_specs=pl.BlockSpec((1, n_heads, blk, LANES), lambda o, t, c: (o, 0, t, c)),
        out_shape=jax.ShapeDtypeStruct((len(offsets), n_heads, rows, cols), F32),
        compiler_params=_cparams("arbitrary", "arbitrary", "arbitrary"),
        name="band_bias")(rel_bias.T, jnp.asarray(offsets, jnp.int32))


def _head_masks(shape):
    lane = lax.broadcasted_iota(jnp.int32, shape, len(shape) - 1)
    return [(lane % LANES) // HEAD_DIM == j for j in range(HEADS_PER_VREG)]


def _ones_lanes():
    return [((j + 1) % HEADS_PER_VREG) * HEAD_DIM for j in range(HEADS_PER_VREG)]


def _masked_values(v, j):
    lane = lax.broadcasted_iota(jnp.int32, v.shape, 1) % LANES
    vj = jnp.where(lane // HEAD_DIM == j, v, jnp.zeros((), BF16))
    return jnp.where(lane == _ones_lanes()[j], jnp.ones((), BF16), vj)


def _band_attn_kernel(q_ref, k_ref, v_ref, bias_ref, o_ref, v16, s_scr, p_scr, *, tq, band):
    step = pl.program_id(1)
    ck = KEY_CHUNK

    @pl.when(step == 0)
    def _():
        v = v_ref[0]
        for j in range(HEADS_PER_VREG):
            v16[j] = _masked_values(v, j)

    start = pl.multiple_of(jnp.maximum(step * (tq // CHUNK) - A_PAST_CHUNKS, 0) * CHUNK, CHUNK)
    qmask = _head_masks((tq, LANES))
    ones_lane = _ones_lanes()
    cols = [slice(c * ck, (c + 1) * ck) for c in range(band // ck)]
    heads = [(hp, j) for hp in range(q_ref.shape[2] // LANES) for j in range(HEADS_PER_VREG)]
    lanes = [slice(hp * LANES, (hp + 1) * LANES) for hp, _ in heads]
    qs = [jnp.where(qmask[j], q_ref[0, :, lanes[n]], jnp.zeros((), BF16))
          for n, (_, j) in enumerate(heads)]

    def logits(n, c, mrun):
        k = k_ref[0, pl.ds(pl.multiple_of(start + c * ck, CHUNK), ck), lanes[n]]
        s = _dot_nt(qs[n], k) + bias_ref[0, n, :, cols[c]]
        s_scr[n % 2, :, cols[c]] = s
        for part in range(ck // LANES):
            mrun = jnp.maximum(mrun, s[:, part * LANES:(part + 1) * LANES])
        return mrun

    def probs(n, c, row_max):
        p_scr[n % 2, :, cols[c]] = jnp.exp2(s_scr[n % 2, :, cols[c]] - row_max).astype(BF16)

    def pv(n):
        return _dot(p_scr[n % 2], v16[heads[n][1], pl.ds(start, band), lanes[n]])

    neg = jnp.full((tq, LANES), NEG_INF, F32)
    outs = []
    mrun = functools.reduce(lambda m, c: logits(0, c, m), range(len(cols)), neg)
    for n in range(len(heads)):
        row_max = jnp.max(mrun, axis=-1, keepdims=True)
        mrun = neg
        for c in range(len(cols)):
            if n + 1 < len(heads):
                mrun = logits(n + 1, c, mrun)
            probs(n, c, row_max)
        o = pv(n)
        lane = ones_lane[heads[n][1]]
        outs.append(o / o[:, lane:lane + 1])
    for n in range(0, len(heads), HEADS_PER_VREG):
        o_ref[0, :, lanes[n]] = jnp.where(qmask[0], outs[n], outs[n + 1]).astype(BF16)


def band_attn(q, k, v, bias, *, tq, band):
    b, sq, w = q.shape
    sk = k.shape[1]
    n_off, n_heads = bias.shape[:2]
    assert band % KEY_CHUNK == 0 and tq % CHUNK == 0 or sq == tq
    kern = functools.partial(_band_attn_kernel, tq=tq, band=band)
    return pl.pallas_call(
        kern, grid=(b, sq // tq),
        in_specs=[pl.BlockSpec((1, tq, w), lambda i, c: (i, c, 0)),
                  pl.BlockSpec((1, sk, w), lambda i, c: (i, 0, 0)),
                  pl.BlockSpec((1, sk, w), lambda i, c: (i, 0, 0)),
                  pl.BlockSpec((1, n_heads, tq, band),
                               lambda i, c: (jnp.minimum(c, n_off - 1), 0, 0, 0))],
        out_specs=pl.BlockSpec((1, tq, w), lambda i, c: (i, c, 0)),
        out_shape=jax.ShapeDtypeStruct((b, sq, w), BF16),
        scratch_shapes=[pltpu.VMEM((HEADS_PER_VREG, sk, w), BF16),
                        pltpu.VMEM((2, tq, band), F32),
                        pltpu.VMEM((2, tq, band), BF16)],
        compiler_params=_cparams("parallel", "arbitrary"), name="band_attn")(q, k, v, bias)


STICK_UNDERFLOW_LOG2 = 160.0
STICK_NEAR_CHUNKS = 2


def _stick_kernel(q_ref, k_ref, v_ref, o_ref, v16, zl_scr, hl_scr, w_scr, *,
                  tq, q_offset, variants):
    qi = pl.program_id(2)
    ck = KEY_CHUNK
    heads = [(pp, j) for pp in range(q_ref.shape[2] // LANES) for j in range(HEADS_PER_VREG)]
    n_heads = len(heads)
    lanes = [slice(pp * LANES, (pp + 1) * LANES) for pp, _ in heads]

    @pl.when(qi == 0)
    def _():
        v = v_ref[0]
        vmask = _head_masks(v.shape)
        for j in range(HEADS_PER_VREG):
            v16[j] = jnp.where(vmask[j], v, jnp.zeros((), BF16))

    qmask = _head_masks((tq, LANES))
    qs = [jnp.where(qmask[j], q_ref[0, :, lanes[n]], jnp.zeros((), BF16))
          for n, (_, j) in enumerate(heads)]
    q_pos = q_offset + qi * tq + lax.broadcasted_iota(jnp.int32, (tq, ck), 0)
    k_iota = lax.broadcasted_iota(jnp.int32, (tq, ck), 1)
    rr = lax.broadcasted_iota(jnp.int32, (2 * ck, ck), 0) % ck
    cc = lax.broadcasted_iota(jnp.int32, (2 * ck, ck), 1)
    tri2 = jnp.where(rr > cc, 1.0, 0.0).astype(BF16)
    n_chunks = (q_offset + (qi + 1) * tq + ck - 1) // ck

    def split_cols(slot):
        return slice(2 * slot.start, 2 * slot.start + ck), slice(2 * slot.start + ck, 2 * slot.stop)

    def logits(j, slot, c, masked):
        k0 = pl.multiple_of(c * ck, ck)
        z = _dot_nt(qs[j], k_ref[0, pl.ds(k0, ck), lanes[j]])
        neg_abs = lax.bitcast_convert_type(
            lax.bitcast_convert_type(z, jnp.int32) | jnp.int32(-2 ** 31), F32)
        sp = jnp.maximum(z, 0.0) + jnp.log2(1.0 + jnp.exp2(neg_abs))
        if masked:
            sp = jnp.where(k0 + k_iota < q_pos, sp, 0.0)
        zl_scr[j, :, slot] = z - sp
        hi_cols, lo_cols = split_cols(slot)
        hl_scr[j, :, hi_cols], hl_scr[j, :, lo_cols] = _split2(sp)
        return jnp.sum(sp, axis=-1, keepdims=True)

    def weights(j, slot, c, later, masked):
        sums = _dot(hl_scr[j, :, 2 * slot.start:2 * slot.stop], tri2)
        w = jnp.exp2(zl_scr[j, :, slot] - sums - later)
        if masked:
            w = jnp.where(c * ck + k_iota < q_pos, w, 0.0)
        w_scr[j, :, slot] = w.astype(BF16)

    def more(laters):
        return (jnp.min(functools.reduce(jnp.minimum, laters))
                < STICK_UNDERFLOW_LOG2).astype(jnp.int32)

    def attend(near, n_masked):
        slots = [slice((near - 1 - i) * ck, (near - i) * ck) for i in range(near)]
        chunk = [n_chunks - 1 - i for i in range(near)]
        k0 = pl.multiple_of((n_chunks - near) * ck, ck)

        def pv(n, k0, width):
            return _dot(w_scr[n, :, :width], v16[heads[n][1], pl.ds(k0, width), lanes[n]])

        laters, accs = [], []
        row_sums = [logits(0, slots[i], chunk[i], i < n_masked) for i in range(near)]
        for n in range(n_heads):
            later = jnp.zeros((tq, 1), F32)
            next_sums = []
            for i in range(near):
                if n + 1 < n_heads:
                    next_sums.append(logits(n + 1, slots[i], chunk[i], i < n_masked))
                weights(n, slots[i], chunk[i], later, i < n_masked)
                later = later + row_sums[i]
            laters.append(later)
            row_sums = next_sums
            o = pv(n, k0, near * ck)
            if heads[n][1] == 0:
                accs.append(o)
            else:
                accs[-1] = accs[-1] + o

        def cond(carry):
            return (carry[0] >= 0) & (carry[1] > 0)

        def body(carry):
            c, _, accs, laters = carry
            accs = list(accs)
            k0 = pl.multiple_of(c * ck, ck)
            new = []
            for n in range(n_heads):
                rs = logits(n, slice(0, ck), c, False)
                weights(n, slice(0, ck), c, laters[n], False)
                accs[heads[n][0]] = accs[heads[n][0]] + pv(n, k0, ck)
                new.append(laters[n] + rs)
            return c - 1, more(new), tuple(accs), tuple(new)

        carry = lax.while_loop(cond, body,
                               (n_chunks - 1 - near, more(laters), tuple(accs), tuple(laters)))
        for pp, acc in enumerate(carry[2]):
            o_ref[0, :, pp * LANES:(pp + 1) * LANES] = acc.astype(BF16)

    for n_total, near, n_masked in variants:
        if n_total is None:
            pl.when(n_chunks >= near)(functools.partial(attend, near, n_masked))
        else:
            pl.when(n_chunks == n_total)(functools.partial(attend, near, n_masked))


def stick_attn(q, k, v, *, tq, q_offset):
    b, sq, w = q.shape
    sk = k.shape[1]
    ck = KEY_CHUNK
    assert q_offset % ck == 0 and (tq % ck == 0 or sq == tq <= ck) and sk % ck == 0
    n_masked = -(-tq // ck)
    totals = sorted({-(-(q_offset + (t + 1) * tq) // ck) for t in range(sq // tq)})
    variants = [(n, n, min(n_masked, n)) for n in totals if n < STICK_NEAR_CHUNKS]
    if totals[-1] >= STICK_NEAR_CHUNKS:
        variants.append((None, STICK_NEAR_CHUNKS, n_masked))
    near_cols = STICK_NEAR_CHUNKS * ck
    wg = ATTN_PAIRS_PER_STEP * LANES
    n_heads = ATTN_PAIRS_PER_STEP * HEADS_PER_VREG
    kern = functools.partial(_stick_kernel, tq=tq, q_offset=q_offset, variants=tuple(variants))
    return pl.pallas_call(
        kern, grid=(b, w // wg, sq // tq),
        in_specs=[pl.BlockSpec((1, tq, wg), lambda i, h, t: (i, t, h)),
                  pl.BlockSpec((1, sk, wg), lambda i, h, t: (i, 0, h)),
                  pl.BlockSpec((1, sk, wg), lambda i, h, t: (i, 0, h))],
        out_specs=pl.BlockSpec((1, tq, wg), lambda i, h, t: (i, t, h)),
        out_shape=jax.ShapeDtypeStruct((b, sq, w), BF16),
        scratch_shapes=[pltpu.VMEM((HEADS_PER_VREG, sk, wg), BF16),
                        pltpu.VMEM((n_heads, tq, near_cols), F32),
                        pltpu.VMEM((n_heads, tq, 2 * near_cols), BF16),
                        pltpu.VMEM((n_heads, tq, near_cols), BF16)],
        compiler_params=_cparams("parallel", "parallel", "arbitrary"), name="stick_attn")(q, k, v)


def _cumsum_kernel(lf_ref, col_ref, row_ref, *, blk):
    s_len, n_heads = lf_ref.shape[1], lf_ref.shape[2]
    rr = lax.broadcasted_iota(jnp.int32, (blk, blk), 0)
    cc = lax.broadcasted_iota(jnp.int32, (blk, blk), 1)
    lower = jnp.where(rr >= cc, 1.0, 0.0).astype(BF16)
    upper = jnp.where(rr <= cc, 1.0, 0.0).astype(BF16)
    dot_tn = lambda a, b: lax.dot_general(a, b, (((0,), (0,)), ((), ())),
                                          preferred_element_type=F32)
    blocks = [slice(n * blk, (n + 1) * blk) for n in range(s_len // blk)]
    parts = [_split3(lf_ref[0, rows, :]) for rows in blocks]
    local_col = [sum(_dot(lower, p) for p in ps) for ps in parts]
    local_row = [sum(dot_tn(p, upper) for p in ps) for ps in parts]
    carry_col = jnp.zeros((1, n_heads), F32)
    carry_row = jnp.zeros((n_heads, 1), F32)
    for rows, c, ct in zip(blocks, local_col, local_row):
        col_ref[0, rows, :] = c + carry_col
        row_ref[0, :, rows] = ct + carry_row
        carry_col = carry_col + c[blk - 1:blk, :]
        carry_row = carry_row + ct[:, blk - 1:blk]


def cumsum_logf(lf):
    b, s_len, n_heads = lf.shape
    kern = functools.partial(_cumsum_kernel, blk=LANES)
    return pl.pallas_call(
        kern, grid=(b,),
        in_specs=[pl.BlockSpec((1, s_len, n_heads), lambda i: (i, 0, 0))],
        out_specs=[pl.BlockSpec((1, s_len, n_heads), lambda i: (i, 0, 0)),
                   pl.BlockSpec((1, n_heads, s_len), lambda i: (i, 0, 0))],
        out_shape=[jax.ShapeDtypeStruct((b, s_len, n_heads), F32),
                   jax.ShapeDtypeStruct((b, n_heads, s_len), F32)],
        compiler_params=_cparams("parallel"), name="cumsum_logf")(lf)


FORGET_SKIP_LOG2 = 152.0
FORGET_BOUND_SLACK = 1.02
FORGET_BOUND_MARGIN = 2.0


def _forget_skip_kernel(qmax_ref, kmax_ref, cum_ref, o_ref, *, heads_per_group):
    n_blk, n_heads = kmax_ref.shape[1], kmax_ref.shape[2]
    blk = cum_ref.shape[1] // n_blk
    kmax = kmax_ref[0]
    k_term = kmax + jnp.max(kmax, axis=0, keepdims=True)
    cum_end = cum_ref[0, pl.ds(blk - 1, n_blk, stride=blk), :]
    cum_start = cum_ref[0, pl.ds(0, n_blk, stride=blk), :]
    chunk_id = lax.broadcasted_iota(jnp.int32, (n_blk, 1), 0)
    head_id = lax.broadcasted_iota(jnp.int32, (n_blk, n_heads), 1)
    row = lax.broadcasted_iota(jnp.int32, o_ref.shape[1:], 0)
    col = lax.broadcasted_iota(jnp.int32, o_ref.shape[1:], 1)
    table = jnp.zeros(o_ref.shape[1:], F32)
    for qi in range(n_blk):
        bound = (FORGET_BOUND_SLACK * qmax_ref[0, qi:qi + 1, :] * k_term
                 + LOG2E * (cum_start[qi:qi + 1, :] - cum_end) + FORGET_BOUND_MARGIN)
        for g in range(n_heads // heads_per_group):
            worst = jnp.max(jnp.where(head_id // heads_per_group == g, bound, NEG_INF),
                            axis=1, keepdims=True)
            needed = jnp.logical_not(worst < -FORGET_SKIP_LOG2) | (chunk_id >= qi)
            first = jnp.min(jnp.where(needed, chunk_id, n_blk).astype(F32), axis=0, keepdims=True)
            table = jnp.where((row == qi) & (col == g), first, table)
    o_ref[0] = table.astype(jnp.int32)


def forget_skip_table(qmax, kmax, cum, *, heads_per_group):
    b, n_blk, n_heads = kmax.shape
    s_len = cum.shape[1]
    blk3 = lambda shape: pl.BlockSpec((1,) + shape, lambda i: (i, 0, 0))
    return pl.pallas_call(
        functools.partial(_forget_skip_kernel, heads_per_group=heads_per_group), grid=(b,),
        in_specs=[blk3((n_blk, n_heads)), blk3((n_blk, n_heads)), blk3((s_len, n_heads))],
        out_specs=blk3((n_blk, LANES)),
        out_shape=jax.ShapeDtypeStruct((b, n_blk, LANES), jnp.int32),
        compiler_params=_cparams("parallel"), name="forget_skip_table")(qmax, kmax, cum)


def _forget_kernel(c0_ref, q_ref, k_ref, v_ref, cq_ref, ck_ref, o_ref, v16, s_scr, p_scr, *,
                   tq, q_offset, max_chunks):
    group = pl.program_id(1)
    qi = pl.program_id(2)
    wg = k_ref.shape[2]
    ck_w = KEY_CHUNK
    ones_lane = _ones_lanes()
    heads = [(pp, j) for pp in range(wg // LANES) for j in range(HEADS_PER_VREG)]
    lanes = [slice(pp * LANES, (pp + 1) * LANES) for pp, _ in heads]

    @pl.when(qi == 0)
    def _():
        v = v_ref[0]
        for j in range(HEADS_PER_VREG):
            v16[j] = _masked_values(v, j)

    qmask = _head_masks((tq, LANES))
    q_pos = q_offset + qi * tq + lax.broadcasted_iota(jnp.int32, (tq, ck_w), 0)
    k_iota = lax.broadcasted_iota(jnp.int32, (tq, ck_w), 1)
    cq_all = cq_ref[0]
    head_lane = lax.broadcasted_iota(jnp.int32, cq_all.shape, 1)
    n_c = (q_offset + (qi + 1) * tq + ck_w - 1) // ck_w
    step = (pl.program_id(0) * pl.num_programs(1) + group) * pl.num_programs(2) + qi
    c0 = jnp.minimum(c0_ref[step], n_c - 1)
    n_proc_here = n_c - c0

    def attend(n_proc):
        cols = [slice(c * ck_w, (c + 1) * ck_w) for c in range(n_proc)]
        k0 = [pl.multiple_of((c0 + c) * ck_w, ck_w) for c in range(n_proc)]
        allowed = k0[-1] + k_iota <= q_pos
        qs = [jnp.where(qmask[j], q_ref[0, :, lanes[n]], jnp.zeros((), BF16))
              for n, (_, j) in enumerate(heads)]
        cq = [LOG2E * jnp.sum(jnp.where(head_lane == group * len(heads) + n, cq_all, 0.0),
                              axis=-1, keepdims=True) for n in range(len(heads))]

        def logits(n, c, mrun):
            s = (_dot_nt(qs[n], k_ref[0, pl.ds(k0[c], ck_w), lanes[n]])
                 - LOG2E * ck_ref[0, 0, n:n + 1, pl.ds(k0[c], ck_w)])
            if c == n_proc - 1:
                s = jnp.where(allowed, s, NEG_INF)
            s_scr[n % 2, :, cols[c]] = s
            for part in range(ck_w // LANES):
                mrun = jnp.maximum(mrun, s[:, part * LANES:(part + 1) * LANES])
            return mrun

        def row_term(n, mrun):
            m = jnp.max(mrun, axis=-1, keepdims=True) + cq[n]
            return cq[n] - m

        def probs(n, c, row):
            p_scr[n % 2, :, cols[c]] = jnp.exp2(s_scr[n % 2, :, cols[c]] + row).astype(BF16)

        def pv(n):
            return _dot(p_scr[n % 2, :, :n_proc * ck_w],
                        v16[heads[n][1], pl.ds(k0[0], n_proc * ck_w), lanes[n]])

        neg = jnp.full((tq, LANES), NEG_INF, F32)
        outs = []
        mrun = functools.reduce(lambda m, c: logits(0, c, m), range(n_proc), neg)
        for n in range(len(heads)):
            row = row_term(n, mrun)
            mrun = neg
            for c in range(n_proc):
                if n + 1 < len(heads):
                    mrun = logits(n + 1, c, mrun)
                probs(n, c, row)
            o = pv(n)
            lane = ones_lane[heads[n][1]]
            outs.append(o / o[:, lane:lane + 1])
        for n in range(0, len(heads), HEADS_PER_VREG):
            o_ref[0, :, lanes[n]] = jnp.where(qmask[0], outs[n], outs[n + 1]).astype(BF16)

    for n_proc in range(1, max_chunks + 1):
        pl.when(n_proc_here == n_proc)(functools.partial(attend, n_proc))


def forget_attn(q, k, v, cum_q, cum_k_rows, skip, *, tq, q_offset):
    b, sq, w = q.shape
    sk = k.shape[1]
    n_heads = cum_q.shape[2]
    ck_w = KEY_CHUNK
    diag = [-(-(q_offset + (t + 1) * tq) // ck_w) - 1 for t in range(sq // tq)]
    assert sk % ck_w == 0 and all((q_offset + t * tq) // ck_w == c for t, c in enumerate(diag))
    max_chunks = max(diag) + 1
    wg = ATTN_PAIRS_PER_STEP * LANES
    heads_per_step = ATTN_PAIRS_PER_STEP * HEADS_PER_VREG
    assert skip.shape == (b, w // wg, sq // tq)
    ck = cum_k_rows.reshape(b, n_heads // heads_per_step, heads_per_step, sk)
    kern = functools.partial(_forget_kernel, tq=tq, q_offset=q_offset, max_chunks=max_chunks)
    grid_spec = pltpu.PrefetchScalarGridSpec(
        num_scalar_prefetch=1, grid=(b, w // wg, sq // tq),
        in_specs=[pl.BlockSpec((1, tq, wg), lambda i, h, t, c0: (i, t, h)),
                  pl.BlockSpec((1, sk, wg), lambda i, h, t, c0: (i, 0, h)),
                  pl.BlockSpec((1, sk, wg), lambda i, h, t, c0: (i, 0, h)),
                  pl.BlockSpec((1, tq, n_heads), lambda i, h, t, c0: (i, t, 0)),
                  pl.BlockSpec((1, 1, heads_per_step, sk), lambda i, h, t, c0: (i, h, 0, 0))],
        out_specs=pl.BlockSpec((1, tq, wg), lambda i, h, t, c0: (i, t, h)),
        scratch_shapes=[pltpu.VMEM((HEADS_PER_VREG, sk, wg), BF16),
                        pltpu.VMEM((2, tq, max_chunks * ck_w), F32),
                        pltpu.VMEM((2, tq, max_chunks * ck_w), BF16)])
    return pl.pallas_call(
        kern, grid_spec=grid_spec, out_shape=jax.ShapeDtypeStruct((b, sq, w), BF16),
        compiler_params=_cparams("parallel", "parallel", "arbitrary"),
        name="forget_attn")(skip.reshape(-1), q, k, v, cum_q, ck)


def _layer_tail_kernel(*refs, n_in, n_heads, rows_per_seq, tf, final_norm):
    a_refs, w_refs = refs[:n_in], refs[n_in:2 * n_in]
    (x_ref, gx_ref, wq_ref, wo_ref, mk_ref, mv_ref, gf_ref, wg_ref, wu_ref, wd_ref, gfin_ref,
     o_ref) = refs[2 * n_in:]
    x = x_ref[...]
    for a_ref, w_ref in zip(a_refs, w_refs):
        x = x + _dot(a_ref[...], w_ref[...])

    hd = x.shape[1] // n_heads
    q = (_dot(_rms_bf16(x, gx_ref[...]), wq_ref[...]) * (hd ** -0.5)).astype(BF16)
    per_seq = []
    for b in range(mk_ref.shape[0]):
        rows = slice(b * rows_per_seq, (b + 1) * rows_per_seq)
        outs = []
        for j in range(n_heads):
            cols = slice(j * hd, (j + 1) * hd)
            s = _dot_nt(q[rows, cols], mk_ref[b, :, cols])
            p = jnp.exp(s - jnp.max(s, axis=-1, keepdims=True))
            p = p / jnp.sum(p, axis=-1, keepdims=True)
            outs.append(_dot(p.astype(BF16), mv_ref[b, :, cols]).astype(BF16))
        per_seq.append(jnp.concatenate(outs, axis=-1))
    x = x + _dot(jnp.concatenate(per_seq, axis=0), wo_ref[...])

    h = _rms_bf16(x, gf_ref[...])
    for c in range(wg_ref.shape[1] // tf):
        cols = slice(c * tf, (c + 1) * tf)
        gate = _dot(h, wg_ref[:, cols])
        up = _dot(h, wu_ref[:, cols])
        a = (gate * jax.nn.sigmoid(gate) * up).astype(BF16)
        x = x + _dot(a, wd_ref[cols, :])
    if final_norm:
        x = x * lax.rsqrt(jnp.mean(x * x, axis=-1, keepdims=True) + RMS_EPS) * gfin_ref[...]
    o_ref[...] = x


def layer_tail(a_list, w_list, x, g_x, wq, wo, mk, mv, g_f, wg, wu, wd, g_final, *, seq, tm, tf,
               final_norm):
    m, d = x.shape
    n_mem = mk.shape[1]
    rows_per_seq = min(tm, seq)
    seqs = tm // rows_per_seq
    assert seq % rows_per_seq == 0 and m % tm == 0
    row = lambda i: (i, 0)
    mem = lambda i: (i * tm // (seq * seqs), 0, 0)
    gain = lambda g: g.reshape(1, d)
    kern = functools.partial(_layer_tail_kernel, n_in=len(a_list), n_heads=XA_HEADS,
                             rows_per_seq=rows_per_seq, tf=tf, final_norm=final_norm)
    in_specs = ([pl.BlockSpec((tm, a.shape[1]), row) for a in a_list]
                + [_resident_spec(w.shape) for w in w_list]
                + [pl.BlockSpec((tm, d), row), _resident_spec((1, d)),
                   _resident_spec(wq.shape), _resident_spec(wo.shape),
                   pl.BlockSpec((seqs, n_mem, d), mem), pl.BlockSpec((seqs, n_mem, d), mem),
                   _resident_spec((1, d)), _resident_spec(wg.shape), _resident_spec(wu.shape),
                   _resident_spec(wd.shape), _resident_spec((1, d))])
    return pl.pallas_call(
        kern, grid=(m // tm,), in_specs=in_specs,
        out_specs=pl.BlockSpec((tm, d), row), out_shape=jax.ShapeDtypeStruct((m, d), F32),
        compiler_params=_cparams("parallel"), name="layer_tail")(
            *a_list, *w_list, x, gain(g_x), wq, wo, mk, mv, gain(g_f), wg, wu, wd, gain(g_final))


def _cast_kernel(x_ref, o_ref):
    o_ref[...] = x_ref[0].astype(BF16)


def to_bf16(w, layer):
    _, rows, cols = w.shape
    tr = next((t for t in (ROW_TILE, ROW_TILE // 2, ROW_TILE // 4) if rows % t == 0), rows)
    return pl.pallas_call(
        _cast_kernel, grid=(rows // tr,),
        in_specs=[pl.BlockSpec((1, tr, cols), lambda i: (layer, i, 0))],
        out_specs=pl.BlockSpec((tr, cols), lambda i: (i, 0)),
        out_shape=jax.ShapeDtypeStruct((rows, cols), BF16),
        compiler_params=_cparams("parallel"), name="to_bf16")(w)


def _pad_rows(a, rows):
    return jnp.pad(a, ((0, 0), (0, rows - a.shape[1]), (0, 0)))


def _row_tile(m, seq, cap):
    tm = min(cap, seq)
    assert seq % tm == 0 and m % tm == 0
    return tm


def _layer_ab(x, seq, g, w_in, w_out, rel_bias, cache, *, tm):
    m, d = x.shape
    b = m // seq
    wa = w_in.shape[1] // 6
    ws = [w_in[:, n * wa:(n + 1) * wa].astype(BF16) for n in range(6)]
    scale = HEAD_DIM ** -0.5 * LOG2E
    kept = 'tail' if min(A_PAST, seq) < seq else 'heads'
    outs = [(4, 'heads', HEAD_DIM), (4, 'bf16', 1.0), (5, 'heads', HEAD_DIM), (5, 'bf16', 1.0),
            (0, 'bf16', scale), (3, 'bf16', scale), (1, 'bf16', 1.0), (2, 'bf16', 1.0),
            (1, kept, HEAD_DIM), (2, kept, HEAD_DIM)]
    kb, kb16, vb, vb16, qa, qb, ka16, va16, ka_keep, va_keep = norm_proj(x, g, ws, outs, seq=seq,
                                                                         tm=tm)
    shp = lambda a: a.reshape(b, -1, a.shape[-1])
    if cache is None:
        tq = BAND_ROWS
        band = -(-(A_PAST + tq) // KEY_CHUNK) * KEY_CHUNK
        offsets = sorted({t * tq - max(t * (tq // CHUNK) - A_PAST_CHUNKS, 0) * CHUNK
                          for t in range(seq // tq)})
        bias = band_bias(rel_bias, offsets, rows=tq, cols=band, n_real=band)
        oa = band_attn(shp(qa), shp(ka16), shp(va16), bias, tq=tq, band=band)
        ob = stick_attn(shp(qb), shp(kb16), shp(vb16), tq=KEY_CHUNK, q_offset=0)
    else:
        ca_k, ca_v, cb_k, cb_v = cache
        n_past = ca_k.shape[1]
        n_keys = n_past + seq
        band = -(-n_keys // KEY_CHUNK) * KEY_CHUNK
        flat = lambda a: a.reshape(a.shape[0], a.shape[1], -1)
        k_all = _pad_rows(jnp.concatenate([flat(ca_k).astype(BF16), shp(ka16)], axis=1), band)
        v_all = _pad_rows(jnp.concatenate([flat(ca_v).astype(BF16), shp(va16)], axis=1), band)
        bias = band_bias(rel_bias, [n_past], rows=seq, cols=band, n_real=n_keys)
        oa = band_attn(shp(qa), k_all, v_all, bias, tq=seq, band=band)
        n_pastb = cb_k.shape[1]
        sk = -(-(n_pastb + seq) // KEY_CHUNK) * KEY_CHUNK
        kb_all = _pad_rows(jnp.concatenate([flat(cb_k).astype(BF16), shp(kb16)], axis=1), sk)
        vb_all = _pad_rows(jnp.concatenate([flat(cb_v).astype(BF16), shp(vb16)], axis=1), sk)
        ob = stick_attn(shp(qb), kb_all, vb_all, tq=seq, q_offset=n_pastb)
    w_out = w_out.astype(BF16)
    wo_a, wo_b = w_out[:oa.shape[-1]], w_out[oa.shape[-1]:]
    return ([oa.reshape(m, -1), ob.reshape(m, -1)], [wo_a, wo_b]), (ka_keep, va_keep, kb, vb)


def _layer_c(x, seq, g, w_in, b_f, w_out, cache, *, tm):
    m, d = x.shape
    b = m // seq
    n_heads = b_f.shape[0]
    wq, wk, wv, wf = (w_in[:, :d], w_in[:, d:2 * d], w_in[:, 2 * d:3 * d], w_in[:, 3 * d:])
    ws = [w.astype(BF16) for w in (wq, wk, wv, wf)]
    scale = HEAD_DIM ** -0.5 * LOG2E
    outs = [(1, 'heads', HEAD_DIM), (1, 'bf16', 1.0), (2, 'heads', HEAD_DIM), (2, 'bf16', 1.0),
            (0, 'bf16', scale), (3, 'logf', None)]
    if cache is None:
        outs += [(1, 'rowmax', (HEAD_DIM, 1.0)), (0, 'rowmax', (HEAD_DIM, scale))]
    k, k16, v, v16, q, lf, *norms = norm_proj(x, g, ws, outs, seq=seq, tm=tm,
                                              bias=b_f.reshape(1, n_heads))
    norms = norms[::-1]
    shp = lambda a: a.reshape(b, -1, a.shape[-1])
    heads_per_group = ATTN_PAIRS_PER_STEP * HEADS_PER_VREG
    n_groups = n_heads // heads_per_group
    if cache is None:
        cum_col, cum_row = cumsum_logf(shp(lf))
        tq = KEY_CHUNK
        qmax, kmax = (a.reshape(b, seq // KEY_CHUNK, n_heads) for a in norms)
        skip = forget_skip_table(qmax, kmax, cum_col, heads_per_group=heads_per_group)
        skip = jnp.swapaxes(skip[:, :, :n_groups], 1, 2)
        o = forget_attn(shp(q), shp(k16), shp(v16), cum_col, cum_row, skip, tq=tq, q_offset=0)
    else:
        c_k, c_v, c_lf = cache
        n_past = c_k.shape[1]
        sk = -(-(n_past + seq) // KEY_CHUNK) * KEY_CHUNK
        flat = lambda a: a.reshape(a.shape[0], a.shape[1], -1)
        k_all = _pad_rows(jnp.concatenate([flat(c_k).astype(BF16), shp(k16)], axis=1), sk)
        v_all = _pad_rows(jnp.concatenate([flat(c_v).astype(BF16), shp(v16)], axis=1), sk)
        lf_all = _pad_rows(jnp.concatenate([c_lf, shp(lf)], axis=1), sk)
        cum_col, cum_row = cumsum_logf(lf_all)
        o = forget_attn(shp(q), k_all, v_all, cum_col[:, n_past:n_past + seq], cum_row,
                        jnp.zeros((b, n_groups, 1), jnp.int32), tq=seq, q_offset=n_past)
    return ([o.reshape(m, d)], [w_out.astype(BF16)]), (k, v, lf)


def kernel(x_prompt, x_sample, cache_a_k, cache_a_v, cache_b_k, cache_b_v, cache_c_k, cache_c_v, cache_c_logf, cache_mem_k, cache_mem_v, mem_prompt, w_in_ab, w_out_ab, rel_bias_a, w_in_c, b_f_c, w_out_c, g_mix, g_xattn, g_mem, w_xq, w_xk, w_xv, w_xo, g_ffn, w_gate, w_up, w_down, g_final):
    bp, sp, d = x_prompt.shape
    bs, ss, _ = x_sample.shape
    depth = g_mix.shape[0]
    n_mem = mem_prompt.shape[1]
    xp = x_prompt.reshape(bp * sp, d)
    xs = x_sample.reshape(bs * ss, d)
    tmp = _row_tile(bp * sp, sp, ROW_TILE)
    tms = bs * ss
    assert tms <= ROW_TILE
    mem = mem_prompt.reshape(bp * n_mem, d)
    tmm = min(ROW_TILE, bp * n_mem)
    assert (bp * n_mem) % tmm == 0
    dff = w_gate.shape[2]
    tf = 256 if dff % 256 == 0 else dff

    a_kp, a_vp, b_kp, b_vp, a_ks, a_vs, b_ks, b_vs = [], [], [], [], [], [], [], []
    c_kp, c_vp, c_lfp, c_ks, c_vs, c_lfs = [], [], [], [], [], []
    mem_kp, mem_vp = [], []
    for layer in range(depth):
        if layer % 2 == 0:
            e = layer // 2
            mix_p, (ka, va, kb, vb) = _layer_ab(xp, sp, g_mix[layer], w_in_ab[e], w_out_ab[e],
                                                rel_bias_a[e], None, tm=tmp)
            a_kp.append(ka); a_vp.append(va); b_kp.append(kb); b_vp.append(vb)
            mix_s, (ka, va, kb, vb) = _layer_ab(
                xs, ss, g_mix[layer], w_in_ab[e], w_out_ab[e], rel_bias_a[e],
                (cache_a_k[e], cache_a_v[e], cache_b_k[e], cache_b_v[e]), tm=tms)
            a_ks.append(ka); a_vs.append(va); b_ks.append(kb); b_vs.append(vb)
        else:
            c = layer // 2
            mix_p, (k, v, lf) = _layer_c(xp, sp, g_mix[layer], w_in_c[c], b_f_c[c], w_out_c[c],
                                         None, tm=tmp)
            c_kp.append(k); c_vp.append(v); c_lfp.append(lf)
            mix_s, (k, v, lf) = _layer_c(xs, ss, g_mix[layer], w_in_c[c], b_f_c[c], w_out_c[c],
                                         (cache_c_k[c], cache_c_v[c], cache_c_logf[c]), tm=tms)
            c_ks.append(k); c_vs.append(v); c_lfs.append(lf)
        mk, mk16, mv, mv16 = norm_proj(
            mem, g_mem[layer], [to_bf16(w_xk, layer), to_bf16(w_xv, layer)],
            [(0, 'heads', d // XA_HEADS), (0, 'bf16', 1.0), (1, 'heads', d // XA_HEADS),
             (1, 'bf16', 1.0)],
            seq=n_mem, tm=tmm)
        mem_kp.append(mk); mem_vp.append(mv)
        tail = functools.partial(
            layer_tail, g_x=g_xattn[layer], wq=to_bf16(w_xq, layer), wo=to_bf16(w_xo, layer),
            g_f=g_ffn[layer], wg=to_bf16(w_gate, layer), wu=to_bf16(w_up, layer),
            wd=to_bf16(w_down, layer), g_final=g_final, tf=tf,
            final_norm=layer == depth - 1)
        xp = tail(*mix_p, xp, mk=mk16.reshape(bp, n_mem, d), mv=mv16.reshape(bp, n_mem, d),
                  seq=sp, tm=tmp)
        xs = tail(*mix_s, xs, mk=cache_mem_k[layer].reshape(bs, n_mem, d).astype(BF16),
                  mv=cache_mem_v[layer].reshape(bs, n_mem, d).astype(BF16), seq=ss, tm=tms)

    hd = HEAD_DIM
    xa_hd = d // XA_HEADS
    r5 = lambda lst, b, s, dd: jnp.stack([a.reshape(b, s, -1, dd) for a in lst])
    r4 = lambda lst, b, s: jnp.stack([a.reshape(b, s, -1) for a in lst])
    keep = min(A_PAST, sp)
    return (xp.reshape(bp, sp, d), xs.reshape(bs, ss, d),
            r5(a_kp, bp, keep, hd), r5(a_vp, bp, keep, hd), r5(b_kp, bp, sp, hd), r5(b_vp, bp, sp, hd),
            r5(c_kp, bp, sp, hd), r5(c_vp, bp, sp, hd), r4(c_lfp, bp, sp),
            r5(mem_kp, bp, n_mem, xa_hd), r5(mem_vp, bp, n_mem, xa_hd),
            r5(a_ks, bs, ss, hd), r5(a_vs, bs, ss, hd), r5(b_ks, bs, ss, hd), r5(b_vs, bs, ss, hd),
            r5(c_ks, bs, ss, hd), r5(c_vs, bs, ss, hd), r4(c_lfs, bs, ss))
```

```python
import functools

import jax
import jax.numpy as jnp
from jax import lax
from jax.experimental import pallas as pl
from jax.experimental.pallas import tpu as pltpu

F32 = jnp.float32
BF16 = jnp.bfloat16

RMS_EPS = 1e-6
NEG_INF = -1e30
LOG2E = 1.4426950408889634
HEAD_DIM = 64
CHUNK = 64
A_PAST_CHUNKS = 8
A_PAST = A_PAST_CHUNKS * CHUNK
REL_CLIP = 128
XA_HEADS = 4

LANES = 128
HEADS_PER_VREG = LANES // HEAD_DIM
KEY_CHUNK = 256
ATTN_PAIRS_PER_STEP = 4
BAND_ROWS = 4 * CHUNK
ROW_TILE = 512
VMEM_LIMIT = 56 * 1024 * 1024


def _cparams(*sem):
    return pltpu.CompilerParams(dimension_semantics=sem, vmem_limit_bytes=VMEM_LIMIT)


def _resident_spec(shape):
    return pl.BlockSpec(shape, lambda *_: (0,) * len(shape), pipeline_mode=pl.Buffered(1))


def _rms_bf16(x, g):
    y = x * lax.rsqrt(jnp.mean(x * x, axis=-1, keepdims=True) + RMS_EPS)
    return (y * g).astype(BF16)


def _log_sigmoid(z):
    return jnp.minimum(z, 0.0) - jnp.log1p(jnp.exp(-jnp.abs(z)))


def _split3(x):
    hi = x.astype(BF16)
    r = x - hi.astype(F32)
    mid = r.astype(BF16)
    lo = (r - mid.astype(F32)).astype(BF16)
    return hi, mid, lo


def _split2(x):
    hi = x.astype(BF16)
    lo = (x - hi.astype(F32)).astype(BF16)
    return hi, lo


def _dot(a, b):
    return jnp.dot(a, b, preferred_element_type=F32)


def _dot_nt(a, b):
    return lax.dot_general(a, b, (((1,), (1,)), ((), ())), preferred_element_type=F32)


def _norm_proj_kernel(x_ref, g_ref, b_ref, *refs, n_w, outs, tiles_per_seq):
    w_refs, o_refs = refs[:n_w], refs[n_w:]
    h = _rms_bf16(x_ref[...], g_ref[...])
    ys = {}
    for o_ref, (grp, kind, arg) in zip(o_refs, outs):
        if grp not in ys:
            ys[grp] = _dot(h, w_refs[grp][...])
        y = ys[grp]
        if kind == 'f32':
            o_ref[...] = y
        elif kind == 'bf16':
            o_ref[...] = (y * arg).astype(BF16)
        elif kind == 'logf':
            o_ref[...] = _log_sigmoid(y + b_ref[...])
        elif kind == 'heads':
            o_ref[...] = y.reshape(o_ref.shape)
        elif kind == 'rowmax':
            hd, scale = arg
            n_cols, n_heads = y.shape[1], y.shape[1] // hd
            sel_l = lax.broadcasted_iota(jnp.int32, (n_cols, n_heads), 0) // hd
            sel_h = lax.broadcasted_iota(jnp.int32, (n_cols, n_heads), 1)
            scaled = y * scale
            norms = jnp.sqrt(_dot((scaled * scaled).astype(BF16),
                                  jnp.where(sel_l == sel_h, 1.0, 0.0).astype(BF16)))
            for r in range(o_ref.shape[1]):
                o_ref[0, r:r + 1, :] = jnp.max(norms[r * KEY_CHUNK:(r + 1) * KEY_CHUNK],
                                               axis=0, keepdims=True)
        else:
            @pl.when(pl.program_id(0) % tiles_per_seq == tiles_per_seq - 1)
            def _(o_ref=o_ref, y=y):
                o_ref[...] = y.reshape(o_ref.shape)


def norm_proj(x, g, ws, outs, *, seq, tm, bias=None):
    m, d = x.shape
    tiles_per_seq = max(seq // tm, 1)
    if bias is None:
        bias = jnp.zeros((1, 16), F32)
    in_specs = [pl.BlockSpec((tm, d), lambda i: (i, 0)),
                pl.BlockSpec((1, d), lambda i: (0, 0)),
                pl.BlockSpec(bias.shape, lambda i: (0, 0))]
    in_specs += [_resident_spec(w.shape) for w in ws]
    out_shape, out_specs = [], []
    for grp, kind, arg in outs:
        n = ws[grp].shape[1]
        if kind == 'tail':
            assert tm == min(A_PAST, seq)
            out_shape.append(jax.ShapeDtypeStruct((m // tiles_per_seq, n // arg, arg), F32))
            out_specs.append(pl.BlockSpec((tm, n // arg, arg), lambda i: (i // tiles_per_seq, 0, 0)))
        elif kind == 'heads':
            out_shape.append(jax.ShapeDtypeStruct((m, n // arg, arg), F32))
            out_specs.append(pl.BlockSpec((tm, n // arg, arg), lambda i: (i, 0, 0)))
        elif kind == 'rowmax':
            assert tm % KEY_CHUNK == 0
            blocks = (tm // KEY_CHUNK, n // arg[0])
            out_shape.append(jax.ShapeDtypeStruct((m // tm,) + blocks, F32))
            out_specs.append(pl.BlockSpec((1,) + blocks, lambda i: (i, 0, 0)))
        else:
            out_shape.append(jax.ShapeDtypeStruct((m, n), BF16 if kind == 'bf16' else F32))
            out_specs.append(pl.BlockSpec((tm, n), lambda i: (i, 0)))
    kern = functools.partial(_norm_proj_kernel, n_w=len(ws), outs=tuple(outs),
                             tiles_per_seq=tiles_per_seq)
    return pl.pallas_call(
        kern, grid=(m // tm,), in_specs=in_specs, out_specs=out_specs, out_shape=out_shape,
        compiler_params=_cparams("arbitrary"), name="norm_proj")(x, g.reshape(1, d), bias, *ws)


BIAS_BLOCK_ROWS = 32


def _band_bias_kernel(rb_ref, off_ref, o_ref, *, n_real):
    n_heads, rows, cols = o_ref.shape[1:]
    off = off_ref[pl.program_id(0)]
    i0 = pl.program_id(1) * rows
    r0 = pl.program_id(2) * cols
    i = lax.broadcasted_iota(jnp.int32, (rows, cols), 0) + i0
    r = lax.broadcasted_iota(jnp.int32, (rows, cols), 1) + r0
    d = jnp.clip(off + i - r, -REL_CLIP, REL_CLIP) + REL_CLIP
    lo = jnp.clip(off + i0 - (r0 + cols - 1), -REL_CLIP, REL_CLIP) + REL_CLIP
    hi = jnp.clip(off + i0 + rows - 1 - r0, -REL_CLIP, REL_CLIP) + REL_CLIP
    first = (i // CHUNK - A_PAST_CHUNKS) * CHUNK + off
    last = jnp.minimum((i // CHUNK + 1) * CHUNK + off, n_real)
    visible = (r >= first) & (r < last)
    first0 = (i0 // CHUNK - A_PAST_CHUNKS) * CHUNK + off
    last0 = jnp.minimum(((i0 + rows - 1) // CHUNK + 1) * CHUNK + off, n_real)
    hi = jnp.where((r0 + cols <= first0) | (r0 >= last0), lo - 1, hi)

    def body(u, tbls):
        hit = d == u
        return tuple(jnp.where(hit, rb_ref[h, u], t) for h, t in enumerate(tbls))

    tbls = lax.fori_loop(lo, hi + 1, body,
                         tuple(jnp.zeros((rows, cols), F32) for _ in range(n_heads)))
    for h in range(n_heads):
        o_ref[0, h] = jnp.where(visible, LOG2E * tbls[h], NEG_INF)


def band_bias(rel_bias, offsets, *, rows, cols, n_real):
    n_rel, n_heads = rel_bias.shape
    blk = min(rows, BIAS_BLOCK_ROWS)
    return pl.pallas_call(
        functools.partial(_band_bias_kernel, n_real=n_real),
        grid=(len(offsets), rows // blk, cols // LANES),
        in_specs=[pl.BlockSpec(memory_space=pltpu.SMEM), pl.BlockSpec(memory_space=pltpu.SMEM)],
        out_specs=pl.BlockSpec((1, n_heads, blk, LANES), lambda o, t, c: (o, 0, t, c)),
        out_shape=jax.ShapeDtypeStruct((len(offsets), n_heads, rows, cols), F32),
        compiler_params=_cparams("arbitrary", "arbitrary", "arbitrary"),
        name="band_bias")(rel_bias.T, jnp.asarray(offsets, jnp.int32))


def _head_masks(shape):
    lane = lax.broadcasted_iota(jnp.int32, shape, len(shape) - 1)
    return [(lane % LANES) // HEAD_DIM == j for j in range(HEADS_PER_VREG)]


def _ones_lanes():
    return [((j + 1) % HEADS_PER_VREG) * HEAD_DIM for j in range(HEADS_PER_VREG)]


def _masked_values(v, j):
    lane = lax.broadcasted_iota(jnp.int32, v.shape, 1) % LANES
    vj = jnp.where(lane // HEAD_DIM == j, v, jnp.zeros((), BF16))
    return jnp.where(lane == _ones_lanes()[j], jnp.ones((), BF16), vj)


def _band_attn_kernel(q_ref, k_ref, v_ref, bias_ref, o_ref, v16, s_scr, p_scr, *, tq, band):
    step = pl.program_id(1)
    ck = KEY_CHUNK

    @pl.when(step == 0)
    def _():
        v = v_ref[0]
        for j in range(HEADS_PER_VREG):
            v16[j] = _masked_values(v, j)

    start = pl.multiple_of(jnp.maximum(step * (tq // CHUNK) - A_PAST_CHUNKS, 0) * CHUNK, CHUNK)
    qmask = _head_masks((tq, LANES))
    ones_lane = _ones_lanes()
    cols = [slice(c * ck, (c + 1) * ck) for c in range(band // ck)]
    heads = [(hp, j) for hp in range(q_ref.shape[2] // LANES) for j in range(HEADS_PER_VREG)]
    lanes = [slice(hp * LANES, (hp + 1) * LANES) for hp, _ in heads]
    qs = [jnp.where(qmask[j], q_ref[0, :, lanes[n]], jnp.zeros((), BF16))
          for n, (_, j) in enumerate(heads)]

    def logits(n, c, mrun):
        k = k_ref[0, pl.ds(pl.multiple_of(start + c * ck, CHUNK), ck), lanes[n]]
        s = _dot_nt(qs[n], k) + bias_ref[0, n, :, cols[c]]
        s_scr[n % 2, :, cols[c]] = s
        for part in range(ck // LANES):
            mrun = jnp.maximum(mrun, s[:, part * LANES:(part + 1) * LANES])
        return mrun

    def probs(n, c, row_max):
        p_scr[n % 2, :, cols[c]] = jnp.exp2(s_scr[n % 2, :, cols[c]] - row_max).astype(BF16)

    def pv(n):
        return _dot(p_scr[n % 2], v16[heads[n][1], pl.ds(start, band), lanes[n]])

    neg = jnp.full((tq, LANES), NEG_INF, F32)
    outs = []
    mrun = functools.reduce(lambda m, c: logits(0, c, m), range(len(cols)), neg)
    for n in range(len(heads)):
        row_max = jnp.max(mrun, axis=-1, keepdims=True)
        mrun = neg
        for c in range(len(cols)):
            if n + 1 < len(heads):
                mrun = logits(n + 1, c, mrun)
            probs(n, c, row_max)
        o = pv(n)
        lane = ones_lane[heads[n][1]]
        outs.append(o / o[:, lane:lane + 1])
    for n in range(0, len(heads), HEADS_PER_VREG):
        o_ref[0, :, lanes[n]] = jnp.where(qmask[0], outs[n], outs[n + 1]).astype(BF16)


def band_attn(q, k, v, bias, *, tq, band):
    b, sq, w = q.shape
    sk = k.shape[1]
    n_off, n_heads = bias.shape[:2]
    assert band % KEY_CHUNK == 0 and tq % CHUNK == 0 or sq == tq
    kern = functools.partial(_band_attn_kernel, tq=tq, band=band)
    return pl.pallas_call(
        kern, grid=(b, sq // tq),
        in_specs=[pl.BlockSpec((1, tq, w), lambda i, c: (i, c, 0)),
                  pl.BlockSpec((1, sk, w), lambda i, c: (i, 0, 0)),
                  pl.BlockSpec((1, sk, w), lambda i, c: (i, 0, 0)),
                  pl.BlockSpec((1, n_heads, tq, band),
                               lambda i, c: (jnp.minimum(c, n_off - 1), 0, 0, 0))],
        out_specs=pl.BlockSpec((1, tq, w), lambda i, c: (i, c, 0)),
        out_shape=jax.ShapeDtypeStruct((b, sq, w), BF16),
        scratch_shapes=[pltpu.VMEM((HEADS_PER_VREG, sk, w), BF16),
                        pltpu.VMEM((2, tq, band), F32),
                        pltpu.VMEM((2, tq, band), BF16)],
        compiler_params=_cparams("parallel", "arbitrary"), name="band_attn")(q, k, v, bias)


STICK_UNDERFLOW_LOG2 = 160.0
STICK_NEAR_CHUNKS = 2


def _stick_kernel(q_ref, k_ref, v_ref, o_ref, v16, zl_scr, hl_scr, w_scr, *,
                  tq, q_offset, variants):
    qi = pl.program_id(2)
    ck = KEY_CHUNK
    heads = [(pp, j) for pp in range(q_ref.shape[2] // LANES) for j in range(HEADS_PER_VREG)]
    n_heads = len(heads)
    lanes = [slice(pp * LANES, (pp + 1) * LANES) for pp, _ in heads]

    @pl.when(qi == 0)
    def _():
        v = v_ref[0]
        vmask = _head_masks(v.shape)
        for j in range(HEADS_PER_VREG):
            v16[j] = jnp.where(vmask[j], v, jnp.zeros((), BF16))

    qmask = _head_masks((tq, LANES))
    qs = [jnp.where(qmask[j], q_ref[0, :, lanes[n]], jnp.zeros((), BF16))
          for n, (_, j) in enumerate(heads)]
    q_pos = q_offset + qi * tq + lax.broadcasted_iota(jnp.int32, (tq, ck), 0)
    k_iota = lax.broadcasted_iota(jnp.int32, (tq, ck), 1)
    rr = lax.broadcasted_iota(jnp.int32, (2 * ck, ck), 0) % ck
    cc = lax.broadcasted_iota(jnp.int32, (2 * ck, ck), 1)
    tri2 = jnp.where(rr > cc, 1.0, 0.0).astype(BF16)
    n_chunks = (q_offset + (qi + 1) * tq + ck - 1) // ck

    def split_cols(slot):
        return slice(2 * slot.start, 2 * slot.start + ck), slice(2 * slot.start + ck, 2 * slot.stop)

    def logits(j, slot, c, masked):
        k0 = pl.multiple_of(c * ck, ck)
        z = _dot_nt(qs[j], k_ref[0, pl.ds(k0, ck), lanes[j]])
        neg_abs = lax.bitcast_convert_type(
            lax.bitcast_convert_type(z, jnp.int32) | jnp.int32(-2 ** 31), F32)
        sp = jnp.maximum(z, 0.0) + jnp.log2(1.0 + jnp.exp2(neg_abs))
        if masked:
            sp = jnp.where(k0 + k_iota < q_pos, sp, 0.0)
        zl_scr[j, :, slot] = z - sp
        hi_cols, lo_cols = split_cols(slot)
        hl_scr[j, :, hi_cols], hl_scr[j, :, lo_cols] = _split2(sp)
        return jnp.sum(sp, axis=-1, keepdims=True)

    def weights(j, slot, c, later, masked):
        sums = _dot(hl_scr[j, :, 2 * slot.start:2 * slot.stop], tri2)
        w = jnp.exp2(zl_scr[j, :, slot] - sums - later)
        if masked:
            w = jnp.where(c * ck + k_iota < q_pos, w, 0.0)
        w_scr[j, :, slot] = w.astype(BF16)

    def more(laters):
        return (jnp.min(functools.reduce(jnp.minimum, laters))
                < STICK_UNDERFLOW_LOG2).astype(jnp.int32)

    def attend(near, n_masked):
        slots = [slice((near - 1 - i) * ck, (near - i) * ck) for i in range(near)]
        chunk = [n_chunks - 1 - i for i in range(near)]
        k0 = pl.multiple_of((n_chunks - near) * ck, ck)

        def pv(n, k0, width):
            return _dot(w_scr[n, :, :width], v16[heads[n][1], pl.ds(k0, width), lanes[n]])

        laters, accs = [], []
        row_sums = [logits(0, slots[i], chunk[i], i < n_masked) for i in range(near)]
        for n in range(n_heads):
            later = jnp.zeros((tq, 1), F32)
            next_sums = []
            for i in range(near):
                if n + 1 < n_heads:
                    next_sums.append(logits(n + 1, slots[i], chunk[i], i < n_masked))
                weights(n, slots[i], chunk[i], later, i < n_masked)
                later = later + row_sums[i]
            laters.append(later)
            row_sums = next_sums
            o = pv(n, k0, near * ck)
            if heads[n][1] == 0:
                accs.append(o)
            else:
                accs[-1] = accs[-1] + o

        def cond(carry):
            return (carry[0] >= 0) & (carry[1] > 0)

        def body(carry):
            c, _, accs, laters = carry
            accs = list(accs)
            k0 = pl.multiple_of(c * ck, ck)
            new = []
            for n in range(n_heads):
                rs = logits(n, slice(0, ck), c, False)
                weights(n, slice(0, ck), c, laters[n], False)
                accs[heads[n][0]] = accs[heads[n][0]] + pv(n, k0, ck)
                new.append(laters[n] + rs)
            return c - 1, more(new), tuple(accs), tuple(new)

        carry = lax.while_loop(cond, body,
                               (n_chunks - 1 - near, more(laters), tuple(accs), tuple(laters)))
        for pp, acc in enumerate(carry[2]):
            o_ref[0, :, pp * LANES:(pp + 1) * LANES] = acc.astype(BF16)

    for n_total, near, n_masked in variants:
        if n_total is None:
            pl.when(n_chunks >= near)(functools.partial(attend, near, n_masked))
        else:
            pl.when(n_chunks == n_total)(functools.partial(attend, near, n_masked))


def stick_attn(q, k, v, *, tq, q_offset):
    b, sq, w = q.shape
    sk = k.shape[1]
    ck = KEY_CHUNK
    assert q_offset % ck == 0 and (tq % ck == 0 or sq == tq <= ck) and sk % ck == 0
    n_masked = -(-tq // ck)
    totals = sorted({-(-(q_offset + (t + 1) * tq) // ck) for t in range(sq // tq)})
    variants = [(n, n, min(n_masked, n)) for n in totals if n < STICK_NEAR_CHUNKS]
    if totals[-1] >= STICK_NEAR_CHUNKS:
        variants.append((None, STICK_NEAR_CHUNKS, n_masked))
    near_cols = STICK_NEAR_CHUNKS * ck
    wg = ATTN_PAIRS_PER_STEP * LANES
    n_heads = ATTN_PAIRS_PER_STEP * HEADS_PER_VREG
    kern = functools.partial(_stick_kernel, tq=tq, q_offset=q_offset, variants=tuple(variants))
    return pl.pallas_call(
        kern, grid=(b, w // wg, sq // tq),
        in_specs=[pl.BlockSpec((1, tq, wg), lambda i, h, t: (i, t, h)),
                  pl.BlockSpec((1, sk, wg), lambda i, h, t: (i, 0, h)),
                  pl.BlockSpec((1, sk, wg), lambda i, h, t: (i, 0, h))],
        out_specs=pl.BlockSpec((1, tq, wg), lambda i, h, t: (i, t, h)),
        out_shape=jax.ShapeDtypeStruct((b, sq, w), BF16),
        scratch_shapes=[pltpu.VMEM((HEADS_PER_VREG, sk, wg), BF16),
                        pltpu.VMEM((n_heads, tq, near_cols), F32),
                        pltpu.VMEM((n_heads, tq, 2 * near_cols), BF16),
                        pltpu.VMEM((n_heads, tq, near_cols), BF16)],
        compiler_params=_cparams("parallel", "parallel", "arbitrary"), name="stick_attn")(q, k, v)


def _cumsum_kernel(lf_ref, col_ref, row_ref, *, blk):
    s_len, n_heads = lf_ref.shape[1], lf_ref.shape[2]
    rr = lax.broadcasted_iota(jnp.int32, (blk, blk), 0)
    cc = lax.broadcasted_iota(jnp.int32, (blk, blk), 1)
    lower = jnp.where(rr >= cc, 1.0, 0.0).astype(BF16)
    upper = jnp.where(rr <= cc, 1.0, 0.0).astype(BF16)
    dot_tn = lambda a, b: lax.dot_general(a, b, (((0,), (0,)), ((), ())),
                                          preferred_element_type=F32)
    blocks = [slice(n * blk, (n + 1) * blk) for n in range(s_len // blk)]
    parts = [_split3(lf_ref[0, rows, :]) for rows in blocks]
    local_col = [sum(_dot(lower, p) for p in ps) for ps in parts]
    local_row = [sum(dot_tn(p, upper) for p in ps) for ps in parts]
    carry_col = jnp.zeros((1, n_heads), F32)
    carry_row = jnp.zeros((n_heads, 1), F32)
    for rows, c, ct in zip(blocks, local_col, local_row):
        col_ref[0, rows, :] = c + carry_col
        row_ref[0, :, rows] = ct + carry_row
        carry_col = carry_col + c[blk - 1:blk, :]
        carry_row = carry_row + ct[:, blk - 1:blk]


def cumsum_logf(lf):
    b, s_len, n_heads = lf.shape
    kern = functools.partial(_cumsum_kernel, blk=LANES)
    return pl.pallas_call(
        kern, grid=(b,),
        in_specs=[pl.BlockSpec((1, s_len, n_heads), lambda i: (i, 0, 0))],
        out_specs=[pl.BlockSpec((1, s_len, n_heads), lambda i: (i, 0, 0)),
                   pl.BlockSpec((1, n_heads, s_len), lambda i: (i, 0, 0))],
        out_shape=[jax.ShapeDtypeStruct((b, s_len, n_heads), F32),
                   jax.ShapeDtypeStruct((b, n_heads, s_len), F32)],
        compiler_params=_cparams("parallel"), name="cumsum_logf")(lf)


FORGET_SKIP_LOG2 = 152.0
FORGET_BOUND_SLACK = 1.02
FORGET_BOUND_MARGIN = 2.0


def _forget_skip_kernel(qmax_ref, kmax_ref, cum_ref, o_ref, *, heads_per_group):
    n_blk, n_heads = kmax_ref.shape[1], kmax_ref.shape[2]
    blk = cum_ref.shape[1] // n_blk
    kmax = kmax_ref[0]
    k_term = kmax + jnp.max(kmax, axis=0, keepdims=True)
    cum_end = cum_ref[0, pl.ds(blk - 1, n_blk, stride=blk), :]
    cum_start = cum_ref[0, pl.ds(0, n_blk, stride=blk), :]
    chunk_id = lax.broadcasted_iota(jnp.int32, (n_blk, 1), 0)
    head_id = lax.broadcasted_iota(jnp.int32, (n_blk, n_heads), 1)
    row = lax.broadcasted_iota(jnp.int32, o_ref.shape[1:], 0)
    col = lax.broadcasted_iota(jnp.int32, o_ref.shape[1:], 1)
    table = jnp.zeros(o_ref.shape[1:], F32)
    for qi in range(n_blk):
        bound = (FORGET_BOUND_SLACK * qmax_ref[0, qi:qi + 1, :] * k_term
                 + LOG2E * (cum_start[qi:qi + 1, :] - cum_end) + FORGET_BOUND_MARGIN)
        for g in range(n_heads // heads_per_group):
            worst = jnp.max(jnp.where(head_id // heads_per_group == g, bound, NEG_INF),
                            axis=1, keepdims=True)
            needed = jnp.logical_not(worst < -FORGET_SKIP_LOG2) | (chunk_id >= qi)
            first = jnp.min(jnp.where(needed, chunk_id, n_blk).astype(F32), axis=0, keepdims=True)
            table = jnp.where((row == qi) & (col == g), first, table)
    o_ref[0] = table.astype(jnp.int32)


def forget_skip_table(qmax, kmax, cum, *, heads_per_group):
    b, n_blk, n_heads = kmax.shape
    s_len = cum.shape[1]
    blk3 = lambda shape: pl.BlockSpec((1,) + shape, lambda i: (i, 0, 0))
    return pl.pallas_call(
        functools.partial(_forget_skip_kernel, heads_per_group=heads_per_group), grid=(b,),
        in_specs=[blk3((n_blk, n_heads)), blk3((n_blk, n_heads)), blk3((s_len, n_heads))],
        out_specs=blk3((n_blk, LANES)),
        out_shape=jax.ShapeDtypeStruct((b, n_blk, LANES), jnp.int32),
        compiler_params=_cparams("parallel"), name="forget_skip_table")(qmax, kmax, cum)


def _forget_kernel(c0_ref, q_ref, k_ref, v_ref, cq_ref, ck_ref, o_ref, v16, s_scr, p_scr, *,
                   tq, q_offset, max_chunks):
    group = pl.program_id(1)
    qi = pl.program_id(2)
    wg = k_ref.shape[2]
    ck_w = KEY_CHUNK
    ones_lane = _ones_lanes()
    heads = [(pp, j) for pp in range(wg // LANES) for j in range(HEADS_PER_VREG)]
    lanes = [slice(pp * LANES, (pp + 1) * LANES) for pp, _ in heads]

    @pl.when(qi == 0)
    def _():
        v = v_ref[0]
        for j in range(HEADS_PER_VREG):
            v16[j] = _masked_values(v, j)

    qmask = _head_masks((tq, LANES))
    q_pos = q_offset + qi * tq + lax.broadcasted_iota(jnp.int32, (tq, ck_w), 0)
    k_iota = lax.broadcasted_iota(jnp.int32, (tq, ck_w), 1)
    cq_all = cq_ref[0]
    head_lane = lax.broadcasted_iota(jnp.int32, cq_all.shape, 1)
    n_c = (q_offset + (qi + 1) * tq + ck_w - 1) // ck_w
    step = (pl.program_id(0) * pl.num_programs(1) + group) * pl.num_programs(2) + qi
    c0 = jnp.minimum(c0_ref[step], n_c - 1)
    n_proc_here = n_c - c0

    def attend(n_proc):
        cols = [slice(c * ck_w, (c + 1) * ck_w) for c in range(n_proc)]
        k0 = [pl.multiple_of((c0 + c) * ck_w, ck_w) for c in range(n_proc)]
        allowed = k0[-1] + k_iota <= q_pos
        qs = [jnp.where(qmask[j], q_ref[0, :, lanes[n]], jnp.zeros((), BF16))
              for n, (_, j) in enumerate(heads)]
        cq = [LOG2E * jnp.sum(jnp.where(head_lane == group * len(heads) + n, cq_all, 0.0),
                              axis=-1, keepdims=True) for n in range(len(heads))]

        def logits(n, c, mrun):
            s = (_dot_nt(qs[n], k_ref[0, pl.ds(k0[c], ck_w), lanes[n]])
                 - LOG2E * ck_ref[0, 0, n:n + 1, pl.ds(k0[c], ck_w)])
            if c == n_proc - 1:
                s = jnp.where(allowed, s, NEG_INF)
            s_scr[n % 2, :, cols[c]] = s
            for part in range(ck_w // LANES):
                mrun = jnp.maximum(mrun, s[:, part * LANES:(part + 1) * LANES])
            return mrun

        def row_term(n, mrun):
            m = jnp.max(mrun, axis=-1, keepdims=True) + cq[n]
            return cq[n] - m

        def probs(n, c, row):
            p_scr[n % 2, :, cols[c]] = jnp.exp2(s_scr[n % 2, :, cols[c]] + row).astype(BF16)

        def pv(n):
            return _dot(p_scr[n % 2, :, :n_proc * ck_w],
                        v16[heads[n][1], pl.ds(k0[0], n_proc * ck_w), lanes[n]])

        neg = jnp.full((tq, LANES), NEG_INF, F32)
        outs = []
        mrun = functools.reduce(lambda m, c: logits(0, c, m), range(n_proc), neg)
        for n in range(len(heads)):
            row = row_term(n, mrun)
            mrun = neg
            for c in range(n_proc):
                if n + 1 < len(heads):
                    mrun = logits(n + 1, c, mrun)
                probs(n, c, row)
            o = pv(n)
            lane = ones_lane[heads[n][1]]
            outs.append(o / o[:, lane:lane + 1])
        for n in range(0, len(heads), HEADS_PER_VREG):
            o_ref[0, :, lanes[n]] = jnp.where(qmask[0], outs[n], outs[n + 1]).astype(BF16)

    for n_proc in range(1, max_chunks + 1):
        pl.when(n_proc_here == n_proc)(functools.partial(attend, n_proc))


def forget_attn(q, k, v, cum_q, cum_k_rows, skip, *, tq, q_offset):
    b, sq, w = q.shape
    sk = k.shape[1]
    n_heads = cum_q.shape[2]
    ck_w = KEY_CHUNK
    diag = [-(-(q_offset + (t + 1) * tq) // ck_w) - 1 for t in range(sq // tq)]
    assert sk % ck_w == 0 and all((q_offset + t * tq) // ck_w == c for t, c in enumerate(diag))
    max_chunks = max(diag) + 1
    wg = ATTN_PAIRS_PER_STEP * LANES
    heads_per_step = ATTN_PAIRS_PER_STEP * HEADS_PER_VREG
    assert skip.shape == (b, w // wg, sq // tq)
    ck = cum_k_rows.reshape(b, n_heads // heads_per_step, heads_per_step, sk)
    kern = functools.partial(_forget_kernel, tq=tq, q_offset=q_offset, max_chunks=max_chunks)
    grid_spec = pltpu.PrefetchScalarGridSpec(
        num_scalar_prefetch=1, grid=(b, w // wg, sq // tq),
        in_specs=[pl.BlockSpec((1, tq, wg), lambda i, h, t, c0: (i, t, h)),
                  pl.BlockSpec((1, sk, wg), lambda i, h, t, c0: (i, 0, h)),
                  pl.BlockSpec((1, sk, wg), lambda i, h, t, c0: (i, 0, h)),
                  pl.BlockSpec((1, tq, n_heads), lambda i, h, t, c0: (i, t, 0)),
                  pl.BlockSpec((1, 1, heads_per_step, sk), lambda i, h, t, c0: (i, h, 0, 0))],
        out_specs=pl.BlockSpec((1, tq, wg), lambda i, h, t, c0: (i, t, h)),
        scratch_shapes=[pltpu.VMEM((HEADS_PER_VREG, sk, wg), BF16),
                        pltpu.VMEM((2, tq, max_chunks * ck_w), F32),
                        pltpu.VMEM((2, tq, max_chunks * ck_w), BF16)])
    return pl.pallas_call(
        kern, grid_spec=grid_spec, out_shape=jax.ShapeDtypeStruct((b, sq, w), BF16),
        compiler_params=_cparams("parallel", "parallel", "arbitrary"),
        name="forget_attn")(skip.reshape(-1), q, k, v, cum_q, ck)


def _layer_tail_kernel(*refs, n_in, n_heads, rows_per_seq, tf, final_norm):
    a_refs, w_refs = refs[:n_in], refs[n_in:2 * n_in]
    (x_ref, gx_ref, wq_ref, wo_ref, mk_ref, mv_ref, gf_ref, wg_ref, wu_ref, wd_ref, gfin_ref,
     o_ref) = refs[2 * n_in:]
    x = x_ref[...]
    for a_ref, w_ref in zip(a_refs, w_refs):
        x = x + _dot(a_ref[...], w_ref[...])

    hd = x.shape[1] // n_heads
    q = (_dot(_rms_bf16(x, gx_ref[...]), wq_ref[...]) * (hd ** -0.5)).astype(BF16)
    per_seq = []
    for b in range(mk_ref.shape[0]):
        rows = slice(b * rows_per_seq, (b + 1) * rows_per_seq)
        outs = []
        for j in range(n_heads):
            cols = slice(j * hd, (j + 1) * hd)
            s = _dot_nt(q[rows, cols], mk_ref[b, :, cols])
            p = jnp.exp(s - jnp.max(s, axis=-1, keepdims=True))
            p = p / jnp.sum(p, axis=-1, keepdims=True)
            outs.append(_dot(p.astype(BF16), mv_ref[b, :, cols]).astype(BF16))
        per_seq.append(jnp.concatenate(outs, axis=-1))
    x = x + _dot(jnp.concatenate(per_seq, axis=0), wo_ref[...])

    h = _rms_bf16(x, gf_ref[...])
    for c in range(wg_ref.shape[1] // tf):
        cols = slice(c * tf, (c + 1) * tf)
        gate = _dot(h, wg_ref[:, cols])
        up = _dot(h, wu_ref[:, cols])
        a = (gate * jax.nn.sigmoid(gate) * up).astype(BF16)
        x = x + _dot(a, wd_ref[cols, :])
    if final_norm:
        x = x * lax.rsqrt(jnp.mean(x * x, axis=-1, keepdims=True) + RMS_EPS) * gfin_ref[...]
    o_ref[...] = x


def layer_tail(a_list, w_list, x, g_x, wq, wo, mk, mv, g_f, wg, wu, wd, g_final, *, seq, tm, tf,
               final_norm):
    m, d = x.shape
    n_mem = mk.shape[1]
    rows_per_seq = min(tm, seq)
    seqs = tm // rows_per_seq
    assert seq % rows_per_seq == 0 and m % tm == 0
    row = lambda i: (i, 0)
    mem = lambda i: (i * tm // (seq * seqs), 0, 0)
    gain = lambda g: g.reshape(1, d)
    kern = functools.partial(_layer_tail_kernel, n_in=len(a_list), n_heads=XA_HEADS,
                             rows_per_seq=rows_per_seq, tf=tf, final_norm=final_norm)
    in_specs = ([pl.BlockSpec((tm, a.shape[1]), row) for a in a_list]
                + [_resident_spec(w.shape) for w in w_list]
                + [pl.BlockSpec((tm, d), row), _resident_spec((1, d)),
                   _resident_spec(wq.shape), _resident_spec(wo.shape),
                   pl.BlockSpec((seqs, n_mem, d), mem), pl.BlockSpec((seqs, n_mem, d), mem),
                   _resident_spec((1, d)), _resident_spec(wg.shape), _resident_spec(wu.shape),
                   _resident_spec(wd.shape), _resident_spec((1, d))])
    return pl.pallas_call(
        kern, grid=(m // tm,), in_specs=in_specs,
        out_specs=pl.BlockSpec((tm, d), row), out_shape=jax.ShapeDtypeStruct((m, d), F32),
        compiler_params=_cparams("parallel"), name="layer_tail")(
            *a_list, *w_list, x, gain(g_x), wq, wo, mk, mv, gain(g_f), wg, wu, wd, gain(g_final))


def _pad_rows(a, rows):
    return jnp.pad(a, ((0, 0), (0, rows - a.shape[1]), (0, 0)))


def _row_tile(m, seq, cap):
    tm = min(cap, seq)
    assert seq % tm == 0 and m % tm == 0
    return tm


def _layer_ab(x, seq, g, w_in, w_out, rel_bias, cache, *, tm):
    m, d = x.shape
    b = m // seq
    wa = w_in.shape[1] // 6
    ws = [w_in[:, n * wa:(n + 1) * wa].astype(BF16) for n in range(6)]
    scale = HEAD_DIM ** -0.5 * LOG2E
    kept = 'tail' if min(A_PAST, seq) < seq else 'heads'
    outs = [(4, 'heads', HEAD_DIM), (4, 'bf16', 1.0), (5, 'heads', HEAD_DIM), (5, 'bf16', 1.0),
            (0, 'bf16', scale), (3, 'bf16', scale), (1, 'bf16', 1.0), (2, 'bf16', 1.0),
            (1, kept, HEAD_DIM), (2, kept, HEAD_DIM)]
    kb, kb16, vb, vb16, qa, qb, ka16, va16, ka_keep, va_keep = norm_proj(x, g, ws, outs, seq=seq,
                                                                         tm=tm)
    shp = lambda a: a.reshape(b, -1, a.shape[-1])
    if cache is None:
        tq = BAND_ROWS
        band = -(-(A_PAST + tq) // KEY_CHUNK) * KEY_CHUNK
        offsets = sorted({t * tq - max(t * (tq // CHUNK) - A_PAST_CHUNKS, 0) * CHUNK
                          for t in range(seq // tq)})
        bias = band_bias(rel_bias, offsets, rows=tq, cols=band, n_real=band)
        oa = band_attn(shp(qa), shp(ka16), shp(va16), bias, tq=tq, band=band)
        ob = stick_attn(shp(qb), shp(kb16), shp(vb16), tq=KEY_CHUNK, q_offset=0)
    else:
        ca_k, ca_v, cb_k, cb_v = cache
        n_past = ca_k.shape[1]
        n_keys = n_past + seq
        band = -(-n_keys // KEY_CHUNK) * KEY_CHUNK
        flat = lambda a: a.reshape(a.shape[0], a.shape[1], -1)
        k_all = _pad_rows(jnp.concatenate([flat(ca_k).astype(BF16), shp(ka16)], axis=1), band)
        v_all = _pad_rows(jnp.concatenate([flat(ca_v).astype(BF16), shp(va16)], axis=1), band)
        bias = band_bias(rel_bias, [n_past], rows=seq, cols=band, n_real=n_keys)
        oa = band_attn(shp(qa), k_all, v_all, bias, tq=seq, band=band)
        n_pastb = cb_k.shape[1]
        sk = -(-(n_pastb + seq) // KEY_CHUNK) * KEY_CHUNK
        kb_all = _pad_rows(jnp.concatenate([flat(cb_k).astype(BF16), shp(kb16)], axis=1), sk)
        vb_all = _pad_rows(jnp.concatenate([flat(cb_v).astype(BF16), shp(vb16)], axis=1), sk)
        ob = stick_attn(shp(qb), kb_all, vb_all, tq=seq, q_offset=n_pastb)
    w_out = w_out.astype(BF16)
    wo_a, wo_b = w_out[:oa.shape[-1]], w_out[oa.shape[-1]:]
    return ([oa.reshape(m, -1), ob.reshape(m, -1)], [wo_a, wo_b]), (ka_keep, va_keep, kb, vb)


def _layer_c(x, seq, g, w_in, b_f, w_out, cache, *, tm):
    m, d = x.shape
    b = m // seq
    n_heads = b_f.shape[0]
    wq, wk, wv, wf = (w_in[:, :d], w_in[:, d:2 * d], w_in[:, 2 * d:3 * d], w_in[:, 3 * d:])
    ws = [w.astype(BF16) for w in (wq, wk, wv, wf)]
    scale = HEAD_DIM ** -0.5 * LOG2E
    outs = [(1, 'heads', HEAD_DIM), (1, 'bf16', 1.0), (2, 'heads', HEAD_DIM), (2, 'bf16', 1.0),
            (0, 'bf16', scale), (3, 'logf', None)]
    if cache is None:
        outs += [(1, 'rowmax', (HEAD_DIM, 1.0)), (0, 'rowmax', (HEAD_DIM, scale))]
    k, k16, v, v16, q, lf, *norms = norm_proj(x, g, ws, outs, seq=seq, tm=tm,
                                              bias=b_f.reshape(1, n_heads))
    norms = norms[::-1]
    shp = lambda a: a.reshape(b, -1, a.shape[-1])
    heads_per_group = ATTN_PAIRS_PER_STEP * HEADS_PER_VREG
    n_groups = n_heads // heads_per_group
    if cache is None:
        cum_col, cum_row = cumsum_logf(shp(lf))
        tq = KEY_CHUNK
        qmax, kmax = (a.reshape(b, seq // KEY_CHUNK, n_heads) for a in norms)
        skip = forget_skip_table(qmax, kmax, cum_col, heads_per_group=heads_per_group)
        skip = jnp.swapaxes(skip[:, :, :n_groups], 1, 2)
        o = forget_attn(shp(q), shp(k16), shp(v16), cum_col, cum_row, skip, tq=tq, q_offset=0)
    else:
        c_k, c_v, c_lf = cache
        n_past = c_k.shape[1]
        sk = -(-(n_past + seq) // KEY_CHUNK) * KEY_CHUNK
        flat = lambda a: a.reshape(a.shape[0], a.shape[1], -1)
        k_all = _pad_rows(jnp.concatenate([flat(c_k).astype(BF16), shp(k16)], axis=1), sk)
        v_all = _pad_rows(jnp.concatenate([flat(c_v).astype(BF16), shp(v16)], axis=1), sk)
        lf_all = _pad_rows(jnp.concatenate([c_lf, shp(lf)], axis=1), sk)
        cum_col, cum_row = cumsum_logf(lf_all)
        o = forget_attn(shp(q), k_all, v_all, cum_col[:, n_past:n_past + seq], cum_row,
                        jnp.zeros((b, n_groups, 1), jnp.int32), tq=seq, q_offset=n_past)
    return ([o.reshape(m, d)], [w_out.astype(BF16)]), (k, v, lf)


def kernel(x_prompt, x_sample, cache_a_k, cache_a_v, cache_b_k, cache_b_v, cache_c_k, cache_c_v, cache_c_logf, cache_mem_k, cache_mem_v, mem_prompt, w_in_ab, w_out_ab, rel_bias_a, w_in_c, b_f_c, w_out_c, g_mix, g_xattn, g_mem, w_xq, w_xk, w_xv, w_xo, g_ffn, w_gate, w_up, w_down, g_final):
    bp, sp, d = x_prompt.shape
    bs, ss, _ = x_sample.shape
    depth = g_mix.shape[0]
    n_mem = mem_prompt.shape[1]
    xp = x_prompt.reshape(bp * sp, d)
    xs = x_sample.reshape(bs * ss, d)
    tmp = _row_tile(bp * sp, sp, ROW_TILE)
    tms = bs * ss
    assert tms <= ROW_TILE
    mem = mem_prompt.reshape(bp * n_mem, d)
    tmm = min(ROW_TILE, bp * n_mem)
    assert (bp * n_mem) % tmm == 0
    dff = w_gate.shape[2]
    tf = 256 if dff % 256 == 0 else dff

    a_kp, a_vp, b_kp, b_vp, a_ks, a_vs, b_ks, b_vs = [], [], [], [], [], [], [], []
    c_kp, c_vp, c_lfp, c_ks, c_vs, c_lfs = [], [], [], [], [], []
    mem_kp, mem_vp, mem_k16, mem_v16 = [], [], [], []
    for layer in range(depth):
        mk, mk16, mv, mv16 = norm_proj(
            mem, g_mem[layer], [w_xk[layer].astype(BF16), w_xv[layer].astype(BF16)],
            [(0, 'heads', d // XA_HEADS), (0, 'bf16', 1.0), (1, 'heads', d // XA_HEADS),
             (1, 'bf16', 1.0)],
            seq=n_mem, tm=tmm)
        mem_kp.append(mk); mem_vp.append(mv)
        mem_k16.append(mk16.reshape(bp, n_mem, d)); mem_v16.append(mv16.reshape(bp, n_mem, d))
    for layer in range(depth):
        if layer % 2 == 0:
            e = layer // 2
            mix_p, (ka, va, kb, vb) = _layer_ab(xp, sp, g_mix[layer], w_in_ab[e], w_out_ab[e],
                                                rel_bias_a[e], None, tm=tmp)
            a_kp.append(ka); a_vp.append(va); b_kp.append(kb); b_vp.append(vb)
            mix_s, (ka, va, kb, vb) = _layer_ab(
                xs, ss, g_mix[layer], w_in_ab[e], w_out_ab[e], rel_bias_a[e],
                (cache_a_k[e], cache_a_v[e], cache_b_k[e], cache_b_v[e]), tm=tms)
            a_ks.append(ka); a_vs.append(va); b_ks.append(kb); b_vs.append(vb)
        else:
            c = layer // 2
            mix_p, (k, v, lf) = _layer_c(xp, sp, g_mix[layer], w_in_c[c], b_f_c[c], w_out_c[c],
                                         None, tm=tmp)
            c_kp.append(k); c_vp.append(v); c_lfp.append(lf)
            mix_s, (k, v, lf) = _layer_c(xs, ss, g_mix[layer], w_in_c[c], b_f_c[c], w_out_c[c],
                                         (cache_c_k[c], cache_c_v[c], cache_c_logf[c]), tm=tms)
            c_ks.append(k); c_vs.append(v); c_lfs.append(lf)
        tail = functools.partial(
            layer_tail, g_x=g_xattn[layer], wq=w_xq[layer].astype(BF16),
            wo=w_xo[layer].astype(BF16), g_f=g_ffn[layer], wg=w_gate[layer].astype(BF16),
            wu=w_up[layer].astype(BF16), wd=w_down[layer].astype(BF16), g_final=g_final, tf=tf,
            final_norm=layer == depth - 1)
        xp = tail(*mix_p, xp, mk=mem_k16[layer], mv=mem_v16[layer], seq=sp, tm=tmp)
        xs = tail(*mix_s, xs, mk=cache_mem_k[layer].reshape(bs, n_mem, d).astype(BF16),
                  mv=cache_mem_v[layer].reshape(bs, n_mem, d).astype(BF16), seq=ss, tm=tms)

    hd = HEAD_DIM
    xa_hd = d // XA_HEADS
    r5 = lambda lst, b, s, dd: jnp.stack([a.reshape(b, s, -1, dd) for a in lst])
    r4 = lambda lst, b, s: jnp.stack([a.reshape(b, s, -1) for a in lst])
    keep = min(A_PAST, sp)
    return (xp.reshape(bp, sp, d), xs.reshape(bs, ss, d),
            r5(a_kp, bp, keep, hd), r5(a_vp, bp, keep, hd), r5(b_kp, bp, sp, hd), r5(b_vp, bp, sp, hd),
            r5(c_kp, bp, sp, hd), r5(c_vp, bp, sp, hd), r4(c_lfp, bp, sp),
            r5(mem_kp, bp, n_mem, xa_hd), r5(mem_vp, bp, n_mem, xa_hd),
            r5(a_ks, bs, ss, hd), r5(a_vs, bs, ss, hd), r5(b_ks, bs, ss, hd), r5(b_vs, bs, ss, hd),
            r5(c_ks, bs, ss, hd), r5(c_vs, bs, ss, hd), r4(c_lfs, bs, ss))
```
